```python
import jax, jax.numpy as jnp
from jax import lax
import numpy as np

D_MODEL = 1024
BATCH = 2
SEQ = 8192
DEPTH = 1

GRID_W = 64
CTX_LEN = 256
D_MIX = 2 * D_MODEL
D_POOL = D_MIX // 2
D_NA = D_MIX - D_POOL
POOL_WINDOWS = (2, 4, 8, 16)
N_POOL_GROUPS = len(POOL_WINDOWS)
POOL_GROUP_DIM = D_POOL // N_POOL_GROUPS
HEAD_DIM = 64
N_HEADS = D_NA // HEAD_DIM
ROPE_AXIS_DIM = HEAD_DIM // 2
NA_ROWS_MAX = 8
NA_COLS = 16
QUERY_BLOCK = 128
ROPE_BASE = 10000.0
LN_EPS = 1e-6
DEEPNORM_ALPHA = (2.0 * DEPTH) ** 0.25
DEEPNORM_BETA = (8.0 * DEPTH) ** -0.25
D_IN = 2 * D_POOL + 4 * D_NA
SPLIT_POINTS = [D_POOL, 2 * D_POOL, 2 * D_POOL + D_NA, 2 * D_POOL + 2 * D_NA, 2 * D_POOL + 3 * D_NA]

kernel_name = "hybrid_pool_natten_dit_layer"


def layer_norm(x):
    x32 = x.astype(jnp.float32)
    mu = jnp.mean(x32, axis=-1, keepdims=True)
    var = jnp.mean(jnp.square(x32 - mu), axis=-1, keepdims=True)
    return ((x32 - mu) * lax.rsqrt(var + LN_EPS)).astype(x.dtype)


def ada_mod(cvec, w_ada, b_ada):
    return jnp.split(jax.nn.silu(cvec) @ w_ada + b_ada, 3, axis=-1)


def modulate(x, shift, scale):
    return layer_norm(x) * (1 + scale) + shift


def split_heads(a):
    return a.reshape(a.shape[:-1] + (N_HEADS, HEAD_DIM))


def rope_axis(x, pos):
    half = x.shape[-1] // 2
    inv_freq = ROPE_BASE ** (-jnp.arange(half, dtype=jnp.float32) / half)
    ang = pos.astype(jnp.float32)[:, None] * inv_freq
    cos = jnp.cos(ang)[:, None, :]
    sin = jnp.sin(ang)[:, None, :]
    x1, x2 = x[..., :half], x[..., half:]
    return jnp.concatenate([x1 * cos - x2 * sin, x1 * sin + x2 * cos], axis=-1).astype(x.dtype)


def rope_2d(x, row, col):
    return jnp.concatenate([rope_axis(x[..., :ROPE_AXIS_DIM], row),
                            rope_axis(x[..., ROPE_AXIS_DIM:], col)], axis=-1)


def multiscale_pool(u, w_pool, pool_scale):
    b, L, _ = u.shape
    ug = u.reshape(b, L, N_POOL_GROUPS, POOL_GROUP_DIM)
    cs = jnp.concatenate([jnp.zeros((b, 1, N_POOL_GROUPS, POOL_GROUP_DIM), jnp.float32),
                          jnp.cumsum(ug.astype(jnp.float32), axis=1)], axis=1)
    t = jnp.arange(L)
    means = []
    for g, w in enumerate(POOL_WINDOWS):
        lo = jnp.clip(t - w // 2, 0, L)
        hi = jnp.clip(t - w // 2 + w, 0, L)
        cs_g = cs[:, :, g]
        means.append((cs_g[:, hi] - cs_g[:, lo]) / (hi - lo).astype(jnp.float32)[None, :, None])
    pooled = jnp.stack(means, axis=2).astype(u.dtype) - ug
    y = jnp.einsum('blgc,gcd->blgd', pooled, w_pool)
    return y.reshape(b, L, D_POOL) * pool_scale


def neighbourhood_indices(L):
    rows = L // GRID_W
    kr = min(NA_ROWS_MAX, rows)
    t = jnp.arange(L)
    row, col = t // GRID_W, t % GRID_W
    rs = jnp.clip(row - kr // 2, 0, rows - kr)
    cs = jnp.clip(col - NA_COLS // 2, 0, GRID_W - NA_COLS)
    key_r = rs[:, None] + jnp.arange(kr)
    key_c = cs[:, None] + jnp.arange(NA_COLS)
    idx = (key_r[:, :, None] * GRID_W + key_c[:, None, :]).reshape(L, kr * NA_COLS)
    dr = jnp.broadcast_to((key_r - row[:, None] + NA_ROWS_MAX - 1)[:, :, None],
                          (L, kr, NA_COLS)).reshape(L, kr * NA_COLS)
    dc = jnp.broadcast_to((key_c - col[:, None] + NA_COLS - 1)[:, None, :],
                          (L, kr, NA_COLS)).reshape(L, kr * NA_COLS)
    return row, col, idx, dr, dc


def neighbourhood_attention(q, k, v, k_ctx, v_ctx, rpb, row, col, idx, dr, dc):
    b, L, h, dh = q.shape
    scale = dh ** -0.5
    q_rot = rope_2d(q, row, col)
    k_rot = rope_2d(k, row, col)
    nb = L // QUERY_BLOCK
    to_blocks = lambda a: jnp.moveaxis(a.reshape((b, nb, QUERY_BLOCK) + a.shape[2:]), 1, 0)
    idx_blocks = lambda a: a.reshape((nb, QUERY_BLOCK) + a.shape[1:])

    def one_block(args):
        qr, qp, ib, drb, dcb = args
        kg = k_rot[:, ib]
        vg = v[:, ib]
        s_loc = (jnp.einsum('bqhd,bqkhd->bhqk', qr, kg).astype(jnp.float32) * scale
                 + rpb[:, drb, dcb][None].astype(jnp.float32))
        s_ctx = jnp.einsum('bqhd,bchd->bhqc', qp, k_ctx).astype(jnp.float32) * scale
        p = jax.nn.softmax(jnp.concatenate([s_loc, s_ctx], axis=-1), axis=-1).astype(v.dtype)
        n_loc = ib.shape[1]
        return (jnp.einsum('bhqk,bqkhd->bqhd', p[..., :n_loc], vg)
                + jnp.einsum('bhqc,bchd->bqhd', p[..., n_loc:], v_ctx))

    out = lax.map(one_block, (to_blocks(q_rot), to_blocks(q), idx_blocks(idx),
                              idx_blocks(dr), idx_blocks(dc)))
    return jnp.moveaxis(out, 0, 1).reshape(b, L, h * dh)


def context_attention(q, k, v):
    b, C, h, dh = q.shape
    s = jnp.einsum('bqhd,bkhd->bhqk', q, k).astype(jnp.float32) * dh ** -0.5
    p = jax.nn.softmax(s, axis=-1).astype(v.dtype)
    return jnp.einsum('bhqk,bkhd->bqhd', p, v).reshape(b, C, h * dh)


def setup_inputs(seed: int = 0) -> dict:
    key = jax.random.key(seed)
    ks = jax.random.split(key, 15)
    f32 = jnp.float32
    nrm = lambda k, shape: jax.random.normal(k, shape, f32)
    col_scale = jnp.concatenate([
        jnp.full((D_POOL,), DEEPNORM_BETA, f32),
        jnp.ones((D_POOL,), f32),
        jnp.ones((2 * D_NA,), f32),
        jnp.full((D_NA,), DEEPNORM_BETA, f32),
        jnp.ones((D_NA,), f32)])
    return {
        "x": nrm(ks[0], (BATCH, SEQ, D_MODEL)),
        "c": nrm(ks[1], (BATCH, D_MODEL)),
        "ctx": nrm(ks[2], (BATCH, CTX_LEN, D_MODEL)),
        "c_ctx": nrm(ks[3], (D_MODEL,)),
        "w_ada": nrm(ks[4], (DEPTH, D_MODEL, 3 * D_MODEL)) * (0.5 * D_MODEL ** -0.5),
        "b_ada": 0.01 * nrm(ks[5], (DEPTH, 3 * D_MODEL)),
        "w_in": nrm(ks[6], (DEPTH, D_MODEL, D_IN)) * (D_MODEL ** -0.5) * col_scale,
        "b_in": 0.01 * nrm(ks[7], (DEPTH, D_IN)),
        "w_pool": nrm(ks[8], (DEPTH, N_POOL_GROUPS, POOL_GROUP_DIM, POOL_GROUP_DIM)) * POOL_GROUP_DIM ** -0.5,
        "pool_scale": 1.0 + 0.1 * nrm(ks[9], (DEPTH, D_POOL)),
        "rpb": 0.1 * nrm(ks[10], (DEPTH, N_HEADS, 2 * NA_ROWS_MAX - 1, 2 * NA_COLS - 1)),
        "w_out": nrm(ks[11], (DEPTH, D_MIX, D_MODEL)) * (D_MIX ** -0.5) * DEEPNORM_BETA,
        "b_out": 0.01 * nrm(ks[12], (DEPTH, D_MODEL)),
        "ln_g": 1.0 + 0.05 * nrm(ks[13], (DEPTH, D_MODEL)),
        "ln_b": 0.01 * nrm(ks[14], (DEPTH, D_MODEL)),
    }


def reference(x, c, ctx, c_ctx, w_ada, b_ada, w_in, b_in, w_pool, pool_scale, rpb,
              w_out, b_out, ln_g, ln_b):
    L = x.shape[1]
    row, col, idx, dr, dc = neighbourhood_indices(L)
    ctx_s = ctx
    for i in range(DEPTH):
        last = i == DEPTH - 1
        sh_x, sc_x, g_x = ada_mod(c, w_ada[i], b_ada[i])
        sh_c, sc_c, g_c = ada_mod(c_ctx, w_ada[i], b_ada[i])
        hx = modulate(x, sh_x[:, None], sc_x[:, None])
        hc = modulate(ctx_s, sh_c, sc_c)
        ux, zpx, qx, kx, vx, zax = jnp.split(hx @ w_in[i] + b_in[i], SPLIT_POINTS, axis=-1)
        uc, zpc, qc, kc, vc, zac = jnp.split(hc @ w_in[i] + b_in[i], SPLIT_POINTS, axis=-1)
        k_ctx, v_ctx = split_heads(kc), split_heads(vc)
        pool_x = multiscale_pool(ux, w_pool[i], pool_scale[i]) * jax.nn.silu(zpx)
        na_x = neighbourhood_attention(split_heads(qx), split_heads(kx), split_heads(vx),
                                       k_ctx, v_ctx, rpb[i], row, col, idx, dr, dc) * jax.nn.silu(zax)
        y_x = jnp.concatenate([pool_x, na_x], axis=-1) @ w_out[i] + b_out[i]
        x_next = layer_norm(DEEPNORM_ALPHA * x + g_x[:, None] * y_x) * ln_g[i] + ln_b[i]
        if not last:
            pool_c = multiscale_pool(uc, w_pool[i], pool_scale[i]) * jax.nn.silu(zpc)
            na_c = context_attention(split_heads(qc), k_ctx, v_ctx) * jax.nn.silu(zac)
            y_c = jnp.concatenate([pool_c, na_c], axis=-1) @ w_out[i] + b_out[i]
            ctx_s = layer_norm(DEEPNORM_ALPHA * ctx_s + g_c * y_c) * ln_g[i] + ln_b[i]
        x = x_next
    return x
```

```python
import functools

import numpy as np
import jax
import jax.numpy as jnp
from jax import lax
from jax.experimental import pallas as pl
from jax.experimental.pallas import tpu as pltpu

GRID_W = 64
POOL_WINDOWS = (2, 4, 8, 16)
HEAD_DIM = 64
NA_ROWS = 8
NA_COLS = 16
ROPE_BASE = 10000.0
LN_EPS = 1e-6
DEPTH = 1
DEEPNORM_ALPHA = (2.0 * DEPTH) ** 0.25

LANES = 128
VMEM_LIMIT_BYTES = 52 * 1024 * 1024

QUERY_ROWS = 2
QB = QUERY_ROWS * GRID_W
WIN_CHUNKS = 5
WIN = WIN_CHUNKS * QB
POOL_HALO = 8
NEG = -1e30
INPROJ_TM = 512

F32 = jnp.float32
BF16 = jnp.bfloat16


def _silu(x):
    return x * jax.nn.sigmoid(x)


def _layer_norm(x):
    mu = jnp.mean(x, axis=-1, keepdims=True)
    xc = x - mu
    var = jnp.mean(xc * xc, axis=-1, keepdims=True)
    return xc * lax.rsqrt(var + LN_EPS)


def _ada_kernel(c_ref, w_ref, b_ref, o_ref):
    s = _silu(c_ref[...]).astype(BF16)
    o_ref[...] = jnp.dot(s, w_ref[...].astype(BF16), preferred_element_type=F32) + b_ref[...]


def _ada(cvec, w_ada, b_ada):
    rows, d = cvec.shape
    n = w_ada.shape[1]
    tn = 768
    return pl.pallas_call(
        _ada_kernel,
        grid=(n // tn,),
        in_specs=[pl.BlockSpec((rows, d), lambda j: (0, 0)),
                  pl.BlockSpec((d, tn), lambda j: (0, j)),
                  pl.BlockSpec((1, tn), lambda j: (0, j))],
        out_specs=pl.BlockSpec((rows, tn), lambda j: (0, j)),
        out_shape=jax.ShapeDtypeStruct((rows, n), F32),
        compiler_params=pltpu.CompilerParams(dimension_semantics=("arbitrary",),
                                             vmem_limit_bytes=VMEM_LIMIT_BYTES),
        name="ada",
    )(cvec, w_ada, b_ada)


def _rope(x, cos, sin):
    first_half = (lax.broadcasted_iota(jnp.int32, (1, LANES), 1) % 32) < 16
    outs = []
    for c in range(x.shape[1] // LANES):
        xc = x[:, c * LANES:(c + 1) * LANES]
        partner = jnp.where(first_half, pltpu.roll(xc, LANES - 16, axis=1), pltpu.roll(xc, 16, axis=1))
        outs.append(xc * cos + partner * sin)
    return jnp.concatenate(outs, axis=1)


def _inproj_kernel(kinds, has_rope, *refs):
    n_in = 6 if has_rope else 4
    x_ref, mod_ref, w_ref, b_ref = refs[:4]
    cos_ref, sin_ref = (refs[4], refs[5]) if has_rope else (None, None)
    n_out = sum(2 if k == "q" else 1 for k in kinds)
    out_refs = refs[n_in:n_in + n_out]
    h_ref = refs[n_in + n_out]
    d = x_ref.shape[1]
    j = pl.program_id(1)

    @pl.when(j == 0)
    def _():
        mod = mod_ref[0]
        h = _layer_norm(x_ref[...]) * (1.0 + mod[:, d:2 * d]) + mod[:, :d]
        h_ref[...] = h.astype(BF16)

    acc = jnp.dot(h_ref[...], w_ref[...], preferred_element_type=F32) + b_ref[...]

    o = 0
    for jj, kind in enumerate(kinds):
        if kind == "q":
            qr_ref, q_ref = out_refs[o], out_refs[o + 1]
            o += 2

            @pl.when(j == jj)
            def _(qr_ref=qr_ref, q_ref=q_ref):
                qs = acc * (HEAD_DIM ** -0.5)
                q_ref[...] = qs.astype(q_ref.dtype)
                qr_ref[...] = _rope(qs, cos_ref[...], sin_ref[...]).astype(qr_ref.dtype)
        else:
            out_ref = out_refs[o]
            o += 1

            @pl.when(j == jj)
            def _(out_ref=out_ref, kind=kind):
                val = _rope(acc, cos_ref[...], sin_ref[...]) if kind == "k" else acc
                out_ref[...] = val.astype(out_ref.dtype)


def _inproj(x2d, mod3, mod_rows_per_tile, w_bf16, b_in, col_blocks, kinds, out_dtypes, rope, tm):
    t, d = x2d.shape
    nt = t // tm
    has_rope = rope is not None
    col_blocks = tuple(col_blocks)
    col0 = col_blocks[0]
    assert col_blocks == tuple(range(col0, col0 + len(col_blocks)))
    in_specs = [
        pl.BlockSpec((tm, d), lambda i, j: (i, 0)),
        pl.BlockSpec((1, 1, mod3.shape[2]), lambda i, j: (mod_rows_per_tile(i), 0, 0)),
        pl.BlockSpec((d, d), lambda i, j: (0, col0 + j)),
        pl.BlockSpec((1, d), lambda i, j: (0, col0 + j)),
    ]
    args = [x2d, mod3, w_bf16, b_in]
    if has_rope:
        tiles_per_seq = rope[0].shape[0] // tm
        for tab in rope:
            in_specs.append(pl.BlockSpec((tm, LANES), lambda i, j: (i % tiles_per_seq, 0)))
            args.append(tab)
    out_specs = [pl.BlockSpec((tm, d), lambda i, j: (i, 0)) for _ in out_dtypes]
    out_shape = [jax.ShapeDtypeStruct((t, d), dt) for dt in out_dtypes]
    return pl.pallas_call(
        functools.partial(_inproj_kernel, tuple(kinds), has_rope),
        grid=(nt, len(kinds)),
        in_specs=in_specs,
        out_specs=out_specs,
        out_shape=out_shape,
        scratch_shapes=[pltpu.VMEM((tm, d), BF16)],
        compiler_params=pltpu.CompilerParams(dimension_semantics=("arbitrary", "arbitrary"),
                                             vmem_limit_bytes=VMEM_LIMIT_BYTES),
        name="inproj_rope" if has_rope else "inproj_ctx",
    )(*args)


def _rope_tables(seq):
    half = HEAD_DIM // 4
    inv_freq = ROPE_BASE ** (-np.arange(half, dtype=np.float64) / half)
    t = np.arange(seq)
    row, col = t // GRID_W, t % GRID_W
    lane = np.arange(LANES)
    pos = np.where((lane % HEAD_DIM < HEAD_DIM // 2)[None, :], row[:, None], col[:, None]).astype(np.float64)
    ang = pos * inv_freq[lane % half][None, :]
    sign = np.where(lane % (2 * half) < half, -1.0, 1.0)[None, :]
    return jnp.asarray(np.cos(ang), F32), jnp.asarray(np.sin(ang) * sign, F32)


def _window_geometry(n_blocks, n_rows):
    cases = {}
    for blk in range(n_blocks):
        ws = min(max(blk - (WIN_CHUNKS // 2), 0), n_blocks - WIN_CHUNKS)
        v = blk - ws
        geo = []
        for a in range(QUERY_ROWS):
            r = blk * QUERY_ROWS + a
            rs = min(max(r - NA_ROWS // 2, 0), n_rows - NA_ROWS)
            wr0 = ws * QUERY_ROWS
            geo.append((rs - wr0, rs - wr0 + NA_ROWS, wr0 - r + NA_ROWS - 1))
        geo = tuple(geo)
        assert cases.setdefault(v, geo) == geo
    assert sorted(cases) == list(range(WIN_CHUNKS))
    return [cases[v] for v in range(WIN_CHUNKS)]


def _bias_kernel(geometry, rpb_ref, o_ref):
    table = rpb_ref[0]
    lane = lax.broadcasted_iota(jnp.int32, (GRID_W, LANES), 1)
    qc = lax.broadcasted_iota(jnp.int32, (GRID_W, LANES), 0)
    kc = lane % GRID_W
    cs = jnp.clip(qc - NA_COLS // 2, 0, GRID_W - NA_COLS)
    col_ok = (kc >= cs) & (kc < cs + NA_COLS)
    low_half = lane < GRID_W

    @functools.cache
    def shifted(dr, half):
        row = jnp.broadcast_to(table[dr:dr + 1, :], (GRID_W, LANES))
        shift = (LANES - (NA_COLS - 1) + GRID_W * half) % LANES
        return pltpu.roll(row, shift, axis=1, stride=1, stride_axis=0)

    neg = jnp.full((GRID_W, LANES), NEG, F32)
    for v, geo in enumerate(geometry):
        for a, (lo, hi, dr0) in enumerate(geo):
            pairs = []
            for jr in range(0, WIN_CHUNKS * QUERY_ROWS, 2):
                left = shifted(dr0 + jr, 0) if lo <= jr < hi else neg
                right = shifted(dr0 + jr + 1, 1) if lo <= jr + 1 < hi else neg
                pairs.append(jnp.where(col_ok, jnp.where(low_half, left, right), NEG))
            o_ref[v, 0, a * GRID_W:(a + 1) * GRID_W, :] = jnp.concatenate(pairs, axis=1)


def _bias_tables(rpb, n_blocks, n_rows):
    n_heads, n_dr, n_dc = rpb.shape
    rpb_pad = jnp.pad(rpb, ((0, 0), (0, 16 - n_dr), (0, LANES - n_dc)))
    geometry = _window_geometry(n_blocks, n_rows)
    return pl.pallas_call(
        functools.partial(_bias_kernel, geometry),
        grid=(n_heads,),
        in_specs=[pl.BlockSpec((1, 16, LANES), lambda h: (h, 0, 0))],
        out_specs=pl.BlockSpec((WIN_CHUNKS, 1, QB, WIN), lambda h: (0, h, 0, 0)),
        out_shape=jax.ShapeDtypeStruct((WIN_CHUNKS, n_heads, QB, WIN), F32),
        compiler_params=pltpu.CompilerParams(dimension_semantics=("arbitrary",),
                                             vmem_limit_bytes=VMEM_LIMIT_BYTES),
        name="bias_tables",
    )(rpb_pad)


def _mix_kernel(seq, x_ref, mod_ref, u_ref, up_ref, un_ref, zp_ref, za_ref, qr_ref, q_ref, *rest):
    k_refs = rest[:WIN_CHUNKS]
    v_refs = rest[WIN_CHUNKS:2 * WIN_CHUNKS]
    (kc_ref, vc_ref, bias_ref, wp_ref, ps_ref, wo_ref, bo_ref, g_ref, be_ref,
     o_ref, cat_ref) = rest[2 * WIN_CHUNKS:]
    d = x_ref.shape[1]
    d_pool = ps_ref.shape[1]
    gdim = d_pool // len(POOL_WINDOWS)
    n_heads = (cat_ref.shape[1] - d_pool) // HEAD_DIM
    i = pl.program_id(1)
    n_blk = pl.num_programs(1)

    u = u_ref[...]
    prev = jnp.where(i > 0, up_ref[...], 0.0)
    nxt = jnp.where(i < n_blk - 1, un_ref[...], 0.0)
    ext = jnp.concatenate([prev, u, nxt], axis=0)
    n_ext = ext.shape[0]
    t = i * QB + lax.broadcasted_iota(jnp.int32, (QB, 1), 0)

    def shift_sum(s, k):
        return pltpu.roll(s, k, axis=0) + pltpu.roll(s, n_ext - k, axis=0)

    for g, w in enumerate(POOL_WINDOWS):
        cols = slice(g * gdim, (g + 1) * gdim)
        s = ext[:, cols]
        s = s + pltpu.roll(s, 1, axis=0)
        k = 1
        while 2 * k < w:
            s = shift_sum(s, k)
            k *= 2
        s = s[POOL_HALO:POOL_HALO + QB]
        lo = jnp.clip(t - w // 2, 0, seq)
        hi = jnp.clip(t - w // 2 + w, 0, seq)
        pooled = s / (hi - lo).astype(F32) - u[:, cols]
        y = jnp.dot(pooled.astype(BF16), wp_ref[g], preferred_element_type=F32)
        y = y * ps_ref[:, cols] * _silu(zp_ref[:, cols])
        cat_ref[:, cols] = y.astype(BF16)

    nt_dims = (((1,), (1,)), ((), ()))
    for h in range(n_heads):
        cols = slice(h * HEAD_DIM, (h + 1) * HEAD_DIM)
        k_win = jnp.concatenate([r[:, cols] for r in k_refs], axis=0)
        v_win = jnp.concatenate([r[:, cols] for r in v_refs], axis=0)
        s_loc = lax.dot_general(qr_ref[:, cols], k_win, nt_dims, preferred_element_type=F32)
        s_loc = s_loc + bias_ref[0, h]
        s_ctx = lax.dot_general(q_ref[:, cols], kc_ref[:, cols], nt_dims, preferred_element_type=F32)
        m = jnp.maximum(jnp.max(s_loc, axis=1, keepdims=True), jnp.max(s_ctx, axis=1, keepdims=True))
        p_loc = jnp.exp(s_loc - m)
        p_ctx = jnp.exp(s_ctx - m)
        denom = jnp.sum(p_loc, axis=1, keepdims=True) + jnp.sum(p_ctx, axis=1, keepdims=True)
        o = (jnp.dot(p_loc.astype(BF16), v_win, preferred_element_type=F32)
             + jnp.dot(p_ctx.astype(BF16), vc_ref[:, cols], preferred_element_type=F32))
        o = o / denom * _silu(za_ref[:, cols])
        cat_ref[:, d_pool + h * HEAD_DIM:d_pool + (h + 1) * HEAD_DIM] = o.astype(BF16)

    y = jnp.dot(cat_ref[...], wo_ref[...], preferred_element_type=F32) + bo_ref[...]
    gate = mod_ref[0][:, 2 * d:3 * d]
    z = DEEPNORM_ALPHA * x_ref[...] + gate * y
    o_ref[...] = _layer_norm(z) * g_ref[...] + be_ref[...]


def _mix(x2d, mod3, u, zp, za, qr, q, kr, v, kc, vc, bias, w_pool, pool_scale, w_out, b_out, ln_g, ln_b,
         batch, seq, ctx_len):
    t, d = x2d.shape
    n_blk = seq // QB
    halo_per_blk = QB // POOL_HALO
    n_halo = t // POOL_HALO
    d_mix = w_out.shape[0]

    def win_start(i):
        return jnp.clip(i - WIN_CHUNKS // 2, 0, n_blk - WIN_CHUNKS)

    def tok(b, i):
        return (b * n_blk + i, 0)

    def win_spec(j):
        return pl.BlockSpec((QB, d), lambda b, i: (b * n_blk + win_start(i) + j, 0))

    full = lambda shape: pl.BlockSpec(shape, lambda b, i: (0,) * len(shape))
    in_specs = [
        pl.BlockSpec((QB, d), tok),
        pl.BlockSpec((1, 1, mod3.shape[2]), lambda b, i: (b, 0, 0)),
        pl.BlockSpec((QB, d), tok),
        pl.BlockSpec((POOL_HALO, d),
                     lambda b, i: (jnp.maximum((b * n_blk + i) * halo_per_blk - 1, 0), 0)),
        pl.BlockSpec((POOL_HALO, d),
                     lambda b, i: (jnp.minimum((b * n_blk + i + 1) * halo_per_blk, n_halo - 1), 0)),
        pl.BlockSpec((QB, d), tok),
        pl.BlockSpec((QB, d), tok),
        pl.BlockSpec((QB, d), tok),
        pl.BlockSpec((QB, d), tok),
    ]
    in_specs += [win_spec(j) for j in range(WIN_CHUNKS)]
    in_specs += [win_spec(j) for j in range(WIN_CHUNKS)]
    in_specs += [
        pl.BlockSpec((ctx_len, d), lambda b, i: (b, 0)),
        pl.BlockSpec((ctx_len, d), lambda b, i: (b, 0)),
        pl.BlockSpec((1,) + bias.shape[1:], lambda b, i: (i - win_start(i), 0, 0, 0)),
        full(w_pool.shape), full(pool_scale.shape), full(w_out.shape),
        full(b_out.shape), full(ln_g.shape), full(ln_b.shape),
    ]
    args = [x2d, mod3, u, u, u, zp, za, qr, q] + [kr] * WIN_CHUNKS + [v] * WIN_CHUNKS + [
        kc, vc, bias, w_pool, pool_scale, w_out, b_out, ln_g, ln_b]
    return pl.pallas_call(
        functools.partial(_mix_kernel, seq),
        grid=(batch, n_blk),
        in_specs=in_specs,
        out_specs=pl.BlockSpec((QB, d), tok),
        out_shape=jax.ShapeDtypeStruct((t, d), F32),
        scratch_shapes=[pltpu.VMEM((QB, d_mix), BF16)],
        compiler_params=pltpu.CompilerParams(dimension_semantics=("arbitrary", "arbitrary"),
                                             vmem_limit_bytes=VMEM_LIMIT_BYTES),
        name="mix",
    )(*args)


def kernel(x, c, ctx, c_ctx, w_ada, b_ada, w_in, b_in, w_pool, pool_scale, rpb, w_out, b_out, ln_g, ln_b):
    batch, seq, d = x.shape
    ctx_len = ctx.shape[1]
    assert w_ada.shape[0] == DEPTH == 1
    assert seq % GRID_W == 0 and seq % INPROJ_TM == 0 and (batch * ctx_len) % INPROJ_TM == 0
    n_rows = seq // GRID_W
    assert n_rows >= NA_ROWS and seq // QB >= WIN_CHUNKS
    d_pool = pool_scale.shape[1]
    assert d_pool == d and w_in.shape[2] == 6 * d and rpb.shape[1] * HEAD_DIM == d

    cvec = jnp.zeros((8, d), F32).at[:batch].set(c).at[batch].set(c_ctx)
    mod3 = _ada(cvec, w_ada[0], b_ada[0][None, :]).reshape(8, 1, 3 * d)

    w_in_b = w_in[0].astype(BF16)
    b_in2 = b_in[0][None, :]
    x2d = x.reshape(batch * seq, d)
    tiles_per_seq = seq // INPROJ_TM
    u, zp, qr, q, kr, v, za = _inproj(
        x2d, mod3, lambda i: i // tiles_per_seq, w_in_b, b_in2, range(6),
        ("f32", "f32", "q", "k", "bf16", "f32"), (F32, F32, BF16, BF16, BF16, BF16, F32),
        _rope_tables(seq), INPROJ_TM)
    kc, vc = _inproj(
        ctx.reshape(batch * ctx_len, d), mod3, lambda i: batch, w_in_b, b_in2, (3, 4),
        ("bf16", "bf16"), (BF16, BF16), None, INPROJ_TM)

    bias = _bias_tables(rpb[0], seq // QB, n_rows)
    out = _mix(x2d, mod3, u, zp, za, qr, q, kr, v, kc, vc, bias,
               w_pool[0].astype(BF16), pool_scale, w_out[0].astype(BF16), b_out, ln_g, ln_b,
               batch, seq, ctx_len)
    return out.reshape(batch, seq, d)
```

```python
import functools

import numpy as np
import jax
import jax.numpy as jnp
from jax import lax
from jax.experimental import pallas as pl
from jax.experimental.pallas import tpu as pltpu

GRID_W = 64
POOL_WINDOWS = (2, 4, 8, 16)
HEAD_DIM = 64
NA_ROWS = 8
NA_COLS = 16
ROPE_BASE = 10000.0
LN_EPS = 1e-6
DEPTH = 1
DEEPNORM_ALPHA = (2.0 * DEPTH) ** 0.25

LANES = 128
VMEM_LIMIT_BYTES = 52 * 1024 * 1024

QUERY_ROWS = 2
QB = QUERY_ROWS * GRID_W
WIN_CHUNKS = 5
WIN = WIN_CHUNKS * QB
POOL_HALO = 8
NEG = -1e30
INPROJ_TM = 512
PAIR_GROUP = 2

F32 = jnp.float32
BF16 = jnp.bfloat16


def _silu(x):
    return x * jax.nn.sigmoid(x)


def _layer_norm(x):
    mu = jnp.mean(x, axis=-1, keepdims=True)
    xc = x - mu
    var = jnp.mean(xc * xc, axis=-1, keepdims=True)
    return xc * lax.rsqrt(var + LN_EPS)


def _ada_kernel(c_ref, w_ref, b_ref, o_ref):
    s = _silu(c_ref[...]).astype(BF16)
    o_ref[...] = jnp.dot(s, w_ref[...].astype(BF16), preferred_element_type=F32) + b_ref[...]


def _ada(cvec, w_ada, b_ada):
    rows, d = cvec.shape
    n = w_ada.shape[1]
    tn = 768
    return pl.pallas_call(
        _ada_kernel,
        grid=(n // tn,),
        in_specs=[pl.BlockSpec((rows, d), lambda j: (0, 0)),
                  pl.BlockSpec((d, tn), lambda j: (0, j)),
                  pl.BlockSpec((1, tn), lambda j: (0, j))],
        out_specs=pl.BlockSpec((rows, tn), lambda j: (0, j)),
        out_shape=jax.ShapeDtypeStruct((rows, n), F32),
        compiler_params=pltpu.CompilerParams(dimension_semantics=("arbitrary",),
                                             vmem_limit_bytes=VMEM_LIMIT_BYTES),
        name="ada",
    )(cvec, w_ada, b_ada)


def _rope(x, cos, sin):
    first_half = (lax.broadcasted_iota(jnp.int32, (1, LANES), 1) % 32) < 16
    outs = []
    for c in range(x.shape[1] // LANES):
        xc = x[:, c * LANES:(c + 1) * LANES]
        partner = jnp.where(first_half, pltpu.roll(xc, LANES - 16, axis=1), pltpu.roll(xc, 16, axis=1))
        outs.append(xc * cos + partner * sin)
    return jnp.concatenate(outs, axis=1)


def _inproj_kernel(kinds, has_rope, *refs):
    n_in = 6 if has_rope else 4
    x_ref, mod_ref, w_ref, b_ref = refs[:4]
    cos_ref, sin_ref = (refs[4], refs[5]) if has_rope else (None, None)
    n_out = sum(2 if k == "q" else 1 for k in kinds)
    out_refs = refs[n_in:n_in + n_out]
    h_ref = refs[n_in + n_out]
    d = x_ref.shape[1]
    j = pl.program_id(1)

    @pl.when(j == 0)
    def _():
        mod = mod_ref[0]
        h = _layer_norm(x_ref[...]) * (1.0 + mod[:, d:2 * d]) + mod[:, :d]
        h_ref[...] = h.astype(BF16)

    acc = jnp.dot(h_ref[...], w_ref[...], preferred_element_type=F32) + b_ref[...]

    o = 0
    for jj, kind in enumerate(kinds):
        if kind == "q":
            qr_ref, q_ref = out_refs[o], out_refs[o + 1]
            o += 2

            @pl.when(j == jj)
            def _(qr_ref=qr_ref, q_ref=q_ref):
                qs = acc * (HEAD_DIM ** -0.5)
                q_ref[...] = qs.astype(q_ref.dtype)
                qr_ref[...] = _rope(qs, cos_ref[...], sin_ref[...]).astype(qr_ref.dtype)
        else:
            out_ref = out_refs[o]
            o += 1

            @pl.when(j == jj)
            def _(out_ref=out_ref, kind=kind):
                val = _rope(acc, cos_ref[...], sin_ref[...]) if kind == "k" else acc
                out_ref[...] = val.astype(out_ref.dtype)


def _inproj(x2d, mod3, mod_rows_per_tile, w_bf16, b_in, col_blocks, kinds, out_dtypes, rope, tm):
    t, d = x2d.shape
    nt = t // tm
    has_rope = rope is not None
    col_blocks = tuple(col_blocks)
    col0 = col_blocks[0]
    assert col_blocks == tuple(range(col0, col0 + len(col_blocks)))
    in_specs = [
        pl.BlockSpec((tm, d), lambda i, j: (i, 0)),
        pl.BlockSpec((1, 1, mod3.shape[2]), lambda i, j: (mod_rows_per_tile(i), 0, 0)),
        pl.BlockSpec((d, d), lambda i, j: (0, col0 + j)),
        pl.BlockSpec((1, d), lambda i, j: (0, col0 + j)),
    ]
    args = [x2d, mod3, w_bf16, b_in]
    if has_rope:
        tiles_per_seq = rope[0].shape[0] // tm
        for tab in rope:
            in_specs.append(pl.BlockSpec((tm, LANES), lambda i, j: (i % tiles_per_seq, 0)))
            args.append(tab)
    out_specs = [pl.BlockSpec((tm, d), lambda i, j: (i, 0)) for _ in out_dtypes]
    out_shape = [jax.ShapeDtypeStruct((t, d), dt) for dt in out_dtypes]
    return pl.pallas_call(
        functools.partial(_inproj_kernel, tuple(kinds), has_rope),
        grid=(nt, len(kinds)),
        in_specs=in_specs,
        out_specs=out_specs,
        out_shape=out_shape,
        scratch_shapes=[pltpu.VMEM((tm, d), BF16)],
        compiler_params=pltpu.CompilerParams(dimension_semantics=("arbitrary", "arbitrary"),
                                             vmem_limit_bytes=VMEM_LIMIT_BYTES),
        name="inproj_rope" if has_rope else "inproj_ctx",
    )(*args)


def _rope_tables(seq):
    half = HEAD_DIM // 4
    inv_freq = ROPE_BASE ** (-np.arange(half, dtype=np.float64) / half)
    t = np.arange(seq)
    row, col = t // GRID_W, t % GRID_W
    lane = np.arange(LANES)
    pos = np.where((lane % HEAD_DIM < HEAD_DIM // 2)[None, :], row[:, None], col[:, None]).astype(np.float64)
    ang = pos * inv_freq[lane % half][None, :]
    sign = np.where(lane % (2 * half) < half, -1.0, 1.0)[None, :]
    return jnp.asarray(np.cos(ang), F32), jnp.asarray(np.sin(ang) * sign, F32)


def _window_geometry(n_blocks, n_rows):
    cases = {}
    for blk in range(n_blocks):
        ws = min(max(blk - (WIN_CHUNKS // 2), 0), n_blocks - WIN_CHUNKS)
        v = blk - ws
        geo = []
        for a in range(QUERY_ROWS):
            r = blk * QUERY_ROWS + a
            rs = min(max(r - NA_ROWS // 2, 0), n_rows - NA_ROWS)
            wr0 = ws * QUERY_ROWS
            geo.append((rs - wr0, rs - wr0 + NA_ROWS, wr0 - r + NA_ROWS - 1))
        geo = tuple(geo)
        assert cases.setdefault(v, geo) == geo
    assert sorted(cases) == list(range(WIN_CHUNKS))
    return [cases[v] for v in range(WIN_CHUNKS)]


def _bias_kernel(geometry, rpb_ref, o_ref):
    table = rpb_ref[0]
    lane = lax.broadcasted_iota(jnp.int32, (GRID_W, LANES), 1)
    qc = lax.broadcasted_iota(jnp.int32, (GRID_W, LANES), 0)
    kc = lane % GRID_W
    cs = jnp.clip(qc - NA_COLS // 2, 0, GRID_W - NA_COLS)
    col_ok = (kc >= cs) & (kc < cs + NA_COLS)
    low_half = lane < GRID_W

    @functools.cache
    def shifted(dr, half):
        row = jnp.broadcast_to(table[dr:dr + 1, :], (GRID_W, LANES))
        shift = (LANES - (NA_COLS - 1) + GRID_W * half) % LANES
        return pltpu.roll(row, shift, axis=1, stride=1, stride_axis=0)

    neg = jnp.full((GRID_W, LANES), NEG, F32)
    for v, geo in enumerate(geometry):
        for a, (lo, hi, dr0) in enumerate(geo):
            pairs = []
            for jr in range(0, WIN_CHUNKS * QUERY_ROWS, 2):
                left = shifted(dr0 + jr, 0) if lo <= jr < hi else neg
                right = shifted(dr0 + jr + 1, 1) if lo <= jr + 1 < hi else neg
                pairs.append(jnp.where(col_ok, jnp.where(low_half, left, right), NEG))
            o_ref[v, 0, a * GRID_W:(a + 1) * GRID_W, :] = jnp.concatenate(pairs, axis=1)


def _bias_tables(rpb, n_blocks, n_rows):
    n_heads, n_dr, n_dc = rpb.shape
    rpb_pad = jnp.pad(rpb, ((0, 0), (0, 16 - n_dr), (0, LANES - n_dc)))
    geometry = _window_geometry(n_blocks, n_rows)
    return pl.pallas_call(
        functools.partial(_bias_kernel, geometry),
        grid=(n_heads,),
        in_specs=[pl.BlockSpec((1, 16, LANES), lambda h: (h, 0, 0))],
        out_specs=pl.BlockSpec((WIN_CHUNKS, 1, QB, WIN), lambda h: (0, h, 0, 0)),
        out_shape=jax.ShapeDtypeStruct((WIN_CHUNKS, n_heads, QB, WIN), F32),
        compiler_params=pltpu.CompilerParams(dimension_semantics=("arbitrary",),
                                             vmem_limit_bytes=VMEM_LIMIT_BYTES),
        name="bias_tables",
    )(rpb_pad)


def _mix_kernel(seq, x_ref, mod_ref, u_ref, up_ref, un_ref, zp_ref, za_ref, qr_ref, q_ref, *rest):
    k_refs = rest[:WIN_CHUNKS]
    v_refs = rest[WIN_CHUNKS:2 * WIN_CHUNKS]
    (kc_ref, vc_ref, bias_ref, wp_ref, ps_ref, wo_ref, bo_ref, g_ref, be_ref,
     o_ref, cat_ref) = rest[2 * WIN_CHUNKS:]
    d = x_ref.shape[1]
    d_pool = ps_ref.shape[1]
    gdim = d_pool // len(POOL_WINDOWS)
    n_heads = (cat_ref.shape[1] - d_pool) // HEAD_DIM
    i = pl.program_id(1)
    n_blk = pl.num_programs(1)

    u = u_ref[...]
    prev = jnp.where(i > 0, up_ref[...], 0.0)
    nxt = jnp.where(i < n_blk - 1, un_ref[...], 0.0)
    ext = jnp.concatenate([prev, u, nxt], axis=0)
    n_ext = ext.shape[0]
    t = i * QB + lax.broadcasted_iota(jnp.int32, (QB, 1), 0)

    def shift_sum(s, k):
        return pltpu.roll(s, k, axis=0) + pltpu.roll(s, n_ext - k, axis=0)

    for g, w in enumerate(POOL_WINDOWS):
        cols = slice(g * gdim, (g + 1) * gdim)
        s = ext[:, cols]
        s = s + pltpu.roll(s, 1, axis=0)
        k = 1
        while 2 * k < w:
            s = shift_sum(s, k)
            k *= 2
        s = s[POOL_HALO:POOL_HALO + QB]
        lo = jnp.clip(t - w // 2, 0, seq)
        hi = jnp.clip(t - w // 2 + w, 0, seq)
        pooled = s / (hi - lo).astype(F32) - u[:, cols]
        y = jnp.dot(pooled.astype(BF16), wp_ref[g], preferred_element_type=F32)
        y = y * ps_ref[:, cols] * _silu(zp_ref[:, cols])
        cat_ref[:, cols] = y.astype(BF16)

    nt_dims = (((1,), (1,)), ((), ()))
    first_head = lax.broadcasted_iota(jnp.int32, (1, LANES), 1) < HEAD_DIM
    ones_win = jnp.ones((WIN, LANES), BF16)
    ones_ctx = jnp.ones((kc_ref.shape[0], LANES), BF16)

    def stack_heads(qp):
        zero = jnp.zeros_like(qp)
        return jnp.concatenate([jnp.where(first_head, qp, zero), jnp.where(first_head, zero, qp)], axis=0)

    def lane_tiles(a):
        return [a[:, c * LANES:(c + 1) * LANES] for c in range(a.shape[1] // LANES)]

    n_pairs = n_heads // 2
    for g0 in range(0, n_pairs, PAIR_GROUP):
        pairs = range(g0, min(g0 + PAIR_GROUP, n_pairs))
        chunk = {p: slice(p * LANES, (p + 1) * LANES) for p in pairs}
        s_loc, s_ctx, m, p_all, o = {}, {}, {}, {}, {}
        for p in pairs:
            k_win = jnp.concatenate([r[:, chunk[p]] for r in k_refs], axis=0)
            bias = bias_ref[0, 2 * p:2 * p + 2].reshape(2 * QB, WIN)
            s_loc[p] = lax.dot_general(stack_heads(qr_ref[:, chunk[p]]), k_win, nt_dims,
                                       preferred_element_type=F32) + bias
            s_ctx[p] = lax.dot_general(stack_heads(q_ref[:, chunk[p]]), kc_ref[:, chunk[p]], nt_dims,
                                       preferred_element_type=F32)
        for p in pairs:
            tiles = lane_tiles(s_loc[p]) + lane_tiles(s_ctx[p])
            m[p] = jnp.max(functools.reduce(jnp.maximum, tiles), axis=1, keepdims=True)
        for p in pairs:
            p_all[p] = jnp.concatenate([jnp.exp(s_loc[p] - m[p]), jnp.exp(s_ctx[p] - m[p])],
                                       axis=1).astype(BF16)
        for p in pairs:
            v_win = jnp.concatenate([r[:, chunk[p]] for r in v_refs], axis=0)
            v_aug = jnp.concatenate([jnp.concatenate([v_win, ones_win], axis=1),
                                     jnp.concatenate([vc_ref[:, chunk[p]], ones_ctx], axis=1)], axis=0)
            o[p] = jnp.dot(p_all[p], v_aug, preferred_element_type=F32)
        for p in pairs:
            num = jnp.where(first_head, o[p][:QB, :LANES], o[p][QB:, :LANES])
            den = jnp.where(first_head, o[p][:QB, LANES:], o[p][QB:, LANES:])
            out = num / den * _silu(za_ref[:, chunk[p]])
            cat_ref[:, d_pool + p * LANES:d_pool + (p + 1) * LANES] = out.astype(BF16)

    y = jnp.dot(cat_ref[...], wo_ref[...], preferred_element_type=F32) + bo_ref[...]
    gate = mod_ref[0][:, 2 * d:3 * d]
    z = DEEPNORM_ALPHA * x_ref[...] + gate * y
    o_ref[...] = _layer_norm(z) * g_ref[...] + be_ref[...]


def _mix(x2d, mod3, u, zp, za, qr, q, kr, v, kc, vc, bias, w_pool, pool_scale, w_out, b_out, ln_g, ln_b,
         batch, seq, ctx_len):
    t, d = x2d.shape
    n_blk = seq // QB
    halo_per_blk = QB // POOL_HALO
    n_halo = t // POOL_HALO
    d_mix = w_out.shape[0]

    def win_start(i):
        return jnp.clip(i - WIN_CHUNKS // 2, 0, n_blk - WIN_CHUNKS)

    def tok(b, i):
        return (b * n_blk + i, 0)

    def win_spec(j):
        return pl.BlockSpec((QB, d), lambda b, i: (b * n_blk + win_start(i) + j, 0))

    full = lambda shape: pl.BlockSpec(shape, lambda b, i: (0,) * len(shape))
    in_specs = [
        pl.BlockSpec((QB, d), tok),
        pl.BlockSpec((1, 1, mod3.shape[2]), lambda b, i: (b, 0, 0)),
        pl.BlockSpec((QB, d), tok),
        pl.BlockSpec((POOL_HALO, d),
                     lambda b, i: (jnp.maximum((b * n_blk + i) * halo_per_blk - 1, 0), 0)),
        pl.BlockSpec((POOL_HALO, d),
                     lambda b, i: (jnp.minimum((b * n_blk + i + 1) * halo_per_blk, n_halo - 1), 0)),
        pl.BlockSpec((QB, d), tok),
        pl.BlockSpec((QB, d), tok),
        pl.BlockSpec((QB, d), tok),
        pl.BlockSpec((QB, d), tok),
    ]
    in_specs += [win_spec(j) for j in range(WIN_CHUNKS)]
    in_specs += [win_spec(j) for j in range(WIN_CHUNKS)]
    in_specs += [
        pl.BlockSpec((ctx_len, d), lambda b, i: (b, 0)),
        pl.BlockSpec((ctx_len, d), lambda b, i: (b, 0)),
        pl.BlockSpec((1,) + bias.shape[1:], lambda b, i: (i - win_start(i), 0, 0, 0)),
        full(w_pool.shape), full(pool_scale.shape), full(w_out.shape),
        full(b_out.shape), full(ln_g.shape), full(ln_b.shape),
    ]
    args = [x2d, mod3, u, u, u, zp, za, qr, q] + [kr] * WIN_CHUNKS + [v] * WIN_CHUNKS + [
        kc, vc, bias, w_pool, pool_scale, w_out, b_out, ln_g, ln_b]
    return pl.pallas_call(
        functools.partial(_mix_kernel, seq),
        grid=(batch, n_blk),
        in_specs=in_specs,
        out_specs=pl.BlockSpec((QB, d), tok),
        out_shape=jax.ShapeDtypeStruct((t, d), F32),
        scratch_shapes=[pltpu.VMEM((QB, d_mix), BF16)],
        compiler_params=pltpu.CompilerParams(dimension_semantics=("arbitrary", "arbitrary"),
                                             vmem_limit_bytes=VMEM_LIMIT_BYTES),
        name="mix",
    )(*args)


def kernel(x, c, ctx, c_ctx, w_ada, b_ada, w_in, b_in, w_pool, pool_scale, rpb, w_out, b_out, ln_g, ln_b):
    batch, seq, d = x.shape
    ctx_len = ctx.shape[1]
    assert w_ada.shape[0] == DEPTH == 1
    assert seq % GRID_W == 0 and seq % INPROJ_TM == 0 and (batch * ctx_len) % INPROJ_TM == 0
    n_rows = seq // GRID_W
    assert n_rows >= NA_ROWS and seq // QB >= WIN_CHUNKS
    d_pool = pool_scale.shape[1]
    assert d_pool == d and w_in.shape[2] == 6 * d and rpb.shape[1] * HEAD_DIM == d

    cvec = jnp.zeros((8, d), F32).at[:batch].set(c).at[batch].set(c_ctx)
    mod3 = _ada(cvec, w_ada[0], b_ada[0][None, :]).reshape(8, 1, 3 * d)

    w_in_b = w_in[0].astype(BF16)
    b_in2 = b_in[0][None, :]
    x2d = x.reshape(batch * seq, d)
    tiles_per_seq = seq // INPROJ_TM
    u, zp, qr, q, kr, v, za = _inproj(
        x2d, mod3, lambda i: i // tiles_per_seq, w_in_b, b_in2, range(6),
        ("f32", "f32", "q", "k", "bf16", "f32"), (F32, F32, BF16, BF16, BF16, BF16, F32),
        _rope_tables(seq), INPROJ_TM)
    kc, vc = _inproj(
        ctx.reshape(batch * ctx_len, d), mod3, lambda i: batch, w_in_b, b_in2, (3, 4),
        ("bf16", "bf16"), (BF16, BF16), None, INPROJ_TM)

    bias = _bias_tables(rpb[0], seq // QB, n_rows)
    out = _mix(x2d, mod3, u, zp, za, qr, q, kr, v, kc, vc, bias,
               w_pool[0].astype(BF16), pool_scale, w_out[0].astype(BF16), b_out, ln_g, ln_b,
               batch, seq, ctx_len)
    return out.reshape(batch, seq, d)
```

```python
import functools

import numpy as np
import jax
import jax.numpy as jnp
from jax import lax
from jax.experimental import pallas as pl
from jax.experimental.pallas import tpu as pltpu

GRID_W = 64
POOL_WINDOWS = (2, 4, 8, 16)
HEAD_DIM = 64
NA_ROWS = 8
NA_COLS = 16
ROPE_BASE = 10000.0
LN_EPS = 1e-6
DEPTH = 1
DEEPNORM_ALPHA = (2.0 * DEPTH) ** 0.25

LANES = 128
VMEM_LIMIT_BYTES = 52 * 1024 * 1024

QUERY_ROWS = 2
QB = QUERY_ROWS * GRID_W
WIN_CHUNKS = 5
WIN = WIN_CHUNKS * QB
POOL_HALO = 8
NEG = -1e30
INPROJ_TM = 512
PAIR_GROUP = 2

F32 = jnp.float32
BF16 = jnp.bfloat16


def _silu(x):
    return x * jax.nn.sigmoid(x)


def _layer_norm(x):
    mu = jnp.mean(x, axis=-1, keepdims=True)
    xc = x - mu
    var = jnp.mean(xc * xc, axis=-1, keepdims=True)
    return xc * lax.rsqrt(var + LN_EPS)


def _ada_kernel(c_ref, w_ref, b_ref, o_ref):
    s = _silu(c_ref[...]).astype(BF16)
    o_ref[...] = jnp.dot(s, w_ref[...].astype(BF16), preferred_element_type=F32) + b_ref[...]


def _ada(cvec, w_ada, b_ada):
    rows, d = cvec.shape
    n = w_ada.shape[1]
    tn = 768
    return pl.pallas_call(
        _ada_kernel,
        grid=(n // tn,),
        in_specs=[pl.BlockSpec((rows, d), lambda j: (0, 0)),
                  pl.BlockSpec((d, tn), lambda j: (0, j)),
                  pl.BlockSpec((1, tn), lambda j: (0, j))],
        out_specs=pl.BlockSpec((rows, tn), lambda j: (0, j)),
        out_shape=jax.ShapeDtypeStruct((rows, n), F32),
        compiler_params=pltpu.CompilerParams(dimension_semantics=("arbitrary",),
                                             vmem_limit_bytes=VMEM_LIMIT_BYTES),
        name="ada",
    )(cvec, w_ada, b_ada)


def _rope(x, cos, sin):
    outs = []
    for c in range(x.shape[1] // LANES):
        xc = x[:, c * LANES:(c + 1) * LANES]
        outs.append(xc * cos + pltpu.roll(xc, LANES // 2, axis=1) * sin)
    return jnp.concatenate(outs, axis=1)


def _inproj_kernel(kinds, has_rope, *refs):
    n_in = 6 if has_rope else 4
    x_ref, mod_ref, w_ref, b_ref = refs[:4]
    cos_ref, sin_ref = (refs[4], refs[5]) if has_rope else (None, None)
    n_out = sum(2 if k == "q" else 1 for k in kinds)
    out_refs = refs[n_in:n_in + n_out]
    h_ref = refs[n_in + n_out]
    d = x_ref.shape[1]
    j = pl.program_id(1)

    @pl.when(j == 0)
    def _():
        mod = mod_ref[0]
        h = _layer_norm(x_ref[...]) * (1.0 + mod[:, d:2 * d]) + mod[:, :d]
        h_ref[...] = h.astype(BF16)

    def project():
        return jnp.dot(h_ref[...], w_ref[...], preferred_element_type=F32) + b_ref[...]

    o = 0
    for jj, kind in enumerate(kinds):
        outs = out_refs[o:o + (2 if kind == "q" else 1)]
        o += len(outs)

        @pl.when(j == jj)
        def _(kind=kind, outs=outs):
            acc = project()
            if kind == "q":
                qr_ref, q_ref = outs
                qs = acc * (HEAD_DIM ** -0.5)
                q_ref[...] = qs.astype(q_ref.dtype)
                qr_ref[...] = _rope(qs, cos_ref[...], sin_ref[...]).astype(qr_ref.dtype)
            else:
                val = _rope(acc, cos_ref[...], sin_ref[...]) if kind == "k" else acc
                outs[0][...] = val.astype(outs[0].dtype)


def _inproj(x2d, mod3, mod_rows_per_tile, w_bf16, b_in, col_blocks, kinds, out_dtypes, rope, tm):
    t, d = x2d.shape
    nt = t // tm
    has_rope = rope is not None
    col_blocks = tuple(col_blocks)
    col0 = col_blocks[0]
    assert col_blocks == tuple(range(col0, col0 + len(col_blocks)))
    in_specs = [
        pl.BlockSpec((tm, d), lambda i, j: (i, 0)),
        pl.BlockSpec((1, 1, mod3.shape[2]), lambda i, j: (mod_rows_per_tile(i), 0, 0)),
        pl.BlockSpec((d, d), lambda i, j: (0, col0 + j)),
        pl.BlockSpec((1, d), lambda i, j: (0, col0 + j)),
    ]
    args = [x2d, mod3, w_bf16, b_in]
    if has_rope:
        tiles_per_seq = rope[0].shape[0] // tm
        for tab in rope:
            in_specs.append(pl.BlockSpec((tm, LANES), lambda i, j: (i % tiles_per_seq, 0)))
            args.append(tab)
    out_specs = [pl.BlockSpec((tm, d), lambda i, j: (i, 0)) for _ in out_dtypes]
    out_shape = [jax.ShapeDtypeStruct((t, d), dt) for dt in out_dtypes]
    return pl.pallas_call(
        functools.partial(_inproj_kernel, tuple(kinds), has_rope),
        grid=(nt, len(kinds)),
        in_specs=in_specs,
        out_specs=out_specs,
        out_shape=out_shape,
        scratch_shapes=[pltpu.VMEM((tm, d), BF16)],
        compiler_params=pltpu.CompilerParams(dimension_semantics=("arbitrary", "arbitrary"),
                                             vmem_limit_bytes=VMEM_LIMIT_BYTES),
        name="inproj_rope" if has_rope else "inproj_ctx",
    )(*args)


def _rotary_column_order(w):
    half = HEAD_DIM // 4
    lead = w.shape[:-1]
    w = w.reshape(lead + (w.shape[-1] // LANES, 2, 2, 2, half))
    n = len(lead)
    w = w.transpose(tuple(range(n)) + (n, n + 3, n + 1, n + 2, n + 4))
    return w.reshape(lead + (-1,))


def _rope_tables(seq):
    half = HEAD_DIM // 4
    inv_freq = ROPE_BASE ** (-np.arange(half, dtype=np.float64) / half)
    t = np.arange(seq)
    row, col = t // GRID_W, t % GRID_W
    lane = np.arange(LANES)
    pos = np.where((lane % (2 * half) < half)[None, :], row[:, None], col[:, None]).astype(np.float64)
    ang = pos * inv_freq[lane % half][None, :]
    sign = np.where(lane < LANES // 2, -1.0, 1.0)[None, :]
    return jnp.asarray(np.cos(ang), F32), jnp.asarray(np.sin(ang) * sign, F32)


def _window_geometry(n_blocks, n_rows):
    cases = {}
    for blk in range(n_blocks):
        ws = min(max(blk - (WIN_CHUNKS // 2), 0), n_blocks - WIN_CHUNKS)
        v = blk - ws
        geo = []
        for a in range(QUERY_ROWS):
            r = blk * QUERY_ROWS + a
            rs = min(max(r - NA_ROWS // 2, 0), n_rows - NA_ROWS)
            wr0 = ws * QUERY_ROWS
            geo.append((rs - wr0, rs - wr0 + NA_ROWS, wr0 - r + NA_ROWS - 1))
        geo = tuple(geo)
        assert cases.setdefault(v, geo) == geo
    assert sorted(cases) == list(range(WIN_CHUNKS))
    return [cases[v] for v in range(WIN_CHUNKS)]


def _bias_kernel(geometry, rpb_ref, o_ref):
    table = rpb_ref[0]
    lane = lax.broadcasted_iota(jnp.int32, (GRID_W, LANES), 1)
    qc = lax.broadcasted_iota(jnp.int32, (GRID_W, LANES), 0)
    kc = lane % GRID_W
    cs = jnp.clip(qc - NA_COLS // 2, 0, GRID_W - NA_COLS)
    col_ok = (kc >= cs) & (kc < cs + NA_COLS)
    low_half = lane < GRID_W

    @functools.cache
    def shifted(dr, half):
        row = jnp.broadcast_to(table[dr:dr + 1, :], (GRID_W, LANES))
        shift = (LANES - (NA_COLS - 1) + GRID_W * half) % LANES
        return pltpu.roll(row, shift, axis=1, stride=1, stride_axis=0)

    neg = jnp.full((GRID_W, LANES), NEG, F32)
    for v, geo in enumerate(geometry):
        for a, (lo, hi, dr0) in enumerate(geo):
            pairs = []
            for jr in range(0, WIN_CHUNKS * QUERY_ROWS, 2):
                left = shifted(dr0 + jr, 0) if lo <= jr < hi else neg
                right = shifted(dr0 + jr + 1, 1) if lo <= jr + 1 < hi else neg
                pairs.append(jnp.where(col_ok, jnp.where(low_half, left, right), NEG))
            o_ref[v, 0, a * GRID_W:(a + 1) * GRID_W, :] = jnp.concatenate(pairs, axis=1)


def _bias_tables(rpb, n_blocks, n_rows):
    n_heads, n_dr, n_dc = rpb.shape
    rpb_pad = jnp.pad(rpb, ((0, 0), (0, 16 - n_dr), (0, LANES - n_dc)))
    geometry = _window_geometry(n_blocks, n_rows)
    return pl.pallas_call(
        functools.partial(_bias_kernel, geometry),
        grid=(n_heads,),
        in_specs=[pl.BlockSpec((1, 16, LANES), lambda h: (h, 0, 0))],
        out_specs=pl.BlockSpec((WIN_CHUNKS, 1, QB, WIN), lambda h: (0, h, 0, 0)),
        out_shape=jax.ShapeDtypeStruct((WIN_CHUNKS, n_heads, QB, WIN), F32),
        compiler_params=pltpu.CompilerParams(dimension_semantics=("arbitrary",),
                                             vmem_limit_bytes=VMEM_LIMIT_BYTES),
        name="bias_tables",
    )(rpb_pad)


def _mix_kernel(seq, x_ref, mod_ref, u_ref, up_ref, un_ref, zp_ref, za_ref, qr_ref, q_ref, *rest):
    k_refs = rest[:WIN_CHUNKS]
    v_refs = rest[WIN_CHUNKS:2 * WIN_CHUNKS]
    (kc_ref, vc_ref, bias_ref, wp_ref, ps_ref, wo_ref, bo_ref, g_ref, be_ref,
     o_ref, cat_ref) = rest[2 * WIN_CHUNKS:]
    d = x_ref.shape[1]
    d_pool = ps_ref.shape[1]
    gdim = d_pool // len(POOL_WINDOWS)
    n_heads = (cat_ref.shape[1] - d_pool) // HEAD_DIM
    i = pl.program_id(1)
    n_blk = pl.num_programs(1)

    u = u_ref[...]
    prev = jnp.where(i > 0, up_ref[...], 0.0)
    nxt = jnp.where(i < n_blk - 1, un_ref[...], 0.0)
    ext = jnp.concatenate([prev, u, nxt], axis=0)
    n_ext = ext.shape[0]
    t = i * QB + lax.broadcasted_iota(jnp.int32, (QB, 1), 0)

    def shift_sum(s, k):
        return pltpu.roll(s, k, axis=0) + pltpu.roll(s, n_ext - k, axis=0)

    for g, w in enumerate(POOL_WINDOWS):
        cols = slice(g * gdim, (g + 1) * gdim)
        s = ext[:, cols]
        s = s + pltpu.roll(s, 1, axis=0)
        k = 1
        while 2 * k < w:
            s = shift_sum(s, k)
            k *= 2
        s = s[POOL_HALO:POOL_HALO + QB]
        lo = jnp.clip(t - w // 2, 0, seq)
        hi = jnp.clip(t - w // 2 + w, 0, seq)
        pooled = s / (hi - lo).astype(F32) - u[:, cols]
        y = jnp.dot(pooled.astype(BF16), wp_ref[g], preferred_element_type=F32)
        y = y * ps_ref[:, cols] * _silu(zp_ref[:, cols])
        cat_ref[:, cols] = y.astype(BF16)

    nt_dims = (((1,), (1,)), ((), ()))
    lane = lax.broadcasted_iota(jnp.int32, (1, LANES), 1)
    first_head = lane < HEAD_DIM
    first_head_qk = lane % HEAD_DIM < HEAD_DIM // 2
    ones_win = jnp.ones((WIN, LANES), BF16)
    ones_ctx = jnp.ones((kc_ref.shape[0], LANES), BF16)

    def stack_heads(qp):
        zero = jnp.zeros_like(qp)
        return jnp.concatenate([jnp.where(first_head_qk, qp, zero), jnp.where(first_head_qk, zero, qp)], axis=0)

    def lane_tiles(a):
        return [a[:, c * LANES:(c + 1) * LANES] for c in range(a.shape[1] // LANES)]

    n_pairs = n_heads // 2
    for g0 in range(0, n_pairs, PAIR_GROUP):
        pairs = range(g0, min(g0 + PAIR_GROUP, n_pairs))
        chunk = {p: slice(p * LANES, (p + 1) * LANES) for p in pairs}
        s_loc, s_ctx, m, p_all, o = {}, {}, {}, {}, {}
        for p in pairs:
            k_win = jnp.concatenate([r[:, chunk[p]] for r in k_refs], axis=0)
            bias = bias_ref[0, 2 * p:2 * p + 2].reshape(2 * QB, WIN)
            s_loc[p] = lax.dot_general(stack_heads(qr_ref[:, chunk[p]]), k_win, nt_dims,
                                       preferred_element_type=F32) + bias
            s_ctx[p] = lax.dot_general(stack_heads(q_ref[:, chunk[p]]), kc_ref[:, chunk[p]], nt_dims,
                                       preferred_element_type=F32)
        for p in pairs:
            tiles = lane_tiles(s_loc[p]) + lane_tiles(s_ctx[p])
            m[p] = jnp.max(functools.reduce(jnp.maximum, tiles), axis=1, keepdims=True)
        for p in pairs:
            p_all[p] = jnp.concatenate([jnp.exp(s_loc[p] - m[p]), jnp.exp(s_ctx[p] - m[p])],
                                       axis=1).astype(BF16)
        for p in pairs:
            v_win = jnp.concatenate([r[:, chunk[p]] for r in v_refs], axis=0)
            v_aug = jnp.concatenate([jnp.concatenate([v_win, ones_win], axis=1),
                                     jnp.concatenate([vc_ref[:, chunk[p]], ones_ctx], axis=1)], axis=0)
            o[p] = jnp.dot(p_all[p], v_aug, preferred_element_type=F32)
        for p in pairs:
            num = jnp.where(first_head, o[p][:QB, :LANES], o[p][QB:, :LANES])
            den = jnp.where(first_head, o[p][:QB, LANES:], o[p][QB:, LANES:])
            out = num / den * _silu(za_ref[:, chunk[p]])
            cat_ref[:, d_pool + p * LANES:d_pool + (p + 1) * LANES] = out.astype(BF16)

    y = jnp.dot(cat_ref[...], wo_ref[...], preferred_element_type=F32) + bo_ref[...]
    gate = mod_ref[0][:, 2 * d:3 * d]
    z = DEEPNORM_ALPHA * x_ref[...] + gate * y
    o_ref[...] = _layer_norm(z) * g_ref[...] + be_ref[...]


def _mix(x2d, mod3, u, zp, za, qr, q, kr, v, kc, vc, bias, w_pool, pool_scale, w_out, b_out, ln_g, ln_b,
         batch, seq, ctx_len):
    t, d = x2d.shape
    n_blk = seq // QB
    halo_per_blk = QB // POOL_HALO
    n_halo = t // POOL_HALO
    d_mix = w_out.shape[0]

    def win_start(i):
        return jnp.clip(i - WIN_CHUNKS // 2, 0, n_blk - WIN_CHUNKS)

    def tok(b, i):
        return (b * n_blk + i, 0)

    def win_spec(j):
        return pl.BlockSpec((QB, d), lambda b, i: (b * n_blk + win_start(i) + j, 0))

    full = lambda shape: pl.BlockSpec(shape, lambda b, i: (0,) * len(shape))
    in_specs = [
        pl.BlockSpec((QB, d), tok),
        pl.BlockSpec((1, 1, mod3.shape[2]), lambda b, i: (b, 0, 0)),
        pl.BlockSpec((QB, d), tok),
        pl.BlockSpec((POOL_HALO, d),
                     lambda b, i: (jnp.maximum((b * n_blk + i) * halo_per_blk - 1, 0), 0)),
        pl.BlockSpec((POOL_HALO, d),
                     lambda b, i: (jnp.minimum((b * n_blk + i + 1) * halo_per_blk, n_halo - 1), 0)),
        pl.BlockSpec((QB, d), tok),
        pl.BlockSpec((QB, d), tok),
        pl.BlockSpec((QB, d), tok),
        pl.BlockSpec((QB, d), tok),
    ]
    in_specs += [win_spec(j) for j in range(WIN_CHUNKS)]
    in_specs += [win_spec(j) for j in range(WIN_CHUNKS)]
    in_specs += [
        pl.BlockSpec((ctx_len, d), lambda b, i: (b, 0)),
        pl.BlockSpec((ctx_len, d), lambda b, i: (b, 0)),
        pl.BlockSpec((1,) + bias.shape[1:], lambda b, i: (i - win_start(i), 0, 0, 0)),
        full(w_pool.shape), full(pool_scale.shape), full(w_out.shape),
        full(b_out.shape), full(ln_g.shape), full(ln_b.shape),
    ]
    args = [x2d, mod3, u, u, u, zp, za, qr, q] + [kr] * WIN_CHUNKS + [v] * WIN_CHUNKS + [
        kc, vc, bias, w_pool, pool_scale, w_out, b_out, ln_g, ln_b]
    return pl.pallas_call(
        functools.partial(_mix_kernel, seq),
        grid=(batch, n_blk),
        in_specs=in_specs,
        out_specs=pl.BlockSpec((QB, d), tok),
        out_shape=jax.ShapeDtypeStruct((t, d), F32),
        scratch_shapes=[pltpu.VMEM((QB, d_mix), BF16)],
        compiler_params=pltpu.CompilerParams(dimension_semantics=("arbitrary", "arbitrary"),
                                             vmem_limit_bytes=VMEM_LIMIT_BYTES),
        name="mix",
    )(*args)


def kernel(x, c, ctx, c_ctx, w_ada, b_ada, w_in, b_in, w_pool, pool_scale, rpb, w_out, b_out, ln_g, ln_b):
    batch, seq, d = x.shape
    ctx_len = ctx.shape[1]
    assert w_ada.shape[0] == DEPTH == 1
    assert seq % GRID_W == 0 and seq % INPROJ_TM == 0 and (batch * ctx_len) % INPROJ_TM == 0
    n_rows = seq // GRID_W
    assert n_rows >= NA_ROWS and seq // QB >= WIN_CHUNKS
    d_pool = pool_scale.shape[1]
    assert d_pool == d and w_in.shape[2] == 6 * d and rpb.shape[1] * HEAD_DIM == d

    cvec = jnp.zeros((8, d), F32).at[:batch].set(c).at[batch].set(c_ctx)
    mod3 = _ada(cvec, w_ada[0], b_ada[0][None, :]).reshape(8, 1, 3 * d)

    def qk_rotary_order(a):
        return jnp.concatenate([a[..., :2 * d], _rotary_column_order(a[..., 2 * d:4 * d]), a[..., 4 * d:]], axis=-1)

    w_in_b = qk_rotary_order(w_in[0]).astype(BF16)
    b_in2 = qk_rotary_order(b_in[0])[None, :]
    x2d = x.reshape(batch * seq, d)
    tiles_per_seq = seq // INPROJ_TM
    u, zp, qr, q, kr, v, za = _inproj(
        x2d, mod3, lambda i: i // tiles_per_seq, w_in_b, b_in2, range(6),
        ("f32", "f32", "q", "k", "bf16", "f32"), (F32, F32, BF16, BF16, BF16, BF16, F32),
        _rope_tables(seq), INPROJ_TM)
    kc, vc = _inproj(
        ctx.reshape(batch * ctx_len, d), mod3, lambda i: batch, w_in_b, b_in2, (3, 4),
        ("bf16", "bf16"), (BF16, BF16), None, INPROJ_TM)

    bias = _bias_tables(rpb[0], seq // QB, n_rows)
    out = _mix(x2d, mod3, u, zp, za, qr, q, kr, v, kc, vc, bias,
               w_pool[0].astype(BF16), pool_scale, w_out[0].astype(BF16), b_out, ln_g, ln_b,
               batch, seq, ctx_len)
    return out.reshape(batch, seq, d)
```

```python
import functools

import numpy as np
import jax
import jax.numpy as jnp
from jax import lax
from jax.experimental import pallas as pl
from jax.experimental.pallas import tpu as pltpu

GRID_W = 64
POOL_WINDOWS = (2, 4, 8, 16)
HEAD_DIM = 64
NA_ROWS = 8
NA_COLS = 16
ROPE_BASE = 10000.0
LN_EPS = 1e-6
DEPTH = 1
DEEPNORM_ALPHA = (2.0 * DEPTH) ** 0.25

LANES = 128
VMEM_LIMIT_BYTES = 52 * 1024 * 1024

QUERY_ROWS = 2
QB = QUERY_ROWS * GRID_W
WIN_CHUNKS = 5
WIN = WIN_CHUNKS * QB
POOL_HALO = 8
NEG = -1e30
INPROJ_TM = 512
PAIR_GROUP = 2

F32 = jnp.float32
BF16 = jnp.bfloat16


def _silu(x):
    return x * jax.nn.sigmoid(x)


def _layer_norm(x):
    mu = jnp.mean(x, axis=-1, keepdims=True)
    xc = x - mu
    var = jnp.mean(xc * xc, axis=-1, keepdims=True)
    return xc * lax.rsqrt(var + LN_EPS)


def _ada_kernel(c_ref, w_ref, b_ref, o_ref):
    s = _silu(c_ref[...]).astype(BF16)
    o_ref[...] = jnp.dot(s, w_ref[...].astype(BF16), preferred_element_type=F32) + b_ref[...]


def _ada(cvec, w_ada, b_ada):
    rows, d = cvec.shape
    n = w_ada.shape[1]
    tn = 768
    return pl.pallas_call(
        _ada_kernel,
        grid=(n // tn,),
        in_specs=[pl.BlockSpec((rows, d), lambda j: (0, 0)),
                  pl.BlockSpec((d, tn), lambda j: (0, j)),
                  pl.BlockSpec((1, tn), lambda j: (0, j))],
        out_specs=pl.BlockSpec((rows, tn), lambda j: (0, j)),
        out_shape=jax.ShapeDtypeStruct((rows, n), F32),
        compiler_params=pltpu.CompilerParams(dimension_semantics=("arbitrary",),
                                             vmem_limit_bytes=VMEM_LIMIT_BYTES),
        name="ada",
    )(cvec, w_ada, b_ada)


def _rope(x, cos, sin):
    outs = []
    for c in range(x.shape[1] // LANES):
        xc = x[:, c * LANES:(c + 1) * LANES]
        outs.append(xc * cos + pltpu.roll(xc, LANES // 2, axis=1) * sin)
    return jnp.concatenate(outs, axis=1)


def _inproj_kernel(kinds, has_rope, *refs):
    n_blk = len(kinds)
    x_ref, mod_ref = refs[:2]
    w_refs = refs[2:2 + n_blk]
    b_refs = refs[2 + n_blk:2 + 2 * n_blk]
    n_in = 2 + 2 * n_blk + (2 if has_rope else 0)
    cos_ref, sin_ref = refs[n_in - 2:n_in] if has_rope else (None, None)
    out_refs = refs[n_in:]
    d = x_ref.shape[1]

    mod = mod_ref[0]
    h = (_layer_norm(x_ref[...]) * (1.0 + mod[:, d:2 * d]) + mod[:, :d]).astype(BF16)

    o = 0
    for kind, w_ref, b_ref in zip(kinds, w_refs, b_refs):
        acc = jnp.dot(h, w_ref[...], preferred_element_type=F32) + b_ref[...]
        if kind == "q":
            qr_ref, q_ref = out_refs[o:o + 2]
            o += 2
            qs = acc * (HEAD_DIM ** -0.5)
            q_ref[...] = qs.astype(q_ref.dtype)
            qr_ref[...] = _rope(qs, cos_ref[...], sin_ref[...]).astype(qr_ref.dtype)
        else:
            val = _rope(acc, cos_ref[...], sin_ref[...]) if kind == "k" else acc
            out_refs[o][...] = val.astype(out_refs[o].dtype)
            o += 1


def _inproj(x2d, mod3, mod_rows_per_tile, w_bf16, b_in, col_blocks, kinds, out_dtypes, rope, tm):
    t, d = x2d.shape
    has_rope = rope is not None
    resident = pl.Buffered(1)
    in_specs = [
        pl.BlockSpec((tm, d), lambda i: (i, 0)),
        pl.BlockSpec((1, 1, mod3.shape[2]), lambda i: (mod_rows_per_tile(i), 0, 0)),
    ]
    in_specs += [pl.BlockSpec((d, d), lambda i, c=c: (0, c), pipeline_mode=resident) for c in col_blocks]
    in_specs += [pl.BlockSpec((1, d), lambda i, c=c: (0, c), pipeline_mode=resident) for c in col_blocks]
    args = [x2d, mod3] + [w_bf16] * len(col_blocks) + [b_in] * len(col_blocks)
    if has_rope:
        tiles_per_seq = rope[0].shape[0] // tm
        for tab in rope:
            in_specs.append(pl.BlockSpec((tm, LANES), lambda i: (i % tiles_per_seq, 0)))
            args.append(tab)
    out_specs = [pl.BlockSpec((tm, d), lambda i: (i, 0)) for _ in out_dtypes]
    out_shape = [jax.ShapeDtypeStruct((t, d), dt) for dt in out_dtypes]
    return pl.pallas_call(
        functools.partial(_inproj_kernel, tuple(kinds), has_rope),
        grid=(t // tm,),
        in_specs=in_specs,
        out_specs=out_specs,
        out_shape=out_shape,
        compiler_params=pltpu.CompilerParams(dimension_semantics=("arbitrary",),
                                             vmem_limit_bytes=VMEM_LIMIT_BYTES),
        name="inproj_rope" if has_rope else "inproj_ctx",
    )(*args)


def _rotary_column_order(w):
    half = HEAD_DIM // 4
    lead = w.shape[:-1]
    w = w.reshape(lead + (w.shape[-1] // LANES, 2, 2, 2, half))
    n = len(lead)
    w = w.transpose(tuple(range(n)) + (n, n + 3, n + 1, n + 2, n + 4))
    return w.reshape(lead + (-1,))


def _rope_tables(seq):
    half = HEAD_DIM // 4
    inv_freq = ROPE_BASE ** (-np.arange(half, dtype=np.float64) / half)
    t = np.arange(seq)
    row, col = t // GRID_W, t % GRID_W
    lane = np.arange(LANES)
    pos = np.where((lane % (2 * half) < half)[None, :], row[:, None], col[:, None]).astype(np.float64)
    ang = pos * inv_freq[lane % half][None, :]
    sign = np.where(lane < LANES // 2, -1.0, 1.0)[None, :]
    return jnp.asarray(np.cos(ang), F32), jnp.asarray(np.sin(ang) * sign, F32)


def _window_geometry(n_blocks, n_rows):
    cases = {}
    for blk in range(n_blocks):
        ws = min(max(blk - (WIN_CHUNKS // 2), 0), n_blocks - WIN_CHUNKS)
        v = blk - ws
        geo = []
        for a in range(QUERY_ROWS):
            r = blk * QUERY_ROWS + a
            rs = min(max(r - NA_ROWS // 2, 0), n_rows - NA_ROWS)
            wr0 = ws * QUERY_ROWS
            geo.append((rs - wr0, rs - wr0 + NA_ROWS, wr0 - r + NA_ROWS - 1))
        geo = tuple(geo)
        assert cases.setdefault(v, geo) == geo
    assert sorted(cases) == list(range(WIN_CHUNKS))
    return [cases[v] for v in range(WIN_CHUNKS)]


def _bias_kernel(geometry, rpb_ref, o_ref):
    table = rpb_ref[0]
    lane = lax.broadcasted_iota(jnp.int32, (GRID_W, LANES), 1)
    qc = lax.broadcasted_iota(jnp.int32, (GRID_W, LANES), 0)
    kc = lane % GRID_W
    cs = jnp.clip(qc - NA_COLS // 2, 0, GRID_W - NA_COLS)
    col_ok = (kc >= cs) & (kc < cs + NA_COLS)
    low_half = lane < GRID_W

    @functools.cache
    def shifted(dr, half):
        row = jnp.broadcast_to(table[dr:dr + 1, :], (GRID_W, LANES))
        shift = (LANES - (NA_COLS - 1) + GRID_W * half) % LANES
        return pltpu.roll(row, shift, axis=1, stride=1, stride_axis=0)

    neg = jnp.full((GRID_W, LANES), NEG, F32)
    for v, geo in enumerate(geometry):
        for a, (lo, hi, dr0) in enumerate(geo):
            pairs = []
            for jr in range(0, WIN_CHUNKS * QUERY_ROWS, 2):
                left = shifted(dr0 + jr, 0) if lo <= jr < hi else neg
                right = shifted(dr0 + jr + 1, 1) if lo <= jr + 1 < hi else neg
                pairs.append(jnp.where(col_ok, jnp.where(low_half, left, right), NEG))
            o_ref[v, 0, a * GRID_W:(a + 1) * GRID_W, :] = jnp.concatenate(pairs, axis=1)


def _bias_tables(rpb, n_blocks, n_rows):
    n_heads, n_dr, n_dc = rpb.shape
    rpb_pad = jnp.pad(rpb, ((0, 0), (0, 16 - n_dr), (0, LANES - n_dc)))
    geometry = _window_geometry(n_blocks, n_rows)
    return pl.pallas_call(
        functools.partial(_bias_kernel, geometry),
        grid=(n_heads,),
        in_specs=[pl.BlockSpec((1, 16, LANES), lambda h: (h, 0, 0))],
        out_specs=pl.BlockSpec((WIN_CHUNKS, 1, QB, WIN), lambda h: (0, h, 0, 0)),
        out_shape=jax.ShapeDtypeStruct((WIN_CHUNKS, n_heads, QB, WIN), F32),
        compiler_params=pltpu.CompilerParams(dimension_semantics=("arbitrary",),
                                             vmem_limit_bytes=VMEM_LIMIT_BYTES),
        name="bias_tables",
    )(rpb_pad)


def _mix_kernel(seq, x_ref, mod_ref, u_ref, up_ref, un_ref, zp_ref, za_ref, qr_ref, q_ref, *rest):
    k_refs = rest[:WIN_CHUNKS]
    v_refs = rest[WIN_CHUNKS:2 * WIN_CHUNKS]
    (kc_ref, vc_ref, bias_ref, wp_ref, ps_ref, wo_ref, bo_ref, g_ref, be_ref,
     o_ref, cat_ref) = rest[2 * WIN_CHUNKS:]
    d = x_ref.shape[1]
    d_pool = ps_ref.shape[1]
    gdim = d_pool // len(POOL_WINDOWS)
    n_heads = (cat_ref.shape[1] - d_pool) // HEAD_DIM
    i = pl.program_id(1)
    n_blk = pl.num_programs(1)

    u = u_ref[...]
    prev = jnp.where(i > 0, up_ref[...], 0.0)
    nxt = jnp.where(i < n_blk - 1, un_ref[...], 0.0)
    ext = jnp.concatenate([prev, u, nxt], axis=0)
    n_ext = ext.shape[0]
    t = i * QB + lax.broadcasted_iota(jnp.int32, (QB, 1), 0)

    def shift_sum(s, k):
        return pltpu.roll(s, k, axis=0) + pltpu.roll(s, n_ext - k, axis=0)

    for g, w in enumerate(POOL_WINDOWS):
        cols = slice(g * gdim, (g + 1) * gdim)
        s = ext[:, cols]
        s = s + pltpu.roll(s, 1, axis=0)
        k = 1
        while 2 * k < w:
            s = shift_sum(s, k)
            k *= 2
        s = s[POOL_HALO:POOL_HALO + QB]
        lo = jnp.clip(t - w // 2, 0, seq)
        hi = jnp.clip(t - w // 2 + w, 0, seq)
        pooled = s / (hi - lo).astype(F32) - u[:, cols]
        y = jnp.dot(pooled.astype(BF16), wp_ref[g], preferred_element_type=F32)
        y = y * ps_ref[:, cols] * _silu(zp_ref[:, cols].astype(F32))
        cat_ref[:, cols] = y.astype(BF16)

    nt_dims = (((1,), (1,)), ((), ()))
    lane = lax.broadcasted_iota(jnp.int32, (1, LANES), 1)
    first_head = lane < HEAD_DIM
    first_head_qk = lane % HEAD_DIM < HEAD_DIM // 2
    ones_win = jnp.ones((WIN, LANES), BF16)
    ones_ctx = jnp.ones((kc_ref.shape[0], LANES), BF16)

    def stack_heads(qp):
        zero = jnp.zeros_like(qp)
        return jnp.concatenate([jnp.where(first_head_qk, qp, zero), jnp.where(first_head_qk, zero, qp)], axis=0)

    def lane_tiles(a):
        return [a[:, c * LANES:(c + 1) * LANES] for c in range(a.shape[1] // LANES)]

    n_pairs = n_heads // 2
    for g0 in range(0, n_pairs, PAIR_GROUP):
        pairs = range(g0, min(g0 + PAIR_GROUP, n_pairs))
        chunk = {p: slice(p * LANES, (p + 1) * LANES) for p in pairs}
        s_loc, s_ctx, m, p_all, o = {}, {}, {}, {}, {}
        for p in pairs:
            k_win = jnp.concatenate([r[:, chunk[p]] for r in k_refs], axis=0)
            bias = bias_ref[0, 2 * p:2 * p + 2].reshape(2 * QB, WIN)
            s_loc[p] = lax.dot_general(stack_heads(qr_ref[:, chunk[p]]), k_win, nt_dims,
                                       preferred_element_type=F32) + bias
            s_ctx[p] = lax.dot_general(stack_heads(q_ref[:, chunk[p]]), kc_ref[:, chunk[p]], nt_dims,
                                       preferred_element_type=F32)
        for p in pairs:
            tiles = lane_tiles(s_loc[p]) + lane_tiles(s_ctx[p])
            m[p] = jnp.max(functools.reduce(jnp.maximum, tiles), axis=1, keepdims=True)
        for p in pairs:
            p_all[p] = jnp.concatenate([jnp.exp(s_loc[p] - m[p]), jnp.exp(s_ctx[p] - m[p])],
                                       axis=1).astype(BF16)
        for p in pairs:
            v_win = jnp.concatenate([r[:, chunk[p]] for r in v_refs], axis=0)
            v_aug = jnp.concatenate([jnp.concatenate([v_win, ones_win], axis=1),
                                     jnp.concatenate([vc_ref[:, chunk[p]], ones_ctx], axis=1)], axis=0)
            o[p] = jnp.dot(p_all[p], v_aug, preferred_element_type=F32)
        for p in pairs:
            num = jnp.where(first_head, o[p][:QB, :LANES], o[p][QB:, :LANES])
            den = jnp.where(first_head, o[p][:QB, LANES:], o[p][QB:, LANES:])
            out = num / den * _silu(za_ref[:, chunk[p]].astype(F32))
            cat_ref[:, d_pool + p * LANES:d_pool + (p + 1) * LANES] = out.astype(BF16)

    y = jnp.dot(cat_ref[...], wo_ref[...], preferred_element_type=F32) + bo_ref[...]
    gate = mod_ref[0][:, 2 * d:3 * d]
    z = DEEPNORM_ALPHA * x_ref[...] + gate * y
    o_ref[...] = _layer_norm(z) * g_ref[...] + be_ref[...]


def _mix(x2d, mod3, u, zp, za, qr, q, kr, v, kc, vc, bias, w_pool, pool_scale, w_out, b_out, ln_g, ln_b,
         batch, seq, ctx_len):
    t, d = x2d.shape
    n_blk = seq // QB
    halo_per_blk = QB // POOL_HALO
    n_halo = t // POOL_HALO
    d_mix = w_out.shape[0]

    def win_start(i):
        return jnp.clip(i - WIN_CHUNKS // 2, 0, n_blk - WIN_CHUNKS)

    def tok(b, i):
        return (b * n_blk + i, 0)

    def win_spec(j):
        return pl.BlockSpec((QB, d), lambda b, i: (b * n_blk + win_start(i) + j, 0))

    full = lambda shape: pl.BlockSpec(shape, lambda b, i: (0,) * len(shape))
    in_specs = [
        pl.BlockSpec((QB, d), tok),
        pl.BlockSpec((1, 1, mod3.shape[2]), lambda b, i: (b, 0, 0)),
        pl.BlockSpec((QB, d), tok),
        pl.BlockSpec((POOL_HALO, d),
                     lambda b, i: (jnp.maximum((b * n_blk + i) * halo_per_blk - 1, 0), 0)),
        pl.BlockSpec((POOL_HALO, d),
                     lambda b, i: (jnp.minimum((b * n_blk + i + 1) * halo_per_blk, n_halo - 1), 0)),
        pl.BlockSpec((QB, d), tok),
        pl.BlockSpec((QB, d), tok),
        pl.BlockSpec((QB, d), tok),
        pl.BlockSpec((QB, d), tok),
    ]
    in_specs += [win_spec(j) for j in range(WIN_CHUNKS)]
    in_specs += [win_spec(j) for j in range(WIN_CHUNKS)]
    in_specs += [
        pl.BlockSpec((ctx_len, d), lambda b, i: (b, 0)),
        pl.BlockSpec((ctx_len, d), lambda b, i: (b, 0)),
        pl.BlockSpec((1,) + bias.shape[1:], lambda b, i: (i - win_start(i), 0, 0, 0)),
        full(w_pool.shape), full(pool_scale.shape), full(w_out.shape),
        full(b_out.shape), full(ln_g.shape), full(ln_b.shape),
    ]
    args = [x2d, mod3, u, u, u, zp, za, qr, q] + [kr] * WIN_CHUNKS + [v] * WIN_CHUNKS + [
        kc, vc, bias, w_pool, pool_scale, w_out, b_out, ln_g, ln_b]
    return pl.pallas_call(
        functools.partial(_mix_kernel, seq),
        grid=(batch, n_blk),
        in_specs=in_specs,
        out_specs=pl.BlockSpec((QB, d), tok),
        out_shape=jax.ShapeDtypeStruct((t, d), F32),
        scratch_shapes=[pltpu.VMEM((QB, d_mix), BF16)],
        compiler_params=pltpu.CompilerParams(dimension_semantics=("arbitrary", "arbitrary"),
                                             vmem_limit_bytes=VMEM_LIMIT_BYTES),
        name="mix",
    )(*args)


def kernel(x, c, ctx, c_ctx, w_ada, b_ada, w_in, b_in, w_pool, pool_scale, rpb, w_out, b_out, ln_g, ln_b):
    batch, seq, d = x.shape
    ctx_len = ctx.shape[1]
    assert w_ada.shape[0] == DEPTH == 1
    assert seq % GRID_W == 0 and seq % INPROJ_TM == 0 and (batch * ctx_len) % INPROJ_TM == 0
    n_rows = seq // GRID_W
    assert n_rows >= NA_ROWS and seq // QB >= WIN_CHUNKS
    d_pool = pool_scale.shape[1]
    assert d_pool == d and w_in.shape[2] == 6 * d and rpb.shape[1] * HEAD_DIM == d

    cvec = jnp.zeros((8, d), F32).at[:batch].set(c).at[batch].set(c_ctx)
    mod3 = _ada(cvec, w_ada[0], b_ada[0][None, :]).reshape(8, 1, 3 * d)

    def qk_rotary_order(a):
        return jnp.concatenate([a[..., :2 * d], _rotary_column_order(a[..., 2 * d:4 * d]), a[..., 4 * d:]], axis=-1)

    w_in_b = qk_rotary_order(w_in[0]).astype(BF16)
    b_in2 = qk_rotary_order(b_in[0])[None, :]
    x2d = x.reshape(batch * seq, d)
    tiles_per_seq = seq // INPROJ_TM
    u, zp, qr, q, kr, v, za = _inproj(
        x2d, mod3, lambda i: i // tiles_per_seq, w_in_b, b_in2, range(6),
        ("plain", "plain", "q", "k", "plain", "plain"), (F32, BF16, BF16, BF16, BF16, BF16, BF16),
        _rope_tables(seq), INPROJ_TM)
    kc, vc = _inproj(
        ctx.reshape(batch * ctx_len, d), mod3, lambda i: batch, w_in_b, b_in2, (3, 4),
        ("plain", "plain"), (BF16, BF16), None, INPROJ_TM)

    bias = _bias_tables(rpb[0], seq // QB, n_rows)
    out = _mix(x2d, mod3, u, zp, za, qr, q, kr, v, kc, vc, bias,
               w_pool[0].astype(BF16), pool_scale, w_out[0].astype(BF16), b_out, ln_g, ln_b,
               batch, seq, ctx_len)
    return out.reshape(batch, seq, d)
```

```python
import functools

import numpy as np
import jax
import jax.numpy as jnp
from jax import lax
from jax.experimental import pallas as pl
from jax.experimental.pallas import tpu as pltpu

GRID_W = 64
POOL_WINDOWS = (2, 4, 8, 16)
HEAD_DIM = 64
NA_ROWS = 8
NA_COLS = 16
ROPE_BASE = 10000.0
LN_EPS = 1e-6
LOG2_E = float(np.log2(np.e))
SCORE_SCALE = HEAD_DIM ** -0.5 * LOG2_E
DEPTH = 1
DEEPNORM_ALPHA = (2.0 * DEPTH) ** 0.25

LANES = 128
VMEM_LIMIT_BYTES = 52 * 1024 * 1024

QUERY_ROWS = 2
QB = QUERY_ROWS * GRID_W
WIN_CHUNKS = 5
WIN = WIN_CHUNKS * QB
POOL_HALO = 8
NEG = -1e30
INPROJ_TM = 512
F32 = jnp.float32
BF16 = jnp.bfloat16


def _silu(x):
    return x * jax.nn.sigmoid(x)


def _layer_norm(x):
    mu = jnp.mean(x, axis=-1, keepdims=True)
    xc = x - mu
    var = jnp.mean(xc * xc, axis=-1, keepdims=True)
    return xc * lax.rsqrt(var + LN_EPS)


def _ada_kernel(c_ref, w_ref, b_ref, o_ref):
    s = _silu(c_ref[...]).astype(BF16)
    o_ref[...] = jnp.dot(s, w_ref[...].astype(BF16), preferred_element_type=F32) + b_ref[...]


def _ada(cvec, w_ada, b_ada):
    rows, d = cvec.shape
    n = w_ada.shape[1]
    tn = 768
    return pl.pallas_call(
        _ada_kernel,
        grid=(n // tn,),
        in_specs=[pl.BlockSpec((rows, d), lambda j: (0, 0)),
                  pl.BlockSpec((d, tn), lambda j: (0, j)),
                  pl.BlockSpec((1, tn), lambda j: (0, j))],
        out_specs=pl.BlockSpec((rows, tn), lambda j: (0, j)),
        out_shape=jax.ShapeDtypeStruct((rows, n), F32),
        compiler_params=pltpu.CompilerParams(dimension_semantics=("arbitrary",),
                                             vmem_limit_bytes=VMEM_LIMIT_BYTES),
        name="ada",
    )(cvec, w_ada, b_ada)


def _rope(x, cos, sin):
    outs = []
    for c in range(x.shape[1] // LANES):
        xc = x[:, c * LANES:(c + 1) * LANES]
        outs.append(xc * cos + pltpu.roll(xc, LANES // 2, axis=1) * sin)
    return jnp.concatenate(outs, axis=1)


def _inproj_kernel(kinds, has_rope, *refs):
    n_blk = len(kinds)
    x_ref, mod_ref = refs[:2]
    w_refs = refs[2:2 + n_blk]
    b_refs = refs[2 + n_blk:2 + 2 * n_blk]
    n_in = 2 + 2 * n_blk + (2 if has_rope else 0)
    cos_ref, sin_ref = refs[n_in - 2:n_in] if has_rope else (None, None)
    out_refs = refs[n_in:]
    d = x_ref.shape[1]

    mod = mod_ref[0]
    h = (_layer_norm(x_ref[...]) * (1.0 + mod[:, d:2 * d]) + mod[:, :d]).astype(BF16)

    o = 0
    for kind, w_ref, b_ref in zip(kinds, w_refs, b_refs):
        acc = jnp.dot(h, w_ref[...], preferred_element_type=F32) + b_ref[...]
        if kind == "q":
            qr_ref, q_ref = out_refs[o:o + 2]
            o += 2
            qs = acc * SCORE_SCALE
            q_ref[...] = qs.astype(q_ref.dtype)
            qr_ref[...] = _rope(qs, cos_ref[...], sin_ref[...]).astype(qr_ref.dtype)
        else:
            val = _rope(acc, cos_ref[...], sin_ref[...]) if kind == "k" else acc
            out_refs[o][...] = val.astype(out_refs[o].dtype)
            o += 1


def _inproj(x2d, mod3, mod_rows_per_tile, w_bf16, b_in, col_blocks, kinds, out_dtypes, rope, tm):
    t, d = x2d.shape
    has_rope = rope is not None
    resident = pl.Buffered(1)
    in_specs = [
        pl.BlockSpec((tm, d), lambda i: (i, 0)),
        pl.BlockSpec((1, 1, mod3.shape[2]), lambda i: (mod_rows_per_tile(i), 0, 0)),
    ]
    in_specs += [pl.BlockSpec((d, d), lambda i, c=c: (0, c), pipeline_mode=resident) for c in col_blocks]
    in_specs += [pl.BlockSpec((1, d), lambda i, c=c: (0, c), pipeline_mode=resident) for c in col_blocks]
    args = [x2d, mod3] + [w_bf16] * len(col_blocks) + [b_in] * len(col_blocks)
    if has_rope:
        tiles_per_seq = rope[0].shape[0] // tm
        for tab in rope:
            in_specs.append(pl.BlockSpec((tm, LANES), lambda i: (i % tiles_per_seq, 0)))
            args.append(tab)
    out_specs = [pl.BlockSpec((tm, d), lambda i: (i, 0)) for _ in out_dtypes]
    out_shape = [jax.ShapeDtypeStruct((t, d), dt) for dt in out_dtypes]
    return pl.pallas_call(
        functools.partial(_inproj_kernel, tuple(kinds), has_rope),
        grid=(t // tm,),
        in_specs=in_specs,
        out_specs=out_specs,
        out_shape=out_shape,
        compiler_params=pltpu.CompilerParams(dimension_semantics=("arbitrary",),
                                             vmem_limit_bytes=VMEM_LIMIT_BYTES),
        name="inproj_rope" if has_rope else "inproj_ctx",
    )(*args)


def _rotary_column_order(w):
    half = HEAD_DIM // 4
    lead = w.shape[:-1]
    w = w.reshape(lead + (w.shape[-1] // LANES, 2, 2, 2, half))
    n = len(lead)
    w = w.transpose(tuple(range(n)) + (n, n + 3, n + 1, n + 2, n + 4))
    return w.reshape(lead + (-1,))


def _rope_tables(seq):
    half = HEAD_DIM // 4
    inv_freq = ROPE_BASE ** (-np.arange(half, dtype=np.float64) / half)
    t = np.arange(seq)
    row, col = t // GRID_W, t % GRID_W
    lane = np.arange(LANES)
    pos = np.where((lane % (2 * half) < half)[None, :], row[:, None], col[:, None]).astype(np.float64)
    ang = pos * inv_freq[lane % half][None, :]
    sign = np.where(lane < LANES // 2, -1.0, 1.0)[None, :]
    return jnp.asarray(np.cos(ang), F32), jnp.asarray(np.sin(ang) * sign, F32)


def _window_geometry(n_blocks, n_rows):
    cases = {}
    for blk in range(n_blocks):
        ws = min(max(blk - (WIN_CHUNKS // 2), 0), n_blocks - WIN_CHUNKS)
        v = blk - ws
        geo = []
        for a in range(QUERY_ROWS):
            r = blk * QUERY_ROWS + a
            rs = min(max(r - NA_ROWS // 2, 0), n_rows - NA_ROWS)
            wr0 = ws * QUERY_ROWS
            geo.append((rs - wr0, rs - wr0 + NA_ROWS, wr0 - r + NA_ROWS - 1))
        geo = tuple(geo)
        assert cases.setdefault(v, geo) == geo
    assert sorted(cases) == list(range(WIN_CHUNKS))
    return [cases[v] for v in range(WIN_CHUNKS)]


def _bias_kernel(geometry, rpb_ref, o_ref):
    table = rpb_ref[0] * LOG2_E
    lane = lax.broadcasted_iota(jnp.int32, (GRID_W, LANES), 1)
    qc = lax.broadcasted_iota(jnp.int32, (GRID_W, LANES), 0)
    kc = lane % GRID_W
    cs = jnp.clip(qc - NA_COLS // 2, 0, GRID_W - NA_COLS)
    col_ok = (kc >= cs) & (kc < cs + NA_COLS)
    low_half = lane < GRID_W

    @functools.cache
    def shifted(dr, half):
        row = jnp.broadcast_to(table[dr:dr + 1, :], (GRID_W, LANES))
        shift = (LANES - (NA_COLS - 1) + GRID_W * half) % LANES
        return pltpu.roll(row, shift, axis=1, stride=1, stride_axis=0)

    neg = jnp.full((GRID_W, LANES), NEG, F32)
    for v, geo in enumerate(geometry):
        for a, (lo, hi, dr0) in enumerate(geo):
            pairs = []
            for jr in range(0, WIN_CHUNKS * QUERY_ROWS, 2):
                left = shifted(dr0 + jr, 0) if lo <= jr < hi else neg
                right = shifted(dr0 + jr + 1, 1) if lo <= jr + 1 < hi else neg
                pairs.append(jnp.where(col_ok, jnp.where(low_half, left, right), NEG))
            o_ref[v, 0, a * GRID_W:(a + 1) * GRID_W, :] = jnp.concatenate(pairs, axis=1)


def _bias_tables(rpb, n_blocks, n_rows):
    n_heads, n_dr, n_dc = rpb.shape
    rpb_pad = jnp.pad(rpb, ((0, 0), (0, 16 - n_dr), (0, LANES - n_dc)))
    geometry = _window_geometry(n_blocks, n_rows)
    return pl.pallas_call(
        functools.partial(_bias_kernel, geometry),
        grid=(n_heads,),
        in_specs=[pl.BlockSpec((1, 16, LANES), lambda h: (h, 0, 0))],
        out_specs=pl.BlockSpec((WIN_CHUNKS, 1, QB, WIN), lambda h: (0, h, 0, 0)),
        out_shape=jax.ShapeDtypeStruct((WIN_CHUNKS, n_heads, QB, WIN), F32),
        compiler_params=pltpu.CompilerParams(dimension_semantics=("arbitrary",),
                                             vmem_limit_bytes=VMEM_LIMIT_BYTES),
        name="bias_tables",
    )(rpb_pad)


def _mix_kernel(seq, x_ref, mod_ref, u_ref, up_ref, un_ref, zp_ref, za_ref, qr_ref, q_ref, *rest):
    k_refs = rest[:WIN_CHUNKS]
    v_refs = rest[WIN_CHUNKS:2 * WIN_CHUNKS]
    (kc_ref, vc_ref, bias_ref, wp_ref, ps_ref, wo_ref, bo_ref, g_ref, be_ref,
     o_ref, cat_ref) = rest[2 * WIN_CHUNKS:]
    d = x_ref.shape[1]
    d_pool = ps_ref.shape[1]
    gdim = d_pool // len(POOL_WINDOWS)
    n_heads = (cat_ref.shape[1] - d_pool) // HEAD_DIM
    i = pl.program_id(1)
    n_blk = pl.num_programs(1)

    u = u_ref[...]
    prev = jnp.where(i > 0, up_ref[...], 0.0)
    nxt = jnp.where(i < n_blk - 1, un_ref[...], 0.0)
    ext = jnp.concatenate([prev, u, nxt], axis=0)
    n_ext = ext.shape[0]
    t = i * QB + lax.broadcasted_iota(jnp.int32, (QB, 1), 0)

    def shift_sum(s, k):
        return pltpu.roll(s, k, axis=0) + pltpu.roll(s, n_ext - k, axis=0)

    def pool_group(g):
        w = POOL_WINDOWS[g]
        cols = slice(g * gdim, (g + 1) * gdim)
        s = ext[:, cols]
        s = s + pltpu.roll(s, 1, axis=0)
        k = 1
        while 2 * k < w:
            s = shift_sum(s, k)
            k *= 2
        s = s[POOL_HALO:POOL_HALO + QB]
        lo = jnp.clip(t - w // 2, 0, seq)
        hi = jnp.clip(t - w // 2 + w, 0, seq)
        pooled = s / (hi - lo).astype(F32) - u[:, cols]
        y = jnp.dot(pooled.astype(BF16), wp_ref[g], preferred_element_type=F32)
        y = y * ps_ref[:, cols] * _silu(zp_ref[:, cols].astype(F32))
        cat_ref[:, cols] = y.astype(BF16)

    nt_dims = (((1,), (1,)), ((), ()))
    lane = lax.broadcasted_iota(jnp.int32, (1, LANES), 1)
    first_head = lane < HEAD_DIM
    first_head_qk = lane % HEAD_DIM < HEAD_DIM // 2
    ones_win = jnp.ones((WIN, LANES), BF16)
    ones_ctx = jnp.ones((kc_ref.shape[0], LANES), BF16)

    def stack_heads(qp):
        zero = jnp.zeros_like(qp)
        return jnp.concatenate([jnp.where(first_head_qk, qp, zero), jnp.where(first_head_qk, zero, qp)], axis=0)

    def lane_tiles(a):
        return [a[:, c * LANES:(c + 1) * LANES] for c in range(a.shape[1] // LANES)]

    def chunk(p):
        return slice(p * LANES, (p + 1) * LANES)

    def scores(p):
        k_win = jnp.concatenate([r[:, chunk(p)] for r in k_refs], axis=0)
        bias = bias_ref[0, 2 * p:2 * p + 2].reshape(2 * QB, WIN)
        s_loc = lax.dot_general(stack_heads(qr_ref[:, chunk(p)]), k_win, nt_dims,
                                preferred_element_type=F32) + bias
        s_ctx = lax.dot_general(stack_heads(q_ref[:, chunk(p)]), kc_ref[:, chunk(p)], nt_dims,
                                preferred_element_type=F32)
        return s_loc, s_ctx

    def softmax_numerators(s_loc, s_ctx):
        m = jnp.max(functools.reduce(jnp.maximum, lane_tiles(s_loc) + lane_tiles(s_ctx)), axis=1, keepdims=True)
        return jnp.concatenate([jnp.exp2(s_loc - m), jnp.exp2(s_ctx - m)], axis=1).astype(BF16)

    def weighted_values(p, probs):
        v_win = jnp.concatenate([r[:, chunk(p)] for r in v_refs], axis=0)
        v_aug = jnp.concatenate([jnp.concatenate([v_win, ones_win], axis=1),
                                 jnp.concatenate([vc_ref[:, chunk(p)], ones_ctx], axis=1)], axis=0)
        return jnp.dot(probs, v_aug, preferred_element_type=F32)

    def emit(p, o):
        num = jnp.where(first_head, o[:QB, :LANES], o[QB:, :LANES])
        den = jnp.where(first_head, o[:QB, LANES:], o[QB:, LANES:])
        out = num / den * _silu(za_ref[:, chunk(p)].astype(F32))
        cat_ref[:, d_pool + p * LANES:d_pool + (p + 1) * LANES] = out.astype(BF16)

    def project_out(lo, hi):
        return jnp.dot(cat_ref[:, lo:hi], wo_ref[lo:hi, :], preferred_element_type=F32)

    n_pairs = n_heads // 2
    n_pool = len(POOL_WINDOWS)
    oproj_k = 2 * LANES
    y = bo_ref[...]
    s_next = scores(0)
    for p in range(n_pairs):
        s_cur = s_next
        if p + 1 < n_pairs:
            s_next = scores(p + 1)
        if p < n_pool:
            pool_group(p)
        emit(p, weighted_values(p, softmax_numerators(*s_cur)))
    y = y + project_out(0, cat_ref.shape[1])

    gate = mod_ref[0][:, 2 * d:3 * d]
    z = DEEPNORM_ALPHA * x_ref[...] + gate * y
    o_ref[...] = _layer_norm(z) * g_ref[...] + be_ref[...]


def _mix(x2d, mod3, u, zp, za, qr, q, kr, v, kc, vc, bias, w_pool, pool_scale, w_out, b_out, ln_g, ln_b,
         batch, seq, ctx_len):
    t, d = x2d.shape
    n_blk = seq // QB
    halo_per_blk = QB // POOL_HALO
    n_halo = t // POOL_HALO
    d_mix = w_out.shape[0]

    def win_start(i):
        return jnp.clip(i - WIN_CHUNKS // 2, 0, n_blk - WIN_CHUNKS)

    def tok(b, i):
        return (b * n_blk + i, 0)

    def win_spec(j):
        return pl.BlockSpec((QB, d), lambda b, i: (b * n_blk + win_start(i) + j, 0))

    full = lambda shape: pl.BlockSpec(shape, lambda b, i: (0,) * len(shape))
    in_specs = [
        pl.BlockSpec((QB, d), tok),
        pl.BlockSpec((1, 1, mod3.shape[2]), lambda b, i: (b, 0, 0)),
        pl.BlockSpec((QB, d), tok),
        pl.BlockSpec((POOL_HALO, d),
                     lambda b, i: (jnp.maximum((b * n_blk + i) * halo_per_blk - 1, 0), 0)),
        pl.BlockSpec((POOL_HALO, d),
                     lambda b, i: (jnp.minimum((b * n_blk + i + 1) * halo_per_blk, n_halo - 1), 0)),
        pl.BlockSpec((QB, d), tok),
        pl.BlockSpec((QB, d), tok),
        pl.BlockSpec((QB, d), tok),
        pl.BlockSpec((QB, d), tok),
    ]
    in_specs += [win_spec(j) for j in range(WIN_CHUNKS)]
    in_specs += [win_spec(j) for j in range(WIN_CHUNKS)]
    in_specs += [
        pl.BlockSpec((ctx_len, d), lambda b, i: (b, 0)),
        pl.BlockSpec((ctx_len, d), lambda b, i: (b, 0)),
        pl.BlockSpec((1,) + bias.shape[1:], lambda b, i: (i - win_start(i), 0, 0, 0)),
        full(w_pool.shape), full(pool_scale.shape), full(w_out.shape),
        full(b_out.shape), full(ln_g.shape), full(ln_b.shape),
    ]
    args = [x2d, mod3, u, u, u, zp, za, qr, q] + [kr] * WIN_CHUNKS + [v] * WIN_CHUNKS + [
        kc, vc, bias, w_pool, pool_scale, w_out, b_out, ln_g, ln_b]
    return pl.pallas_call(
        functools.partial(_mix_kernel, seq),
        grid=(batch, n_blk),
        in_specs=in_specs,
        out_specs=pl.BlockSpec((QB, d), tok),
        out_shape=jax.ShapeDtypeStruct((t, d), F32),
        scratch_shapes=[pltpu.VMEM((QB, d_mix), BF16)],
        compiler_params=pltpu.CompilerParams(dimension_semantics=("arbitrary", "arbitrary"),
                                             vmem_limit_bytes=VMEM_LIMIT_BYTES),
        name="mix",
    )(*args)


def kernel(x, c, ctx, c_ctx, w_ada, b_ada, w_in, b_in, w_pool, pool_scale, rpb, w_out, b_out, ln_g, ln_b):
    batch, seq, d = x.shape
    ctx_len = ctx.shape[1]
    assert w_ada.shape[0] == DEPTH == 1
    assert seq % GRID_W == 0 and seq % INPROJ_TM == 0 and (batch * ctx_len) % INPROJ_TM == 0
    n_rows = seq // GRID_W
    assert n_rows >= NA_ROWS and seq // QB >= WIN_CHUNKS
    d_pool = pool_scale.shape[1]
    assert d_pool == d and w_in.shape[2] == 6 * d and rpb.shape[1] * HEAD_DIM == d

    cvec = jnp.zeros((8, d), F32).at[:batch].set(c).at[batch].set(c_ctx)
    mod3 = _ada(cvec, w_ada[0], b_ada[0][None, :]).reshape(8, 1, 3 * d)

    def qk_rotary_order(a):
        return jnp.concatenate([a[..., :2 * d], _rotary_column_order(a[..., 2 * d:4 * d]), a[..., 4 * d:]], axis=-1)

    w_in_b = qk_rotary_order(w_in[0]).astype(BF16)
    b_in2 = qk_rotary_order(b_in[0])[None, :]
    x2d = x.reshape(batch * seq, d)
    tiles_per_seq = seq // INPROJ_TM
    u, zp, qr, q, kr, v, za = _inproj(
        x2d, mod3, lambda i: i // tiles_per_seq, w_in_b, b_in2, range(6),
        ("plain", "plain", "q", "k", "plain", "plain"), (F32, BF16, BF16, BF16, BF16, BF16, BF16),
        _rope_tables(seq), INPROJ_TM)
    kc, vc = _inproj(
        ctx.reshape(batch * ctx_len, d), mod3, lambda i: batch, w_in_b, b_in2, (3, 4),
        ("plain", "plain"), (BF16, BF16), None, INPROJ_TM)

    bias = _bias_tables(rpb[0], seq // QB, n_rows)
    out = _mix(x2d, mod3, u, zp, za, qr, q, kr, v, kc, vc, bias,
               w_pool[0].astype(BF16), pool_scale, w_out[0].astype(BF16), b_out, ln_g, ln_b,
               batch, seq, ctx_len)
    return out.reshape(batch, seq, d)
```

```python
import functools

import numpy as np
import jax
import jax.numpy as jnp
from jax import lax
from jax.experimental import pallas as pl
from jax.experimental.pallas import tpu as pltpu

GRID_W = 64
POOL_WINDOWS = (2, 4, 8, 16)
HEAD_DIM = 64
NA_ROWS = 8
NA_COLS = 16
ROPE_BASE = 10000.0
LN_EPS = 1e-6
LOG2_E = float(np.log2(np.e))
SCORE_SCALE = HEAD_DIM ** -0.5 * LOG2_E
DEPTH = 1
DEEPNORM_ALPHA = (2.0 * DEPTH) ** 0.25

LANES = 128
VMEM_LIMIT_BYTES = 52 * 1024 * 1024

QUERY_ROWS = 2
QB = QUERY_ROWS * GRID_W
WIN = NA_ROWS * GRID_W
UNION_ROWS = NA_ROWS + QUERY_ROWS - 1
POOL_HALO = 8
NEG = -1e30
INPROJ_TM = 512
F32 = jnp.float32
BF16 = jnp.bfloat16


def _silu(x):
    return x * jax.nn.sigmoid(x)


def _layer_norm(x):
    mu = jnp.mean(x, axis=-1, keepdims=True)
    xc = x - mu
    var = jnp.mean(xc * xc, axis=-1, keepdims=True)
    return xc * lax.rsqrt(var + LN_EPS)


def _ada_kernel(c_ref, w_ref, b_ref, o_ref):
    s = _silu(c_ref[...]).astype(BF16)
    o_ref[...] = jnp.dot(s, w_ref[...].astype(BF16), preferred_element_type=F32) + b_ref[...]


def _ada(cvec, w_ada, b_ada):
    rows, d = cvec.shape
    n = w_ada.shape[1]
    tn = 768
    return pl.pallas_call(
        _ada_kernel,
        grid=(n // tn,),
        in_specs=[pl.BlockSpec((rows, d), lambda j: (0, 0)),
                  pl.BlockSpec((d, tn), lambda j: (0, j)),
                  pl.BlockSpec((1, tn), lambda j: (0, j))],
        out_specs=pl.BlockSpec((rows, tn), lambda j: (0, j)),
        out_shape=jax.ShapeDtypeStruct((rows, n), F32),
        compiler_params=pltpu.CompilerParams(dimension_semantics=("arbitrary",),
                                             vmem_limit_bytes=VMEM_LIMIT_BYTES),
        name="ada",
    )(cvec, w_ada, b_ada)


def _rope(x, cos, sin):
    outs = []
    for c in range(x.shape[1] // LANES):
        xc = x[:, c * LANES:(c + 1) * LANES]
        outs.append(xc * cos + pltpu.roll(xc, LANES // 2, axis=1) * sin)
    return jnp.concatenate(outs, axis=1)


def _inproj_kernel(kinds, has_rope, *refs):
    n_blk = len(kinds)
    x_ref, mod_ref = refs[:2]
    w_refs = refs[2:2 + n_blk]
    b_refs = refs[2 + n_blk:2 + 2 * n_blk]
    n_in = 2 + 2 * n_blk + (2 if has_rope else 0)
    cos_ref, sin_ref = refs[n_in - 2:n_in] if has_rope else (None, None)
    out_refs = refs[n_in:]
    d = x_ref.shape[1]

    mod = mod_ref[0]
    h = (_layer_norm(x_ref[...]) * (1.0 + mod[:, d:2 * d]) + mod[:, :d]).astype(BF16)

    o = 0
    for kind, w_ref, b_ref in zip(kinds, w_refs, b_refs):
        acc = jnp.dot(h, w_ref[...], preferred_element_type=F32) + b_ref[...]
        if kind == "q":
            qr_ref, q_ref = out_refs[o:o + 2]
            o += 2
            qs = acc * SCORE_SCALE
            q_ref[...] = qs.astype(q_ref.dtype)
            qr_ref[...] = _rope(qs, cos_ref[...], sin_ref[...]).astype(qr_ref.dtype)
        else:
            val = _rope(acc, cos_ref[...], sin_ref[...]) if kind == "k" else acc
            out_refs[o][...] = val.astype(out_refs[o].dtype)
            o += 1


def _inproj(x2d, mod3, mod_rows_per_tile, w_bf16, b_in, col_blocks, kinds, out_dtypes, rope, tm):
    t, d = x2d.shape
    has_rope = rope is not None
    resident = pl.Buffered(1)
    in_specs = [
        pl.BlockSpec((tm, d), lambda i: (i, 0)),
        pl.BlockSpec((1, 1, mod3.shape[2]), lambda i: (mod_rows_per_tile(i), 0, 0)),
    ]
    in_specs += [pl.BlockSpec((d, d), lambda i, c=c: (0, c), pipeline_mode=resident) for c in col_blocks]
    in_specs += [pl.BlockSpec((1, d), lambda i, c=c: (0, c), pipeline_mode=resident) for c in col_blocks]
    args = [x2d, mod3] + [w_bf16] * len(col_blocks) + [b_in] * len(col_blocks)
    if has_rope:
        tiles_per_seq = rope[0].shape[0] // tm
        for tab in rope:
            in_specs.append(pl.BlockSpec((tm, LANES), lambda i: (i % tiles_per_seq, 0)))
            args.append(tab)
    out_specs = [pl.BlockSpec((tm, d), lambda i: (i, 0)) for _ in out_dtypes]
    out_shape = [jax.ShapeDtypeStruct((t, d), dt) for dt in out_dtypes]
    return pl.pallas_call(
        functools.partial(_inproj_kernel, tuple(kinds), has_rope),
        grid=(t // tm,),
        in_specs=in_specs,
        out_specs=out_specs,
        out_shape=out_shape,
        compiler_params=pltpu.CompilerParams(dimension_semantics=("arbitrary",),
                                             vmem_limit_bytes=VMEM_LIMIT_BYTES),
        name="inproj_rope" if has_rope else "inproj_ctx",
    )(*args)


def _rotary_column_order(w):
    half = HEAD_DIM // 4
    lead = w.shape[:-1]
    w = w.reshape(lead + (w.shape[-1] // LANES, 2, 2, 2, half))
    n = len(lead)
    w = w.transpose(tuple(range(n)) + (n, n + 3, n + 1, n + 2, n + 4))
    return w.reshape(lead + (-1,))


def _rope_tables(seq):
    half = HEAD_DIM // 4
    inv_freq = ROPE_BASE ** (-np.arange(half, dtype=np.float64) / half)
    t = np.arange(seq)
    row, col = t // GRID_W, t % GRID_W
    lane = np.arange(LANES)
    pos = np.where((lane % (2 * half) < half)[None, :], row[:, None], col[:, None]).astype(np.float64)
    ang = pos * inv_freq[lane % half][None, :]
    sign = np.where(lane < LANES // 2, -1.0, 1.0)[None, :]
    return jnp.asarray(np.cos(ang), F32), jnp.asarray(np.sin(ang) * sign, F32)


def _window_row_start(r, n_rows):
    return jnp.clip(r - NA_ROWS // 2, 0, n_rows - NA_ROWS)


def _bias_kernel(rpb_ref, o_ref):
    table = rpb_ref[0] * LOG2_E
    lane = lax.broadcasted_iota(jnp.int32, (GRID_W, LANES), 1)
    qc = lax.broadcasted_iota(jnp.int32, (GRID_W, LANES), 0)
    kc = lane % GRID_W
    cs = jnp.clip(qc - NA_COLS // 2, 0, GRID_W - NA_COLS)
    col_ok = (kc >= cs) & (kc < cs + NA_COLS)
    low_half = lane < GRID_W

    @functools.cache
    def shifted(dr, half):
        row = jnp.broadcast_to(table[dr:dr + 1, :], (GRID_W, LANES))
        shift = (LANES - (NA_COLS - 1) + GRID_W * half) % LANES
        return pltpu.roll(row, shift, axis=1, stride=1, stride_axis=0)

    for v in range(NA_ROWS):
        pairs = [jnp.where(col_ok, jnp.where(low_half, shifted(v + jr, 0), shifted(v + jr + 1, 1)), NEG)
                 for jr in range(0, NA_ROWS, 2)]
        o_ref[v, 0] = jnp.concatenate(pairs, axis=1)


def _bias_tables(rpb):
    n_heads, n_dr, n_dc = rpb.shape
    assert n_dr == 2 * NA_ROWS - 1 and n_dc == 2 * NA_COLS - 1
    rpb_pad = jnp.pad(rpb, ((0, 0), (0, 16 - n_dr), (0, LANES - n_dc)))
    return pl.pallas_call(
        _bias_kernel,
        grid=(n_heads,),
        in_specs=[pl.BlockSpec((1, 16, LANES), lambda h: (h, 0, 0))],
        out_specs=pl.BlockSpec((NA_ROWS, 1, GRID_W, WIN), lambda h: (0, h, 0, 0)),
        out_shape=jax.ShapeDtypeStruct((NA_ROWS, n_heads, GRID_W, WIN), F32),
        compiler_params=pltpu.CompilerParams(dimension_semantics=("arbitrary",),
                                             vmem_limit_bytes=VMEM_LIMIT_BYTES),
        name="bias_tables",
    )(rpb_pad)


def _union_row_start(i, n_rows):
    return jnp.minimum(_window_row_start(i * QUERY_ROWS, n_rows), n_rows - UNION_ROWS)


def _mix_kernel(seq, x_ref, mod_ref, u_ref, up_ref, un_ref, zp_ref, za_ref, qr_ref, q_ref, k_hbm, v_hbm, *rest):
    bias_refs = rest[:QUERY_ROWS]
    (kc_ref, vc_ref, wp_ref, ps_ref, wo_ref, bo_ref, g_ref, be_ref,
     o_ref, cat_ref, k_buf, v_buf, win_sem) = rest[QUERY_ROWS:]
    d = x_ref.shape[1]
    d_pool = ps_ref.shape[1]
    gdim = d_pool // len(POOL_WINDOWS)
    n_heads = (cat_ref.shape[1] - d_pool) // HEAD_DIM
    n_rows = seq // GRID_W
    i = pl.program_id(1)
    n_blk = pl.num_programs(1)

    step = pl.program_id(0) * n_blk + i
    n_steps = pl.num_programs(0) * n_blk
    slot = step % 2

    def window_copies(s, into):
        tok0 = pl.multiple_of((s // n_blk) * seq + _union_row_start(s % n_blk, n_rows) * GRID_W, GRID_W)
        return [pltpu.make_async_copy(hbm.at[pl.ds(tok0, UNION_ROWS * GRID_W)], buf.at[into], win_sem.at[n, into])
                for n, (hbm, buf) in enumerate(((k_hbm, k_buf), (v_hbm, v_buf)))]

    @pl.when(step == 0)
    def _():
        for cp in window_copies(step, slot):
            cp.start()

    for cp in window_copies(step, slot):
        cp.wait()

    @pl.when(step + 1 < n_steps)
    def _():
        for cp in window_copies(step + 1, 1 - slot):
            cp.start()

    win_off = [pl.multiple_of((_window_row_start(i * QUERY_ROWS + a, n_rows) - _union_row_start(i, n_rows)) * GRID_W,
                              GRID_W) for a in range(QUERY_ROWS)]

    u = u_ref[...]
    prev = jnp.where(i > 0, up_ref[...], 0.0)
    nxt = jnp.where(i < n_blk - 1, un_ref[...], 0.0)
    ext = jnp.concatenate([prev, u, nxt], axis=0)
    n_ext = ext.shape[0]
    t = i * QB + lax.broadcasted_iota(jnp.int32, (QB, 1), 0)

    def shift_sum(s, k):
        return pltpu.roll(s, k, axis=0) + pltpu.roll(s, n_ext - k, axis=0)

    def pool_group(g):
        w = POOL_WINDOWS[g]
        cols = slice(g * gdim, (g + 1) * gdim)
        s = ext[:, cols]
        s = s + pltpu.roll(s, 1, axis=0)
        k = 1
        while 2 * k < w:
            s = shift_sum(s, k)
            k *= 2
        s = s[POOL_HALO:POOL_HALO + QB]
        lo = jnp.clip(t - w // 2, 0, seq)
        hi = jnp.clip(t - w // 2 + w, 0, seq)
        pooled = s / (hi - lo).astype(F32) - u[:, cols]
        y = jnp.dot(pooled.astype(BF16), wp_ref[g], preferred_element_type=F32)
        y = y * ps_ref[:, cols] * _silu(zp_ref[:, cols].astype(F32))
        cat_ref[:, cols] = y.astype(BF16)

    nt_dims = (((1,), (1,)), ((), ()))
    lane = lax.broadcasted_iota(jnp.int32, (1, LANES), 1)
    first_head = lane < HEAD_DIM
    first_head_qk = lane % HEAD_DIM < HEAD_DIM // 2
    ones_win = jnp.ones((WIN, LANES), BF16)
    ones_ctx = jnp.ones((kc_ref.shape[0], LANES), BF16)
    stacked = 2 * GRID_W

    def chunk(p):
        return slice(p * LANES, (p + 1) * LANES)

    def stack_heads(ref, p):
        qp = ref[:, chunk(p)]
        zero = jnp.zeros_like(qp)
        first, second = jnp.where(first_head_qk, qp, zero), jnp.where(first_head_qk, zero, qp)
        pieces = []
        for a in range(QUERY_ROWS):
            pieces += [first[a * GRID_W:(a + 1) * GRID_W], second[a * GRID_W:(a + 1) * GRID_W]]
        return jnp.concatenate(pieces, axis=0)

    def lane_tiles(a):
        return [a[:, c * LANES:(c + 1) * LANES] for c in range(a.shape[1] // LANES)]

    def scores(p):
        q_rot = stack_heads(qr_ref, p)
        s_loc = []
        for a in range(QUERY_ROWS):
            k_win = k_buf[slot, pl.ds(win_off[a], WIN), chunk(p)]
            bias = bias_refs[a][0, 2 * p:2 * p + 2].reshape(stacked, WIN)
            s_loc.append(lax.dot_general(q_rot[a * stacked:(a + 1) * stacked], k_win, nt_dims,
                                         preferred_element_type=F32) + bias)
        s_ctx = lax.dot_general(stack_heads(q_ref, p), kc_ref[:, chunk(p)], nt_dims,
                                preferred_element_type=F32)
        return jnp.concatenate(s_loc, axis=0), s_ctx

    def softmax_numerators(s_loc, s_ctx):
        m = jnp.max(functools.reduce(jnp.maximum, lane_tiles(s_loc) + lane_tiles(s_ctx)), axis=1, keepdims=True)
        return jnp.concatenate([jnp.exp2(s_loc - m), jnp.exp2(s_ctx - m)], axis=1).astype(BF16)

    def weighted_values(p, probs):
        vc_aug = jnp.concatenate([vc_ref[:, chunk(p)], ones_ctx], axis=1)
        outs = []
        for a in range(QUERY_ROWS):
            v_win = v_buf[slot, pl.ds(win_off[a], WIN), chunk(p)]
            v_aug = jnp.concatenate([jnp.concatenate([v_win, ones_win], axis=1), vc_aug], axis=0)
            outs.append(jnp.dot(probs[a * stacked:(a + 1) * stacked], v_aug, preferred_element_type=F32))
        return outs

    def emit(p, outs):
        num = jnp.concatenate([jnp.where(first_head, o[:GRID_W, :LANES], o[GRID_W:, :LANES]) for o in outs], axis=0)
        den = jnp.concatenate([jnp.where(first_head, o[:GRID_W, LANES:], o[GRID_W:, LANES:]) for o in outs], axis=0)
        out = num / den * _silu(za_ref[:, chunk(p)].astype(F32))
        cat_ref[:, d_pool + p * LANES:d_pool + (p + 1) * LANES] = out.astype(BF16)

    def project_out(lo, hi):
        return jnp.dot(cat_ref[:, lo:hi], wo_ref[lo:hi, :], preferred_element_type=F32)

    n_pairs = n_heads // 2
    n_pool = len(POOL_WINDOWS)
    oproj_k = 2 * LANES
    y = bo_ref[...]
    s_next = scores(0)
    for p in range(n_pairs):
        s_cur = s_next
        if p + 1 < n_pairs:
            s_next = scores(p + 1)
        if p < n_pool:
            pool_group(p)
        emit(p, weighted_values(p, softmax_numerators(*s_cur)))
    y = y + project_out(0, cat_ref.shape[1])

    gate = mod_ref[0][:, 2 * d:3 * d]
    z = DEEPNORM_ALPHA * x_ref[...] + gate * y
    o_ref[...] = _layer_norm(z) * g_ref[...] + be_ref[...]


def _mix(x2d, mod3, u, zp, za, qr, q, kr, v, kc, vc, bias, w_pool, pool_scale, w_out, b_out, ln_g, ln_b,
         batch, seq, ctx_len):
    t, d = x2d.shape
    n_blk = seq // QB
    halo_per_blk = QB // POOL_HALO
    n_halo = t // POOL_HALO
    d_mix = w_out.shape[0]

    n_rows = seq // GRID_W

    def tok(b, i):
        return (b * n_blk + i, 0)

    def bias_spec(a):
        def index(b, i):
            r = i * QUERY_ROWS + a
            return (_window_row_start(r, n_rows) - r + NA_ROWS - 1, 0, 0, 0)
        return pl.BlockSpec((1,) + bias.shape[1:], index)

    full = lambda shape: pl.BlockSpec(shape, lambda b, i: (0,) * len(shape))
    in_specs = [
        pl.BlockSpec((QB, d), tok),
        pl.BlockSpec((1, 1, mod3.shape[2]), lambda b, i: (b, 0, 0)),
        pl.BlockSpec((QB, d), tok),
        pl.BlockSpec((POOL_HALO, d),
                     lambda b, i: (jnp.maximum((b * n_blk + i) * halo_per_blk - 1, 0), 0)),
        pl.BlockSpec((POOL_HALO, d),
                     lambda b, i: (jnp.minimum((b * n_blk + i + 1) * halo_per_blk, n_halo - 1), 0)),
        pl.BlockSpec((QB, d), tok),
        pl.BlockSpec((QB, d), tok),
        pl.BlockSpec((QB, d), tok),
        pl.BlockSpec((QB, d), tok),
    ]
    in_specs += [pl.BlockSpec(memory_space=pl.ANY)] * 2
    in_specs += [bias_spec(a) for a in range(QUERY_ROWS)]
    in_specs += [
        pl.BlockSpec((ctx_len, d), lambda b, i: (b, 0)),
        pl.BlockSpec((ctx_len, d), lambda b, i: (b, 0)),
        full(w_pool.shape), full(pool_scale.shape), full(w_out.shape),
        full(b_out.shape), full(ln_g.shape), full(ln_b.shape),
    ]
    args = ([x2d, mod3, u, u, u, zp, za, qr, q, kr, v]
            + [bias] * QUERY_ROWS + [kc, vc, w_pool, pool_scale, w_out, b_out, ln_g, ln_b])
    return pl.pallas_call(
        functools.partial(_mix_kernel, seq),
        grid=(batch, n_blk),
        in_specs=in_specs,
        out_specs=pl.BlockSpec((QB, d), tok),
        out_shape=jax.ShapeDtypeStruct((t, d), F32),
        scratch_shapes=[pltpu.VMEM((QB, d_mix), BF16),
                        pltpu.VMEM((2, UNION_ROWS * GRID_W, d), BF16),
                        pltpu.VMEM((2, UNION_ROWS * GRID_W, d), BF16),
                        pltpu.SemaphoreType.DMA((2, 2))],
        compiler_params=pltpu.CompilerParams(dimension_semantics=("arbitrary", "arbitrary"),
                                             vmem_limit_bytes=VMEM_LIMIT_BYTES),
        name="mix",
    )(*args)


def kernel(x, c, ctx, c_ctx, w_ada, b_ada, w_in, b_in, w_pool, pool_scale, rpb, w_out, b_out, ln_g, ln_b):
    batch, seq, d = x.shape
    ctx_len = ctx.shape[1]
    assert w_ada.shape[0] == DEPTH == 1
    assert seq % GRID_W == 0 and seq % INPROJ_TM == 0 and (batch * ctx_len) % INPROJ_TM == 0
    assert seq // GRID_W >= NA_ROWS and seq % QB == 0
    d_pool = pool_scale.shape[1]
    assert d_pool == d and w_in.shape[2] == 6 * d and rpb.shape[1] * HEAD_DIM == d

    cvec = jnp.zeros((8, d), F32).at[:batch].set(c).at[batch].set(c_ctx)
    mod3 = _ada(cvec, w_ada[0], b_ada[0][None, :]).reshape(8, 1, 3 * d)

    def qk_rotary_order(a):
        return jnp.concatenate([a[..., :2 * d], _rotary_column_order(a[..., 2 * d:4 * d]), a[..., 4 * d:]], axis=-1)

    w_in_b = qk_rotary_order(w_in[0]).astype(BF16)
    b_in2 = qk_rotary_order(b_in[0])[None, :]
    x2d = x.reshape(batch * seq, d)
    tiles_per_seq = seq // INPROJ_TM
    u, zp, qr, q, kr, v, za = _inproj(
        x2d, mod3, lambda i: i // tiles_per_seq, w_in_b, b_in2, range(6),
        ("plain", "plain", "q", "k", "plain", "plain"), (F32, BF16, BF16, BF16, BF16, BF16, BF16),
        _rope_tables(seq), INPROJ_TM)
    kc, vc = _inproj(
        ctx.reshape(batch * ctx_len, d), mod3, lambda i: batch, w_in_b, b_in2, (3, 4),
        ("plain", "plain"), (BF16, BF16), None, INPROJ_TM)

    bias = _bias_tables(rpb[0])
    out = _mix(x2d, mod3, u, zp, za, qr, q, kr, v, kc, vc, bias,
               w_pool[0].astype(BF16), pool_scale, w_out[0].astype(BF16), b_out, ln_g, ln_b,
               batch, seq, ctx_len)
    return out.reshape(batch, seq, d)
```

```python
import functools

import numpy as np
import jax
import jax.numpy as jnp
from jax import lax
from jax.experimental import pallas as pl
from jax.experimental.pallas import tpu as pltpu

GRID_W = 64
POOL_WINDOWS = (2, 4, 8, 16)
HEAD_DIM = 64
NA_ROWS = 8
NA_COLS = 16
ROPE_BASE = 10000.0
LN_EPS = 1e-6
LOG2_E = float(np.log2(np.e))
SCORE_SCALE = HEAD_DIM ** -0.5 * LOG2_E
DEPTH = 1
DEEPNORM_ALPHA = (2.0 * DEPTH) ** 0.25

LANES = 128
VMEM_LIMIT_BYTES = 52 * 1024 * 1024

QUERY_ROWS = 2
QB = QUERY_ROWS * GRID_W
WIN = NA_ROWS * GRID_W
UNION_ROWS = NA_ROWS + QUERY_ROWS - 1
POOL_HALO = 8
NEG = -1e30
INPROJ_TM = 512
F32 = jnp.float32
BF16 = jnp.bfloat16


def _silu(x):
    return x * jax.nn.sigmoid(x)


def _layer_norm(x):
    mu = jnp.mean(x, axis=-1, keepdims=True)
    xc = x - mu
    var = jnp.mean(xc * xc, axis=-1, keepdims=True)
    return xc * lax.rsqrt(var + LN_EPS)


def _ada_kernel(c_ref, w_ref, b_ref, o_ref):
    s = _silu(c_ref[...]).astype(BF16)
    o_ref[...] = jnp.dot(s, w_ref[...].astype(BF16), preferred_element_type=F32) + b_ref[...]


def _ada(cvec, w_ada, b_ada):
    rows, d = cvec.shape
    n = w_ada.shape[1]
    tn = 768
    return pl.pallas_call(
        _ada_kernel,
        grid=(n // tn,),
        in_specs=[pl.BlockSpec((rows, d), lambda j: (0, 0)),
                  pl.BlockSpec((d, tn), lambda j: (0, j)),
                  pl.BlockSpec((1, tn), lambda j: (0, j))],
        out_specs=pl.BlockSpec((rows, tn), lambda j: (0, j)),
        out_shape=jax.ShapeDtypeStruct((rows, n), F32),
        compiler_params=pltpu.CompilerParams(dimension_semantics=("arbitrary",),
                                             vmem_limit_bytes=VMEM_LIMIT_BYTES),
        name="ada",
    )(cvec, w_ada, b_ada)


def _rope(x, cos, sin):
    half = HEAD_DIM // 4
    first_half = lax.broadcasted_iota(jnp.int32, (1, LANES), 1) % (2 * half) < half
    outs = []
    for c in range(x.shape[1] // LANES):
        xc = x[:, c * LANES:(c + 1) * LANES]
        partner = jnp.where(first_half, pltpu.roll(xc, LANES - half, axis=1), pltpu.roll(xc, half, axis=1))
        outs.append(xc * cos + partner * sin)
    return jnp.concatenate(outs, axis=1)


def _inproj_kernel(kinds, has_rope, *refs):
    n_blk = len(kinds)
    x_ref, mod_ref = refs[:2]
    w_refs = refs[2:2 + n_blk]
    b_refs = refs[2 + n_blk:2 + 2 * n_blk]
    n_in = 2 + 2 * n_blk + (2 if has_rope else 0)
    cos_ref, sin_ref = refs[n_in - 2:n_in] if has_rope else (None, None)
    out_refs = refs[n_in:]
    d = x_ref.shape[1]

    mod = mod_ref[0]
    h = (_layer_norm(x_ref[...]) * (1.0 + mod[:, d:2 * d]) + mod[:, :d]).astype(BF16)

    o = 0
    for kind, w_ref, b_ref in zip(kinds, w_refs, b_refs):
        acc = jnp.dot(h, w_ref[...], preferred_element_type=F32) + b_ref[...]
        if kind == "q":
            qr_ref, q_ref = out_refs[o:o + 2]
            o += 2
            qs = acc * SCORE_SCALE
            q_ref[...] = qs.astype(q_ref.dtype)
            qr_ref[...] = _rope(qs, cos_ref[...], sin_ref[...]).astype(qr_ref.dtype)
        else:
            val = _rope(acc, cos_ref[...], sin_ref[...]) if kind == "k" else acc
            out_refs[o][...] = val.astype(out_refs[o].dtype)
            o += 1


def _inproj(x2d, mod3, mod_rows_per_tile, w_bf16, b_in, col_blocks, kinds, out_dtypes, rope, tm):
    t, d = x2d.shape
    has_rope = rope is not None
    resident = pl.Buffered(1)
    in_specs = [
        pl.BlockSpec((tm, d), lambda i: (i, 0)),
        pl.BlockSpec((1, 1, mod3.shape[2]), lambda i: (mod_rows_per_tile(i), 0, 0)),
    ]
    in_specs += [pl.BlockSpec((d, d), lambda i, c=c: (0, c), pipeline_mode=resident) for c in col_blocks]
    in_specs += [pl.BlockSpec((1, d), lambda i, c=c: (0, c), pipeline_mode=resident) for c in col_blocks]
    args = [x2d, mod3] + [w_bf16] * len(col_blocks) + [b_in] * len(col_blocks)
    if has_rope:
        tiles_per_seq = rope[0].shape[0] // tm
        for tab in rope:
            in_specs.append(pl.BlockSpec((tm, LANES), lambda i: (i % tiles_per_seq, 0)))
            args.append(tab)
    out_specs = [pl.BlockSpec((tm, d), lambda i: (i, 0)) for _ in out_dtypes]
    out_shape = [jax.ShapeDtypeStruct((t, d), dt) for dt in out_dtypes]
    return pl.pallas_call(
        functools.partial(_inproj_kernel, tuple(kinds), has_rope),
        grid=(t // tm,),
        in_specs=in_specs,
        out_specs=out_specs,
        out_shape=out_shape,
        compiler_params=pltpu.CompilerParams(dimension_semantics=("arbitrary",),
                                             vmem_limit_bytes=VMEM_LIMIT_BYTES),
        name="inproj_rope" if has_rope else "inproj_ctx",
    )(*args)


def _rope_tables(seq):
    half = HEAD_DIM // 4
    inv_freq = ROPE_BASE ** (-np.arange(half, dtype=np.float64) / half)
    t = np.arange(seq)
    row, col = t // GRID_W, t % GRID_W
    lane = np.arange(LANES)
    pos = np.where((lane % HEAD_DIM < HEAD_DIM // 2)[None, :], row[:, None], col[:, None]).astype(np.float64)
    ang = pos * inv_freq[lane % half][None, :]
    sign = np.where(lane % (2 * half) < half, -1.0, 1.0)[None, :]
    return jnp.asarray(np.cos(ang), F32), jnp.asarray(np.sin(ang) * sign, F32)


def _window_row_start(r, n_rows):
    return jnp.clip(r - NA_ROWS // 2, 0, n_rows - NA_ROWS)


def _bias_kernel(rpb_ref, o_ref):
    table = rpb_ref[0] * LOG2_E
    lane = lax.broadcasted_iota(jnp.int32, (GRID_W, LANES), 1)
    qc = lax.broadcasted_iota(jnp.int32, (GRID_W, LANES), 0)
    kc = lane % GRID_W
    cs = jnp.clip(qc - NA_COLS // 2, 0, GRID_W - NA_COLS)
    col_ok = (kc >= cs) & (kc < cs + NA_COLS)
    low_half = lane < GRID_W

    @functools.cache
    def shifted(dr, half):
        row = jnp.broadcast_to(table[dr:dr + 1, :], (GRID_W, LANES))
        shift = (LANES - (NA_COLS - 1) + GRID_W * half) % LANES
        return pltpu.roll(row, shift, axis=1, stride=1, stride_axis=0)

    for v in range(NA_ROWS):
        pairs = [jnp.where(col_ok, jnp.where(low_half, shifted(v + jr, 0), shifted(v + jr + 1, 1)), NEG)
                 for jr in range(0, NA_ROWS, 2)]
        o_ref[v, 0] = jnp.concatenate(pairs, axis=1)


def _bias_tables(rpb):
    n_heads, n_dr, n_dc = rpb.shape
    assert n_dr == 2 * NA_ROWS - 1 and n_dc == 2 * NA_COLS - 1
    rpb_pad = jnp.pad(rpb, ((0, 0), (0, 16 - n_dr), (0, LANES - n_dc)))
    return pl.pallas_call(
        _bias_kernel,
        grid=(n_heads,),
        in_specs=[pl.BlockSpec((1, 16, LANES), lambda h: (h, 0, 0))],
        out_specs=pl.BlockSpec((NA_ROWS, 1, GRID_W, WIN), lambda h: (0, h, 0, 0)),
        out_shape=jax.ShapeDtypeStruct((NA_ROWS, n_heads, GRID_W, WIN), F32),
        compiler_params=pltpu.CompilerParams(dimension_semantics=("arbitrary",),
                                             vmem_limit_bytes=VMEM_LIMIT_BYTES),
        name="bias_tables",
    )(rpb_pad)


def _union_row_start(i, n_rows):
    return jnp.minimum(_window_row_start(i * QUERY_ROWS, n_rows), n_rows - UNION_ROWS)


def _mix_kernel(seq, x_ref, mod_ref, u_ref, up_ref, un_ref, zp_ref, za_ref, qr_ref, q_ref, k_hbm, v_hbm, *rest):
    bias_refs = rest[:QUERY_ROWS]
    (kc_ref, vc_ref, wp_ref, ps_ref, wo_ref, bo_ref, g_ref, be_ref,
     o_ref, cat_ref, k_buf, v_buf, win_sem) = rest[QUERY_ROWS:]
    d = x_ref.shape[1]
    d_pool = ps_ref.shape[1]
    gdim = d_pool // len(POOL_WINDOWS)
    n_heads = (cat_ref.shape[1] - d_pool) // HEAD_DIM
    n_rows = seq // GRID_W
    i = pl.program_id(1)
    n_blk = pl.num_programs(1)

    step = pl.program_id(0) * n_blk + i
    n_steps = pl.num_programs(0) * n_blk
    slot = step % 2

    def window_copies(s, into):
        tok0 = pl.multiple_of((s // n_blk) * seq + _union_row_start(s % n_blk, n_rows) * GRID_W, GRID_W)
        return [pltpu.make_async_copy(hbm.at[pl.ds(tok0, UNION_ROWS * GRID_W)], buf.at[into], win_sem.at[n, into])
                for n, (hbm, buf) in enumerate(((k_hbm, k_buf), (v_hbm, v_buf)))]

    @pl.when(step == 0)
    def _():
        for cp in window_copies(step, slot):
            cp.start()

    for cp in window_copies(step, slot):
        cp.wait()

    @pl.when(step + 1 < n_steps)
    def _():
        for cp in window_copies(step + 1, 1 - slot):
            cp.start()

    win_off = [pl.multiple_of((_window_row_start(i * QUERY_ROWS + a, n_rows) - _union_row_start(i, n_rows)) * GRID_W,
                              GRID_W) for a in range(QUERY_ROWS)]

    u = u_ref[...]
    prev = jnp.where(i > 0, up_ref[...], 0.0)
    nxt = jnp.where(i < n_blk - 1, un_ref[...], 0.0)
    ext = jnp.concatenate([prev, u, nxt], axis=0)
    n_ext = ext.shape[0]
    t = i * QB + lax.broadcasted_iota(jnp.int32, (QB, 1), 0)

    def shift_sum(s, k):
        return pltpu.roll(s, k, axis=0) + pltpu.roll(s, n_ext - k, axis=0)

    def pool_group(g):
        w = POOL_WINDOWS[g]
        cols = slice(g * gdim, (g + 1) * gdim)
        s = ext[:, cols]
        s = s + pltpu.roll(s, 1, axis=0)
        k = 1
        while 2 * k < w:
            s = shift_sum(s, k)
            k *= 2
        s = s[POOL_HALO:POOL_HALO + QB]
        lo = jnp.clip(t - w // 2, 0, seq)
        hi = jnp.clip(t - w // 2 + w, 0, seq)
        pooled = s / (hi - lo).astype(F32) - u[:, cols]
        y = jnp.dot(pooled.astype(BF16), wp_ref[g], preferred_element_type=F32)
        y = y * ps_ref[:, cols] * _silu(zp_ref[:, cols].astype(F32))
        cat_ref[:, cols] = y.astype(BF16)

    nt_dims = (((1,), (1,)), ((), ()))
    lane = lax.broadcasted_iota(jnp.int32, (1, LANES), 1)
    first_head = lane < HEAD_DIM
    ones_win = jnp.ones((WIN, LANES), BF16)
    ones_ctx = jnp.ones((kc_ref.shape[0], LANES), BF16)
    stacked = 2 * GRID_W

    def chunk(p):
        return slice(p * LANES, (p + 1) * LANES)

    def stack_heads(ref, p):
        qp = ref[:, chunk(p)]
        zero = jnp.zeros_like(qp)
        first, second = jnp.where(first_head, qp, zero), jnp.where(first_head, zero, qp)
        pieces = []
        for a in range(QUERY_ROWS):
            pieces += [first[a * GRID_W:(a + 1) * GRID_W], second[a * GRID_W:(a + 1) * GRID_W]]
        return jnp.concatenate(pieces, axis=0)

    def lane_tiles(a):
        return [a[:, c * LANES:(c + 1) * LANES] for c in range(a.shape[1] // LANES)]

    def scores(p):
        q_rot = stack_heads(qr_ref, p)
        s_loc = []
        for a in range(QUERY_ROWS):
            k_win = k_buf[slot, pl.ds(win_off[a], WIN), chunk(p)]
            bias = bias_refs[a][0, 2 * p:2 * p + 2].reshape(stacked, WIN)
            s_loc.append(lax.dot_general(q_rot[a * stacked:(a + 1) * stacked], k_win, nt_dims,
                                         preferred_element_type=F32) + bias)
        s_ctx = lax.dot_general(stack_heads(q_ref, p), kc_ref[:, chunk(p)], nt_dims,
                                preferred_element_type=F32)
        return jnp.concatenate(s_loc, axis=0), s_ctx

    def softmax_numerators(s_loc, s_ctx):
        m = jnp.max(functools.reduce(jnp.maximum, lane_tiles(s_loc) + lane_tiles(s_ctx)), axis=1, keepdims=True)
        return jnp.concatenate([jnp.exp2(s_loc - m), jnp.exp2(s_ctx - m)], axis=1).astype(BF16)

    def weighted_values(p, probs):
        vc_aug = jnp.concatenate([vc_ref[:, chunk(p)], ones_ctx], axis=1)
        outs = []
        for a in range(QUERY_ROWS):
            v_win = v_buf[slot, pl.ds(win_off[a], WIN), chunk(p)]
            v_aug = jnp.concatenate([jnp.concatenate([v_win, ones_win], axis=1), vc_aug], axis=0)
            outs.append(jnp.dot(probs[a * stacked:(a + 1) * stacked], v_aug, preferred_element_type=F32))
        return outs

    def emit(p, outs):
        num = jnp.concatenate([jnp.where(first_head, o[:GRID_W, :LANES], o[GRID_W:, :LANES]) for o in outs], axis=0)
        den = jnp.concatenate([jnp.where(first_head, o[:GRID_W, LANES:], o[GRID_W:, LANES:]) for o in outs], axis=0)
        out = num / den * _silu(za_ref[:, chunk(p)].astype(F32))
        cat_ref[:, d_pool + p * LANES:d_pool + (p + 1) * LANES] = out.astype(BF16)

    def project_out(lo, hi):
        return jnp.dot(cat_ref[:, lo:hi], wo_ref[lo:hi, :], preferred_element_type=F32)

    n_pairs = n_heads // 2
    n_pool = len(POOL_WINDOWS)
    oproj_k = 2 * LANES
    y = bo_ref[...]
    s_next = scores(0)
    for p in range(n_pairs):
        s_cur = s_next
        if p + 1 < n_pairs:
            s_next = scores(p + 1)
        if p < n_pool:
            pool_group(p)
        emit(p, weighted_values(p, softmax_numerators(*s_cur)))
    y = y + project_out(0, cat_ref.shape[1])

    gate = mod_ref[0][:, 2 * d:3 * d]
    z = DEEPNORM_ALPHA * x_ref[...] + gate * y
    o_ref[...] = _layer_norm(z) * g_ref[...] + be_ref[...]


def _mix(x2d, mod3, u, zp, za, qr, q, kr, v, kc, vc, bias, w_pool, pool_scale, w_out, b_out, ln_g, ln_b,
         batch, seq, ctx_len):
    t, d = x2d.shape
    n_blk = seq // QB
    halo_per_blk = QB // POOL_HALO
    n_halo = t // POOL_HALO
    d_mix = w_out.shape[0]

    n_rows = seq // GRID_W

    def tok(b, i):
        return (b * n_blk + i, 0)

    def bias_spec(a):
        def index(b, i):
            r = i * QUERY_ROWS + a
            return (_window_row_start(r, n_rows) - r + NA_ROWS - 1, 0, 0, 0)
        return pl.BlockSpec((1,) + bias.shape[1:], index)

    full = lambda shape: pl.BlockSpec(shape, lambda b, i: (0,) * len(shape))
    in_specs = [
        pl.BlockSpec((QB, d), tok),
        pl.BlockSpec((1, 1, mod3.shape[2]), lambda b, i: (b, 0, 0)),
        pl.BlockSpec((QB, d), tok),
        pl.BlockSpec((POOL_HALO, d),
                     lambda b, i: (jnp.maximum((b * n_blk + i) * halo_per_blk - 1, 0), 0)),
        pl.BlockSpec((POOL_HALO, d),
                     lambda b, i: (jnp.minimum((b * n_blk + i + 1) * halo_per_blk, n_halo - 1), 0)),
        pl.BlockSpec((QB, d), tok),
        pl.BlockSpec((QB, d), tok),
        pl.BlockSpec((QB, d), tok),
        pl.BlockSpec((QB, d), tok),
    ]
    in_specs += [pl.BlockSpec(memory_space=pl.ANY)] * 2
    in_specs += [bias_spec(a) for a in range(QUERY_ROWS)]
    in_specs += [
        pl.BlockSpec((ctx_len, d), lambda b, i: (b, 0)),
        pl.BlockSpec((ctx_len, d), lambda b, i: (b, 0)),
        full(w_pool.shape), full(pool_scale.shape), full(w_out.shape),
        full(b_out.shape), full(ln_g.shape), full(ln_b.shape),
    ]
    args = ([x2d, mod3, u, u, u, zp, za, qr, q, kr, v]
            + [bias] * QUERY_ROWS + [kc, vc, w_pool, pool_scale, w_out, b_out, ln_g, ln_b])
    return pl.pallas_call(
        functools.partial(_mix_kernel, seq),
        grid=(batch, n_blk),
        in_specs=in_specs,
        out_specs=pl.BlockSpec((QB, d), tok),
        out_shape=jax.ShapeDtypeStruct((t, d), F32),
        scratch_shapes=[pltpu.VMEM((QB, d_mix), BF16),
                        pltpu.VMEM((2, UNION_ROWS * GRID_W, d), BF16),
                        pltpu.VMEM((2, UNION_ROWS * GRID_W, d), BF16),
                        pltpu.SemaphoreType.DMA((2, 2))],
        compiler_params=pltpu.CompilerParams(dimension_semantics=("arbitrary", "arbitrary"),
                                             vmem_limit_bytes=VMEM_LIMIT_BYTES),
        name="mix",
    )(*args)


def kernel(x, c, ctx, c_ctx, w_ada, b_ada, w_in, b_in, w_pool, pool_scale, rpb, w_out, b_out, ln_g, ln_b):
    batch, seq, d = x.shape
    ctx_len = ctx.shape[1]
    assert w_ada.shape[0] == DEPTH == 1
    assert seq % GRID_W == 0 and seq % INPROJ_TM == 0 and (batch * ctx_len) % INPROJ_TM == 0
    assert seq // GRID_W >= NA_ROWS and seq % QB == 0
    d_pool = pool_scale.shape[1]
    assert d_pool == d and w_in.shape[2] == 6 * d and rpb.shape[1] * HEAD_DIM == d

    cvec = jnp.zeros((8, d), F32).at[:batch].set(c).at[batch].set(c_ctx)
    mod3 = _ada(cvec, w_ada[0], b_ada[0][None, :]).reshape(8, 1, 3 * d)

    w_in_b = w_in[0].astype(BF16)
    b_in2 = b_in[0][None, :]
    x2d = x.reshape(batch * seq, d)
    tiles_per_seq = seq // INPROJ_TM
    u, zp, qr, q, kr, v, za = _inproj(
        x2d, mod3, lambda i: i // tiles_per_seq, w_in_b, b_in2, range(6),
        ("plain", "plain", "q", "k", "plain", "plain"), (F32, BF16, BF16, BF16, BF16, BF16, BF16),
        _rope_tables(seq), INPROJ_TM)
    kc, vc = _inproj(
        ctx.reshape(batch * ctx_len, d), mod3, lambda i: batch, w_in_b, b_in2, (3, 4),
        ("plain", "plain"), (BF16, BF16), None, INPROJ_TM)

    bias = _bias_tables(rpb[0])
    out = _mix(x2d, mod3, u, zp, za, qr, q, kr, v, kc, vc, bias,
               w_pool[0].astype(BF16), pool_scale, w_out[0].astype(BF16), b_out, ln_g, ln_b,
               batch, seq, ctx_len)
    return out.reshape(batch, seq, d)
```

```python
import functools

import numpy as np
import jax
import jax.numpy as jnp
from jax import lax
from jax.experimental import pallas as pl
from jax.experimental.pallas import tpu as pltpu

GRID_W = 64
POOL_WINDOWS = (2, 4, 8, 16)
HEAD_DIM = 64
NA_ROWS = 8
NA_COLS = 16
ROPE_BASE = 10000.0
LN_EPS = 1e-6
LOG2_E = float(np.log2(np.e))
SCORE_SCALE = HEAD_DIM ** -0.5 * LOG2_E
DEPTH = 1
DEEPNORM_ALPHA = (2.0 * DEPTH) ** 0.25

LANES = 128
VMEM_LIMIT_BYTES = 52 * 1024 * 1024

QUERY_ROWS = 2
QB = QUERY_ROWS * GRID_W
WIN = NA_ROWS * GRID_W
STEP_BLOCKS = 2
STEP_ROWS = STEP_BLOCKS * QUERY_ROWS
ST = STEP_BLOCKS * QB
UNION_ROWS = NA_ROWS + STEP_ROWS - 1
POOL_HALO = 8
NEG = -1e30
INPROJ_TM = 512
F32 = jnp.float32
BF16 = jnp.bfloat16


def _silu(x):
    return x * jax.nn.sigmoid(x)


def _layer_norm(x):
    mu = jnp.mean(x, axis=-1, keepdims=True)
    xc = x - mu
    var = jnp.mean(xc * xc, axis=-1, keepdims=True)
    return xc * lax.rsqrt(var + LN_EPS)


def _ada_kernel(c_ref, w_ref, b_ref, o_ref):
    s = _silu(c_ref[...]).astype(BF16)
    o_ref[...] = jnp.dot(s, w_ref[...].astype(BF16), preferred_element_type=F32) + b_ref[...]


def _ada(cvec, w_ada, b_ada):
    rows, d = cvec.shape
    n = w_ada.shape[1]
    tn = 768
    return pl.pallas_call(
        _ada_kernel,
        grid=(n // tn,),
        in_specs=[pl.BlockSpec((rows, d), lambda j: (0, 0)),
                  pl.BlockSpec((d, tn), lambda j: (0, j)),
                  pl.BlockSpec((1, tn), lambda j: (0, j))],
        out_specs=pl.BlockSpec((rows, tn), lambda j: (0, j)),
        out_shape=jax.ShapeDtypeStruct((rows, n), F32),
        compiler_params=pltpu.CompilerParams(dimension_semantics=("arbitrary",),
                                             vmem_limit_bytes=VMEM_LIMIT_BYTES),
        name="ada",
    )(cvec, w_ada, b_ada)


def _rope(x, cos, sin):
    half = HEAD_DIM // 4
    first_half = lax.broadcasted_iota(jnp.int32, (1, LANES), 1) % (2 * half) < half
    outs = []
    for c in range(x.shape[1] // LANES):
        xc = x[:, c * LANES:(c + 1) * LANES]
        partner = jnp.where(first_half, pltpu.roll(xc, LANES - half, axis=1), pltpu.roll(xc, half, axis=1))
        outs.append(xc * cos + partner * sin)
    return jnp.concatenate(outs, axis=1)


def _inproj_kernel(kinds, has_rope, *refs):
    n_blk = len(kinds)
    x_ref, mod_ref = refs[:2]
    w_refs = refs[2:2 + n_blk]
    b_refs = refs[2 + n_blk:2 + 2 * n_blk]
    n_in = 2 + 2 * n_blk + (2 if has_rope else 0)
    cos_ref, sin_ref = refs[n_in - 2:n_in] if has_rope else (None, None)
    out_refs = refs[n_in:]
    d = x_ref.shape[1]

    mod = mod_ref[0]
    h = (_layer_norm(x_ref[...]) * (1.0 + mod[:, d:2 * d]) + mod[:, :d]).astype(BF16)

    o = 0
    for kind, w_ref, b_ref in zip(kinds, w_refs, b_refs):
        acc = jnp.dot(h, w_ref[...], preferred_element_type=F32) + b_ref[...]
        if kind == "q":
            qr_ref, q_ref = out_refs[o:o + 2]
            o += 2
            qs = acc * SCORE_SCALE
            q_ref[...] = qs.astype(q_ref.dtype)
            qr_ref[...] = _rope(qs, cos_ref[...], sin_ref[...]).astype(qr_ref.dtype)
        else:
            val = _rope(acc, cos_ref[...], sin_ref[...]) if kind == "k" else acc
            out_refs[o][...] = val.astype(out_refs[o].dtype)
            o += 1


def _inproj(x2d, mod3, mod_rows_per_tile, w_bf16, b_in, col_blocks, kinds, out_dtypes, rope, tm):
    t, d = x2d.shape
    has_rope = rope is not None
    resident = pl.Buffered(1)
    in_specs = [
        pl.BlockSpec((tm, d), lambda i: (i, 0)),
        pl.BlockSpec((1, 1, mod3.shape[2]), lambda i: (mod_rows_per_tile(i), 0, 0)),
    ]
    in_specs += [pl.BlockSpec((d, d), lambda i, c=c: (0, c), pipeline_mode=resident) for c in col_blocks]
    in_specs += [pl.BlockSpec((1, d), lambda i, c=c: (0, c), pipeline_mode=resident) for c in col_blocks]
    args = [x2d, mod3] + [w_bf16] * len(col_blocks) + [b_in] * len(col_blocks)
    if has_rope:
        tiles_per_seq = rope[0].shape[0] // tm
        for tab in rope:
            in_specs.append(pl.BlockSpec((tm, LANES), lambda i: (i % tiles_per_seq, 0)))
            args.append(tab)
    out_specs = [pl.BlockSpec((tm, d), lambda i: (i, 0)) for _ in out_dtypes]
    out_shape = [jax.ShapeDtypeStruct((t, d), dt) for dt in out_dtypes]
    return pl.pallas_call(
        functools.partial(_inproj_kernel, tuple(kinds), has_rope),
        grid=(t // tm,),
        in_specs=in_specs,
        out_specs=out_specs,
        out_shape=out_shape,
        compiler_params=pltpu.CompilerParams(dimension_semantics=("arbitrary",),
                                             vmem_limit_bytes=VMEM_LIMIT_BYTES),
        name="inproj_rope" if has_rope else "inproj_ctx",
    )(*args)


def _rope_tables(seq):
    half = HEAD_DIM // 4
    inv_freq = ROPE_BASE ** (-np.arange(half, dtype=np.float64) / half)
    t = np.arange(seq)
    row, col = t // GRID_W, t % GRID_W
    lane = np.arange(LANES)
    pos = np.where((lane % HEAD_DIM < HEAD_DIM // 2)[None, :], row[:, None], col[:, None]).astype(np.float64)
    ang = pos * inv_freq[lane % half][None, :]
    sign = np.where(lane % (2 * half) < half, -1.0, 1.0)[None, :]
    return jnp.asarray(np.cos(ang), F32), jnp.asarray(np.sin(ang) * sign, F32)


def _window_row_start(r, n_rows):
    return jnp.clip(r - NA_ROWS // 2, 0, n_rows - NA_ROWS)


def _bias_kernel(rpb_ref, o_ref):
    table = rpb_ref[0] * LOG2_E
    lane = lax.broadcasted_iota(jnp.int32, (GRID_W, LANES), 1)
    qc = lax.broadcasted_iota(jnp.int32, (GRID_W, LANES), 0)
    kc = lane % GRID_W
    cs = jnp.clip(qc - NA_COLS // 2, 0, GRID_W - NA_COLS)
    col_ok = (kc >= cs) & (kc < cs + NA_COLS)
    low_half = lane < GRID_W

    @functools.cache
    def shifted(dr, half):
        row = jnp.broadcast_to(table[dr:dr + 1, :], (GRID_W, LANES))
        shift = (LANES - (NA_COLS - 1) + GRID_W * half) % LANES
        return pltpu.roll(row, shift, axis=1, stride=1, stride_axis=0)

    for v in range(NA_ROWS):
        pairs = [jnp.where(col_ok, jnp.where(low_half, shifted(v + jr, 0), shifted(v + jr + 1, 1)), NEG)
                 for jr in range(0, NA_ROWS, 2)]
        o_ref[v, 0] = jnp.concatenate(pairs, axis=1)


def _bias_tables(rpb):
    n_heads, n_dr, n_dc = rpb.shape
    assert n_dr == 2 * NA_ROWS - 1 and n_dc == 2 * NA_COLS - 1
    rpb_pad = jnp.pad(rpb, ((0, 0), (0, 16 - n_dr), (0, LANES - n_dc)))
    return pl.pallas_call(
        _bias_kernel,
        grid=(n_heads,),
        in_specs=[pl.BlockSpec((1, 16, LANES), lambda h: (h, 0, 0))],
        out_specs=pl.BlockSpec((NA_ROWS, 1, GRID_W, WIN), lambda h: (0, h, 0, 0)),
        out_shape=jax.ShapeDtypeStruct((NA_ROWS, n_heads, GRID_W, WIN), F32),
        compiler_params=pltpu.CompilerParams(dimension_semantics=("arbitrary",),
                                             vmem_limit_bytes=VMEM_LIMIT_BYTES),
        name="bias_tables",
    )(rpb_pad)


def _union_row_start(i, n_rows):
    return jnp.minimum(_window_row_start(i * STEP_ROWS, n_rows), n_rows - UNION_ROWS)


def _mix_kernel(seq, x_ref, mod_ref, u_ref, up_ref, un_ref, zp_ref, za_ref, qr_ref, q_ref, k_hbm, v_hbm, *rest):
    bias_refs = rest[:STEP_ROWS]
    (kc_ref, vc_ref, wp_ref, ps_ref, wo_ref, bo_ref, g_ref, be_ref,
     o_ref, cat_ref, k_buf, v_buf, win_sem) = rest[STEP_ROWS:]
    d = x_ref.shape[1]
    d_pool = ps_ref.shape[1]
    gdim = d_pool // len(POOL_WINDOWS)
    n_heads = (cat_ref.shape[1] - d_pool) // HEAD_DIM
    n_rows = seq // GRID_W
    i = pl.program_id(1)
    n_blk = pl.num_programs(1)

    step = pl.program_id(0) * n_blk + i
    n_steps = pl.num_programs(0) * n_blk
    slot = step % 2

    def window_copies(s, into):
        tok0 = pl.multiple_of((s // n_blk) * seq + _union_row_start(s % n_blk, n_rows) * GRID_W, GRID_W)
        return [pltpu.make_async_copy(hbm.at[pl.ds(tok0, UNION_ROWS * GRID_W)], buf.at[into], win_sem.at[n, into])
                for n, (hbm, buf) in enumerate(((k_hbm, k_buf), (v_hbm, v_buf)))]

    @pl.when(step == 0)
    def _():
        for cp in window_copies(step, slot):
            cp.start()

    for cp in window_copies(step, slot):
        cp.wait()

    @pl.when(step + 1 < n_steps)
    def _():
        for cp in window_copies(step + 1, 1 - slot):
            cp.start()

    win_off = [pl.multiple_of((_window_row_start(i * STEP_ROWS + r, n_rows) - _union_row_start(i, n_rows)) * GRID_W,
                              GRID_W) for r in range(STEP_ROWS)]

    u = u_ref[...]
    prev = jnp.where(i > 0, up_ref[...], 0.0)
    nxt = jnp.where(i < n_blk - 1, un_ref[...], 0.0)
    ext = jnp.concatenate([prev, u, nxt], axis=0)
    n_ext = ext.shape[0]
    t = i * ST + lax.broadcasted_iota(jnp.int32, (ST, 1), 0)

    def shift_sum(s, k):
        return pltpu.roll(s, k, axis=0) + pltpu.roll(s, n_ext - k, axis=0)

    def pool_group(g):
        w = POOL_WINDOWS[g]
        cols = slice(g * gdim, (g + 1) * gdim)
        s = ext[:, cols]
        s = s + pltpu.roll(s, 1, axis=0)
        k = 1
        while 2 * k < w:
            s = shift_sum(s, k)
            k *= 2
        s = s[POOL_HALO:POOL_HALO + ST]
        lo = jnp.clip(t - w // 2, 0, seq)
        hi = jnp.clip(t - w // 2 + w, 0, seq)
        pooled = s / (hi - lo).astype(F32) - u[:, cols]
        y = jnp.dot(pooled.astype(BF16), wp_ref[g], preferred_element_type=F32)
        y = y * ps_ref[:, cols] * _silu(zp_ref[:, cols].astype(F32))
        cat_ref[:, cols] = y.astype(BF16)

    nt_dims = (((1,), (1,)), ((), ()))
    lane = lax.broadcasted_iota(jnp.int32, (1, LANES), 1)
    first_head = lane < HEAD_DIM
    ones_win = jnp.ones((WIN, LANES), BF16)
    ones_ctx = jnp.ones((kc_ref.shape[0], LANES), BF16)
    stacked = 2 * GRID_W

    def chunk(p):
        return slice(p * LANES, (p + 1) * LANES)

    def block_rows(blk):
        return slice(blk * QB, (blk + 1) * QB)

    def stack_heads(ref, blk, p):
        qp = ref[block_rows(blk), chunk(p)]
        zero = jnp.zeros_like(qp)
        first, second = jnp.where(first_head, qp, zero), jnp.where(first_head, zero, qp)
        pieces = []
        for a in range(QUERY_ROWS):
            pieces += [first[a * GRID_W:(a + 1) * GRID_W], second[a * GRID_W:(a + 1) * GRID_W]]
        return jnp.concatenate(pieces, axis=0)

    def lane_tiles(a):
        return [a[:, c * LANES:(c + 1) * LANES] for c in range(a.shape[1] // LANES)]

    def scores(blk, p):
        q_rot = stack_heads(qr_ref, blk, p)
        s_loc = []
        for a in range(QUERY_ROWS):
            r = blk * QUERY_ROWS + a
            k_win = k_buf[slot, pl.ds(win_off[r], WIN), chunk(p)]
            bias = bias_refs[r][0, 2 * p:2 * p + 2].reshape(stacked, WIN)
            s_loc.append(lax.dot_general(q_rot[a * stacked:(a + 1) * stacked], k_win, nt_dims,
                                         preferred_element_type=F32) + bias)
        s_ctx = lax.dot_general(stack_heads(q_ref, blk, p), kc_ref[:, chunk(p)], nt_dims,
                                preferred_element_type=F32)
        return jnp.concatenate(s_loc, axis=0), s_ctx

    def softmax_numerators(s_loc, s_ctx):
        m = jnp.max(functools.reduce(jnp.maximum, lane_tiles(s_loc) + lane_tiles(s_ctx)), axis=1, keepdims=True)
        return jnp.concatenate([jnp.exp2(s_loc - m), jnp.exp2(s_ctx - m)], axis=1).astype(BF16)

    def weighted_values(blk, p, probs):
        vc_aug = jnp.concatenate([vc_ref[:, chunk(p)], ones_ctx], axis=1)
        outs = []
        for a in range(QUERY_ROWS):
            v_win = v_buf[slot, pl.ds(win_off[blk * QUERY_ROWS + a], WIN), chunk(p)]
            v_aug = jnp.concatenate([jnp.concatenate([v_win, ones_win], axis=1), vc_aug], axis=0)
            outs.append(jnp.dot(probs[a * stacked:(a + 1) * stacked], v_aug, preferred_element_type=F32))
        return outs

    def emit(blk, p, outs):
        num = jnp.concatenate([jnp.where(first_head, o[:GRID_W, :LANES], o[GRID_W:, :LANES]) for o in outs], axis=0)
        den = jnp.concatenate([jnp.where(first_head, o[:GRID_W, LANES:], o[GRID_W:, LANES:]) for o in outs], axis=0)
        out = num / den * _silu(za_ref[block_rows(blk), chunk(p)].astype(F32))
        cat_ref[block_rows(blk), d_pool + p * LANES:d_pool + (p + 1) * LANES] = out.astype(BF16)

    oproj_n = 2 * LANES
    n_oproj = d // oproj_n

    def project_out(blk, c):
        cols = slice(c * oproj_n, (c + 1) * oproj_n)
        return jnp.dot(cat_ref[block_rows(blk), :], wo_ref[:, cols], preferred_element_type=F32) + bo_ref[:, cols]

    def finish(blk, y_chunks):
        rows = block_rows(blk)
        y = jnp.concatenate(y_chunks, axis=1)
        z = DEEPNORM_ALPHA * x_ref[rows, :] + mod_ref[0][:, 2 * d:3 * d] * y
        o_ref[rows, :] = _layer_norm(z) * g_ref[...] + be_ref[...]

    n_pairs = n_heads // 2
    assert n_oproj < n_pairs
    items = [(blk, p) for blk in range(STEP_BLOCKS) for p in range(n_pairs)]
    s_next = scores(*items[0])
    y_chunks = []
    for n, (blk, p) in enumerate(items):
        s_cur = s_next
        if n + 1 < len(items):
            s_next = scores(*items[n + 1])
        if n < len(POOL_WINDOWS):
            pool_group(n)
        emit(blk, p, weighted_values(blk, p, softmax_numerators(*s_cur)))
        if blk > 0 and p < n_oproj:
            y_chunks.append(project_out(blk - 1, p))
        if blk > 0 and p == n_oproj:
            finish(blk - 1, y_chunks)
            y_chunks = []
    finish(STEP_BLOCKS - 1, [project_out(STEP_BLOCKS - 1, c) for c in range(n_oproj)])


def _mix(x2d, mod3, u, zp, za, qr, q, kr, v, kc, vc, bias, w_pool, pool_scale, w_out, b_out, ln_g, ln_b,
         batch, seq, ctx_len):
    t, d = x2d.shape
    n_blk = seq // ST
    halo_per_blk = ST // POOL_HALO
    n_halo = t // POOL_HALO
    d_mix = w_out.shape[0]
    n_rows = seq // GRID_W
    single = pl.Buffered(1)

    def tok(b, i):
        return (b * n_blk + i, 0)

    def bias_spec(row):
        def index(b, i):
            r = i * STEP_ROWS + row
            return (_window_row_start(r, n_rows) - r + NA_ROWS - 1, 0, 0, 0)
        return pl.BlockSpec((1,) + bias.shape[1:], index, pipeline_mode=single)

    full = lambda shape: pl.BlockSpec(shape, lambda b, i: (0,) * len(shape), pipeline_mode=single)
    in_specs = [
        pl.BlockSpec((ST, d), tok),
        pl.BlockSpec((1, 1, mod3.shape[2]), lambda b, i: (b, 0, 0)),
        pl.BlockSpec((ST, d), tok),
        pl.BlockSpec((POOL_HALO, d),
                     lambda b, i: (jnp.maximum((b * n_blk + i) * halo_per_blk - 1, 0), 0)),
        pl.BlockSpec((POOL_HALO, d),
                     lambda b, i: (jnp.minimum((b * n_blk + i + 1) * halo_per_blk, n_halo - 1), 0)),
        pl.BlockSpec((ST, d), tok),
        pl.BlockSpec((ST, d), tok),
        pl.BlockSpec((ST, d), tok),
        pl.BlockSpec((ST, d), tok),
    ]
    in_specs += [pl.BlockSpec(memory_space=pl.ANY)] * 2
    in_specs += [bias_spec(row) for row in range(STEP_ROWS)]
    in_specs += [
        pl.BlockSpec((ctx_len, d), lambda b, i: (b, 0)),
        pl.BlockSpec((ctx_len, d), lambda b, i: (b, 0)),
        full(w_pool.shape), full(pool_scale.shape), full(w_out.shape),
        full(b_out.shape), full(ln_g.shape), full(ln_b.shape),
    ]
    args = ([x2d, mod3, u, u, u, zp, za, qr, q, kr, v]
            + [bias] * STEP_ROWS + [kc, vc, w_pool, pool_scale, w_out, b_out, ln_g, ln_b])
    return pl.pallas_call(
        functools.partial(_mix_kernel, seq),
        grid=(batch, n_blk),
        in_specs=in_specs,
        out_specs=pl.BlockSpec((ST, d), tok),
        out_shape=jax.ShapeDtypeStruct((t, d), F32),
        scratch_shapes=[pltpu.VMEM((ST, d_mix), BF16),
                        pltpu.VMEM((2, UNION_ROWS * GRID_W, d), BF16),
                        pltpu.VMEM((2, UNION_ROWS * GRID_W, d), BF16),
                        pltpu.SemaphoreType.DMA((2, 2))],
        compiler_params=pltpu.CompilerParams(dimension_semantics=("arbitrary", "arbitrary"),
                                             vmem_limit_bytes=VMEM_LIMIT_BYTES),
        name="mix",
    )(*args)


def kernel(x, c, ctx, c_ctx, w_ada, b_ada, w_in, b_in, w_pool, pool_scale, rpb, w_out, b_out, ln_g, ln_b):
    batch, seq, d = x.shape
    ctx_len = ctx.shape[1]
    assert w_ada.shape[0] == DEPTH == 1
    assert seq % GRID_W == 0 and seq % INPROJ_TM == 0 and (batch * ctx_len) % INPROJ_TM == 0
    assert seq // GRID_W >= UNION_ROWS and seq % ST == 0
    d_pool = pool_scale.shape[1]
    assert d_pool == d and w_in.shape[2] == 6 * d and rpb.shape[1] * HEAD_DIM == d

    cvec = jnp.zeros((8, d), F32).at[:batch].set(c).at[batch].set(c_ctx)
    mod3 = _ada(cvec, w_ada[0], b_ada[0][None, :]).reshape(8, 1, 3 * d)

    w_in_b = w_in[0].astype(BF16)
    b_in2 = b_in[0][None, :]
    x2d = x.reshape(batch * seq, d)
    tiles_per_seq = seq // INPROJ_TM
    u, zp, qr, q, kr, v, za = _inproj(
        x2d, mod3, lambda i: i // tiles_per_seq, w_in_b, b_in2, range(6),
        ("plain", "plain", "q", "k", "plain", "plain"), (F32, BF16, BF16, BF16, BF16, BF16, BF16),
        _rope_tables(seq), INPROJ_TM)
    kc, vc = _inproj(
        ctx.reshape(batch * ctx_len, d), mod3, lambda i: batch, w_in_b, b_in2, (3, 4),
        ("plain", "plain"), (BF16, BF16), None, INPROJ_TM)

    bias = _bias_tables(rpb[0])
    out = _mix(x2d, mod3, u, zp, za, qr, q, kr, v, kc, vc, bias,
               w_pool[0].astype(BF16), pool_scale, w_out[0].astype(BF16), b_out, ln_g, ln_b,
               batch, seq, ctx_len)
    return out.reshape(batch, seq, d)
```

```python
import functools

import numpy as np
import jax
import jax.numpy as jnp
from jax import lax
from jax.experimental import pallas as pl
from jax.experimental.pallas import tpu as pltpu

GRID_W = 64
POOL_WINDOWS = (2, 4, 8, 16)
HEAD_DIM = 64
NA_ROWS = 8
NA_COLS = 16
ROPE_BASE = 10000.0
LN_EPS = 1e-6
LOG2_E = float(np.log2(np.e))
SCORE_SCALE = HEAD_DIM ** -0.5 * LOG2_E
DEPTH = 1
DEEPNORM_ALPHA = (2.0 * DEPTH) ** 0.25

LANES = 128
VMEM_LIMIT_BYTES = 52 * 1024 * 1024

QUERY_ROWS = 2
QB = QUERY_ROWS * GRID_W
WIN = NA_ROWS * GRID_W
STEP_BLOCKS = 2
STEP_ROWS = STEP_BLOCKS * QUERY_ROWS
ST = STEP_BLOCKS * QB
UNION_ROWS = NA_ROWS + STEP_ROWS - 1
POOL_HALO = 8
NEG = -1e30
INPROJ_TM = 512
F32 = jnp.float32
BF16 = jnp.bfloat16


def _silu(x):
    return x * jax.nn.sigmoid(x)


def _layer_norm(x):
    mu = jnp.mean(x, axis=-1, keepdims=True)
    xc = x - mu
    var = jnp.mean(xc * xc, axis=-1, keepdims=True)
    return xc * lax.rsqrt(var + LN_EPS)


def _ada_kernel(c_ref, w_ref, b_ref, o_ref):
    s = _silu(c_ref[...]).astype(BF16)
    o_ref[...] = jnp.dot(s, w_ref[...].astype(BF16), preferred_element_type=F32) + b_ref[...]


def _ada(cvec, w_ada, b_ada):
    rows, d = cvec.shape
    n = w_ada.shape[1]
    tn = 768
    return pl.pallas_call(
        _ada_kernel,
        grid=(n // tn,),
        in_specs=[pl.BlockSpec((rows, d), lambda j: (0, 0)),
                  pl.BlockSpec((d, tn), lambda j: (0, j)),
                  pl.BlockSpec((1, tn), lambda j: (0, j))],
        out_specs=pl.BlockSpec((rows, tn), lambda j: (0, j)),
        out_shape=jax.ShapeDtypeStruct((rows, n), F32),
        compiler_params=pltpu.CompilerParams(dimension_semantics=("arbitrary",),
                                             vmem_limit_bytes=VMEM_LIMIT_BYTES),
        name="ada",
    )(cvec, w_ada, b_ada)


def _rope(x, cos, sin):
    half = HEAD_DIM // 4
    first_half = lax.broadcasted_iota(jnp.int32, (1, LANES), 1) % (2 * half) < half
    outs = []
    for c in range(x.shape[1] // LANES):
        xc = x[:, c * LANES:(c + 1) * LANES]
        partner = jnp.where(first_half, pltpu.roll(xc, LANES - half, axis=1), pltpu.roll(xc, half, axis=1))
        outs.append(xc * cos + partner * sin)
    return jnp.concatenate(outs, axis=1)


def _inproj_kernel(kinds, has_rope, *refs):
    n_blk = len(kinds)
    x_ref, mod_ref = refs[:2]
    w_refs = refs[2:2 + n_blk]
    b_refs = refs[2 + n_blk:2 + 2 * n_blk]
    n_in = 2 + 2 * n_blk + (2 if has_rope else 0)
    cos_ref, sin_ref = refs[n_in - 2:n_in] if has_rope else (None, None)
    out_refs = refs[n_in:]
    d = x_ref.shape[1]

    mod = mod_ref[0]
    tm = x_ref.shape[0]

    def modulated(rows):
        return (_layer_norm(x_ref[rows, :]) * (1.0 + mod[:, d:2 * d]) + mod[:, :d]).astype(BF16)

    h_halves = [modulated(slice(0, tm // 2)), modulated(slice(tm // 2, tm))]
    h = jnp.concatenate(h_halves, axis=0)

    o = 0
    for n, (kind, w_ref, b_ref) in enumerate(zip(kinds, w_refs, b_refs)):
        if n == 0:
            acc = jnp.concatenate([jnp.dot(hh, w_ref[...], preferred_element_type=F32) for hh in h_halves], axis=0)
        else:
            acc = jnp.dot(h, w_ref[...], preferred_element_type=F32)
        acc = acc + b_ref[...]
        if kind == "q":
            qr_ref, q_ref = out_refs[o:o + 2]
            o += 2
            qs = acc * SCORE_SCALE
            q_ref[...] = qs.astype(q_ref.dtype)
            qr_ref[...] = _rope(qs, cos_ref[...], sin_ref[...]).astype(qr_ref.dtype)
        else:
            val = _rope(acc, cos_ref[...], sin_ref[...]) if kind == "k" else acc
            out_refs[o][...] = val.astype(out_refs[o].dtype)
            o += 1


def _inproj(x2d, mod3, mod_rows_per_tile, w_bf16, b_in, col_blocks, kinds, out_dtypes, rope, tm):
    t, d = x2d.shape
    has_rope = rope is not None
    resident = pl.Buffered(1)
    in_specs = [
        pl.BlockSpec((tm, d), lambda i: (i, 0)),
        pl.BlockSpec((1, 1, mod3.shape[2]), lambda i: (mod_rows_per_tile(i), 0, 0)),
    ]
    in_specs += [pl.BlockSpec((d, d), lambda i, c=c: (0, c), pipeline_mode=resident) for c in col_blocks]
    in_specs += [pl.BlockSpec((1, d), lambda i, c=c: (0, c), pipeline_mode=resident) for c in col_blocks]
    args = [x2d, mod3] + [w_bf16] * len(col_blocks) + [b_in] * len(col_blocks)
    if has_rope:
        tiles_per_seq = rope[0].shape[0] // tm
        for tab in rope:
            in_specs.append(pl.BlockSpec((tm, LANES), lambda i: (i % tiles_per_seq, 0)))
            args.append(tab)
    out_specs = [pl.BlockSpec((tm, d), lambda i: (i, 0)) for _ in out_dtypes]
    out_shape = [jax.ShapeDtypeStruct((t, d), dt) for dt in out_dtypes]
    return pl.pallas_call(
        functools.partial(_inproj_kernel, tuple(kinds), has_rope),
        grid=(t // tm,),
        in_specs=in_specs,
        out_specs=out_specs,
        out_shape=out_shape,
        compiler_params=pltpu.CompilerParams(dimension_semantics=("arbitrary",),
                                             vmem_limit_bytes=VMEM_LIMIT_BYTES),
        name="inproj_rope" if has_rope else "inproj_ctx",
    )(*args)


def _rope_tables(seq):
    half = HEAD_DIM // 4
    inv_freq = ROPE_BASE ** (-np.arange(half, dtype=np.float64) / half)
    t = np.arange(seq)
    row, col = t // GRID_W, t % GRID_W
    lane = np.arange(LANES)
    pos = np.where((lane % HEAD_DIM < HEAD_DIM // 2)[None, :], row[:, None], col[:, None]).astype(np.float64)
    ang = pos * inv_freq[lane % half][None, :]
    sign = np.where(lane % (2 * half) < half, -1.0, 1.0)[None, :]
    return jnp.asarray(np.cos(ang), F32), jnp.asarray(np.sin(ang) * sign, F32)


def _window_row_start(r, n_rows):
    return jnp.clip(r - NA_ROWS // 2, 0, n_rows - NA_ROWS)


def _bias_kernel(rpb_ref, o_ref):
    table = rpb_ref[0] * LOG2_E
    lane = lax.broadcasted_iota(jnp.int32, (GRID_W, LANES), 1)
    qc = lax.broadcasted_iota(jnp.int32, (GRID_W, LANES), 0)
    kc = lane % GRID_W
    cs = jnp.clip(qc - NA_COLS // 2, 0, GRID_W - NA_COLS)
    col_ok = (kc >= cs) & (kc < cs + NA_COLS)
    low_half = lane < GRID_W

    @functools.cache
    def shifted(dr, half):
        row = jnp.broadcast_to(table[dr:dr + 1, :], (GRID_W, LANES))
        shift = (LANES - (NA_COLS - 1) + GRID_W * half) % LANES
        return pltpu.roll(row, shift, axis=1, stride=1, stride_axis=0)

    for i in range(o_ref.shape[1]):
        o_ref[0, i] = jnp.where(col_ok, jnp.where(low_half, shifted(i, 0), shifted(i + 1, 1)), NEG)


def _bias_tables(rpb):
    n_heads, n_dr, n_dc = rpb.shape
    assert n_dr == 2 * NA_ROWS - 1 and n_dc == 2 * NA_COLS - 1
    rpb_pad = jnp.pad(rpb, ((0, 0), (0, 16 - n_dr), (0, LANES - n_dc)))
    n_tiles = n_dr - 1
    return pl.pallas_call(
        _bias_kernel,
        grid=(n_heads,),
        in_specs=[pl.BlockSpec((1, 16, LANES), lambda h: (h, 0, 0))],
        out_specs=pl.BlockSpec((1, n_tiles, GRID_W, LANES), lambda h: (h, 0, 0, 0)),
        out_shape=jax.ShapeDtypeStruct((n_heads, n_tiles, GRID_W, LANES), F32),
        compiler_params=pltpu.CompilerParams(dimension_semantics=("arbitrary",),
                                             vmem_limit_bytes=VMEM_LIMIT_BYTES),
        name="bias_tables",
    )(rpb_pad)


def _union_row_start(i, n_rows):
    return jnp.minimum(_window_row_start(i * STEP_ROWS, n_rows), n_rows - UNION_ROWS)


def _mix_kernel(seq, x_ref, mod_ref, u_ref, up_ref, un_ref, zp_ref, za_ref, qr_ref, q_ref, k_hbm, v_hbm, *rest):
    (bias_ref, kc_ref, vc_ref, wp_ref, ps_ref, wo_ref, bo_ref, g_ref, be_ref,
     o_ref, cat_ref, k_buf, v_buf, win_sem) = rest
    d = x_ref.shape[1]
    d_pool = ps_ref.shape[1]
    gdim = d_pool // len(POOL_WINDOWS)
    n_heads = (cat_ref.shape[1] - d_pool) // HEAD_DIM
    n_rows = seq // GRID_W
    i = pl.program_id(1)
    n_blk = pl.num_programs(1)

    step = pl.program_id(0) * n_blk + i
    n_steps = pl.num_programs(0) * n_blk
    slot = step % 2

    def window_copies(s, into):
        tok0 = pl.multiple_of((s // n_blk) * seq + _union_row_start(s % n_blk, n_rows) * GRID_W, GRID_W)
        return [pltpu.make_async_copy(hbm.at[pl.ds(tok0, UNION_ROWS * GRID_W)], buf.at[into], win_sem.at[n, into])
                for n, (hbm, buf) in enumerate(((k_hbm, k_buf), (v_hbm, v_buf)))]

    @pl.when(step == 0)
    def _():
        for cp in window_copies(step, slot):
            cp.start()

    for cp in window_copies(step, slot):
        cp.wait()

    @pl.when(step + 1 < n_steps)
    def _():
        for cp in window_copies(step + 1, 1 - slot):
            cp.start()

    win_row = [_window_row_start(i * STEP_ROWS + r, n_rows) for r in range(STEP_ROWS)]
    win_off = [pl.multiple_of((win_row[r] - _union_row_start(i, n_rows)) * GRID_W, GRID_W) for r in range(STEP_ROWS)]
    win_dr = [win_row[r] - (i * STEP_ROWS + r) + NA_ROWS - 1 for r in range(STEP_ROWS)]

    u = u_ref[...]
    prev = jnp.where(i > 0, up_ref[...], 0.0)
    nxt = jnp.where(i < n_blk - 1, un_ref[...], 0.0)
    ext = jnp.concatenate([prev, u, nxt], axis=0)
    n_ext = ext.shape[0]
    t = i * ST + lax.broadcasted_iota(jnp.int32, (ST, 1), 0)

    def shift_sum(s, k):
        return pltpu.roll(s, k, axis=0) + pltpu.roll(s, n_ext - k, axis=0)

    def pool_group(g):
        w = POOL_WINDOWS[g]
        cols = slice(g * gdim, (g + 1) * gdim)
        s = ext[:, cols]
        s = s + pltpu.roll(s, 1, axis=0)
        k = 1
        while 2 * k < w:
            s = shift_sum(s, k)
            k *= 2
        s = s[POOL_HALO:POOL_HALO + ST]
        lo = jnp.clip(t - w // 2, 0, seq)
        hi = jnp.clip(t - w // 2 + w, 0, seq)
        pooled = s / (hi - lo).astype(F32) - u[:, cols]
        y = jnp.dot(pooled.astype(BF16), wp_ref[g], preferred_element_type=F32)
        y = y * ps_ref[:, cols] * _silu(zp_ref[:, cols].astype(F32))
        cat_ref[:, cols] = y.astype(BF16)

    nt_dims = (((1,), (1,)), ((), ()))
    lane = lax.broadcasted_iota(jnp.int32, (1, LANES), 1)
    first_head = lane < HEAD_DIM
    ones_win = jnp.ones((WIN, LANES), BF16)
    ones_ctx = jnp.ones((kc_ref.shape[0], LANES), BF16)
    stacked = 2 * GRID_W

    def chunk(p):
        return slice(p * LANES, (p + 1) * LANES)

    def block_rows(blk):
        return slice(blk * QB, (blk + 1) * QB)

    def stack_heads(ref, blk, p):
        qp = ref[block_rows(blk), chunk(p)]
        zero = jnp.zeros_like(qp)
        first, second = jnp.where(first_head, qp, zero), jnp.where(first_head, zero, qp)
        pieces = []
        for a in range(QUERY_ROWS):
            pieces += [first[a * GRID_W:(a + 1) * GRID_W], second[a * GRID_W:(a + 1) * GRID_W]]
        return jnp.concatenate(pieces, axis=0)

    def lane_tiles(a):
        return [a[:, c * LANES:(c + 1) * LANES] for c in range(a.shape[1] // LANES)]

    def scores(blk, p):
        q_rot = stack_heads(qr_ref, blk, p)
        s_loc = []
        for a in range(QUERY_ROWS):
            r = blk * QUERY_ROWS + a
            k_win = k_buf[slot, pl.ds(win_off[r], WIN), chunk(p)]
            bias = jnp.concatenate(
                [jnp.concatenate([bias_ref[2 * p + h, win_dr[r] + jr] for jr in range(0, NA_ROWS, 2)], axis=1)
                 for h in range(2)], axis=0)
            s_loc.append(lax.dot_general(q_rot[a * stacked:(a + 1) * stacked], k_win, nt_dims,
                                         preferred_element_type=F32) + bias)
        s_ctx = lax.dot_general(stack_heads(q_ref, blk, p), kc_ref[:, chunk(p)], nt_dims,
                                preferred_element_type=F32)
        return jnp.concatenate(s_loc, axis=0), s_ctx

    def softmax_numerators(s_loc, s_ctx):
        m = jnp.max(functools.reduce(jnp.maximum, lane_tiles(s_loc) + lane_tiles(s_ctx)), axis=1, keepdims=True)
        return jnp.concatenate([jnp.exp2(s_loc - m), jnp.exp2(s_ctx - m)], axis=1).astype(BF16)

    def weighted_values(blk, p, probs):
        vc_aug = jnp.concatenate([vc_ref[:, chunk(p)], ones_ctx], axis=1)
        outs = []
        for a in range(QUERY_ROWS):
            v_win = v_buf[slot, pl.ds(win_off[blk * QUERY_ROWS + a], WIN), chunk(p)]
            v_aug = jnp.concatenate([jnp.concatenate([v_win, ones_win], axis=1), vc_aug], axis=0)
            outs.append(jnp.dot(probs[a * stacked:(a + 1) * stacked], v_aug, preferred_element_type=F32))
        return outs

    def emit(blk, p, outs):
        num = jnp.concatenate([jnp.where(first_head, o[:GRID_W, :LANES], o[GRID_W:, :LANES]) for o in outs], axis=0)
        den = jnp.concatenate([jnp.where(first_head, o[:GRID_W, LANES:], o[GRID_W:, LANES:]) for o in outs], axis=0)
        out = num / den * _silu(za_ref[block_rows(blk), chunk(p)].astype(F32))
        cat_ref[block_rows(blk), d_pool + p * LANES:d_pool + (p + 1) * LANES] = out.astype(BF16)

    oproj_n = d
    n_oproj = d // oproj_n

    def project_out(blk, c):
        cols = slice(c * oproj_n, (c + 1) * oproj_n)
        return jnp.dot(cat_ref[block_rows(blk), :], wo_ref[:, cols], preferred_element_type=F32) + bo_ref[:, cols]

    def finish(blk, y_chunks):
        rows = block_rows(blk)
        y = jnp.concatenate(y_chunks, axis=1)
        z = DEEPNORM_ALPHA * x_ref[rows, :] + mod_ref[0][:, 2 * d:3 * d] * y
        o_ref[rows, :] = _layer_norm(z) * g_ref[...] + be_ref[...]

    n_pairs = n_heads // 2
    assert n_oproj < n_pairs
    items = [(blk, p) for blk in range(STEP_BLOCKS) for p in range(n_pairs)]
    s_next = scores(*items[0])
    y_chunks = []
    for n, (blk, p) in enumerate(items):
        s_cur = s_next
        if n + 1 < len(items):
            s_next = scores(*items[n + 1])
        if n < len(POOL_WINDOWS):
            pool_group(n)
        emit(blk, p, weighted_values(blk, p, softmax_numerators(*s_cur)))
        if blk > 0 and p < n_oproj:
            y_chunks.append(project_out(blk - 1, p))
        if blk > 0 and p == n_oproj:
            finish(blk - 1, y_chunks)
            y_chunks = []
    finish(STEP_BLOCKS - 1, [project_out(STEP_BLOCKS - 1, c) for c in range(n_oproj)])


def _mix(x2d, mod3, u, zp, za, qr, q, kr, v, kc, vc, bias, w_pool, pool_scale, w_out, b_out, ln_g, ln_b,
         batch, seq, ctx_len):
    t, d = x2d.shape
    n_blk = seq // ST
    halo_per_blk = ST // POOL_HALO
    n_halo = t // POOL_HALO
    d_mix = w_out.shape[0]
    n_rows = seq // GRID_W
    single = pl.Buffered(1)

    def tok(b, i):
        return (b * n_blk + i, 0)

    full =lambda shape: pl.BlockSpec(shape, lambda b, i: (0,) * len(shape), pipeline_mode=single)
    in_specs = [
        pl.BlockSpec((ST, d), tok),
        pl.BlockSpec((1, 1, mod3.shape[2]), lambda b, i: (b, 0, 0)),
        pl.BlockSpec((ST, d), tok),
        pl.BlockSpec((POOL_HALO, d),
                     lambda b, i: (jnp.maximum((b * n_blk + i) * halo_per_blk - 1, 0), 0)),
        pl.BlockSpec((POOL_HALO, d),
                     lambda b, i: (jnp.minimum((b * n_blk + i + 1) * halo_per_blk, n_halo - 1), 0)),
        pl.BlockSpec((ST, d), tok),
        pl.BlockSpec((ST, d), tok),
        pl.BlockSpec((ST, d), tok),
        pl.BlockSpec((ST, d), tok),
    ]
    in_specs += [pl.BlockSpec(memory_space=pl.ANY)] * 2
    in_specs += [
        full(bias.shape),
        pl.BlockSpec((ctx_len, d), lambda b, i: (b, 0)),
        pl.BlockSpec((ctx_len, d), lambda b, i: (b, 0)),
        full(w_pool.shape), full(pool_scale.shape), full(w_out.shape),
        full(b_out.shape), full(ln_g.shape), full(ln_b.shape),
    ]
    args = ([x2d, mod3, u, u, u, zp, za, qr, q, kr, v]
            + [bias, kc, vc, w_pool, pool_scale, w_out, b_out, ln_g, ln_b])
    return pl.pallas_call(
        functools.partial(_mix_kernel, seq),
        grid=(batch, n_blk),
        in_specs=in_specs,
        out_specs=pl.BlockSpec((ST, d), tok),
        out_shape=jax.ShapeDtypeStruct((t, d), F32),
        scratch_shapes=[pltpu.VMEM((ST, d_mix), BF16),
                        pltpu.VMEM((2, UNION_ROWS * GRID_W, d), BF16),
                        pltpu.VMEM((2, UNION_ROWS * GRID_W, d), BF16),
                        pltpu.SemaphoreType.DMA((2, 2))],
        compiler_params=pltpu.CompilerParams(dimension_semantics=("arbitrary", "arbitrary"),
                                             vmem_limit_bytes=VMEM_LIMIT_BYTES),
        name="mix",
    )(*args)


def kernel(x, c, ctx, c_ctx, w_ada, b_ada, w_in, b_in, w_pool, pool_scale, rpb, w_out, b_out, ln_g, ln_b):
    batch, seq, d = x.shape
    ctx_len = ctx.shape[1]
    assert w_ada.shape[0] == DEPTH == 1
    assert seq % GRID_W == 0 and seq % INPROJ_TM == 0 and (batch * ctx_len) % INPROJ_TM == 0
    assert seq // GRID_W >= UNION_ROWS and seq % ST == 0
    d_pool = pool_scale.shape[1]
    assert d_pool == d and w_in.shape[2] == 6 * d and rpb.shape[1] * HEAD_DIM == d

    cvec = jnp.zeros((8, d), F32).at[:batch].set(c).at[batch].set(c_ctx)
    mod3 = _ada(cvec, w_ada[0], b_ada[0][None, :]).reshape(8, 1, 3 * d)

    w_in_b = w_in[0].astype(BF16)
    b_in2 = b_in[0][None, :]
    x2d = x.reshape(batch * seq, d)
    tiles_per_seq = seq // INPROJ_TM
    u, zp, qr, q, kr, v, za = _inproj(
        x2d, mod3, lambda i: i // tiles_per_seq, w_in_b, b_in2, range(6),
        ("plain", "plain", "q", "k", "plain", "plain"), (F32, BF16, BF16, BF16, BF16, BF16, BF16),
        _rope_tables(seq), INPROJ_TM)
    kc, vc = _inproj(
        ctx.reshape(batch * ctx_len, d), mod3, lambda i: batch, w_in_b, b_in2, (3, 4),
        ("plain", "plain"), (BF16, BF16), None, INPROJ_TM)

    bias = _bias_tables(rpb[0])
    out = _mix(x2d, mod3, u, zp, za, qr, q, kr, v, kc, vc, bias,
               w_pool[0].astype(BF16), pool_scale, w_out[0].astype(BF16), b_out, ln_g, ln_b,
               batch, seq, ctx_len)
    return out.reshape(batch, seq, d)
```

```python
import functools

import numpy as np
import jax
import jax.numpy as jnp
from jax import lax
from jax.experimental import pallas as pl
from jax.experimental.pallas import tpu as pltpu

GRID_W = 64
POOL_WINDOWS = (2, 4, 8, 16)
HEAD_DIM = 64
NA_ROWS = 8
NA_COLS = 16
ROPE_BASE = 10000.0
LN_EPS = 1e-6
LOG2_E = float(np.log2(np.e))
SCORE_SCALE = HEAD_DIM ** -0.5 * LOG2_E
DEPTH = 1
DEEPNORM_ALPHA = (2.0 * DEPTH) ** 0.25

LANES = 128
VMEM_LIMIT_BYTES = 52 * 1024 * 1024

QUERY_ROWS = 2
QB = QUERY_ROWS * GRID_W
WIN = NA_ROWS * GRID_W
STEP_BLOCKS = 2
STEP_ROWS = STEP_BLOCKS * QUERY_ROWS
ST = STEP_BLOCKS * QB
POOL_FIRST_ITEM = 4
UNION_ROWS = NA_ROWS + STEP_ROWS - 1
POOL_HALO = 8
NEG = -1e30
INPROJ_TM = 512
F32 = jnp.float32
BF16 = jnp.bfloat16


def _silu(x):
    return x * jax.nn.sigmoid(x)


def _layer_norm(x):
    mu = jnp.mean(x, axis=-1, keepdims=True)
    xc = x - mu
    var = jnp.mean(xc * xc, axis=-1, keepdims=True)
    return xc * lax.rsqrt(var + LN_EPS)


def _ada_kernel(c_ref, w_ref, b_ref, o_ref):
    s = _silu(c_ref[...]).astype(BF16)
    o_ref[...] = jnp.dot(s, w_ref[...].astype(BF16), preferred_element_type=F32) + b_ref[...]


def _ada(cvec, w_ada, b_ada):
    rows, d = cvec.shape
    n = w_ada.shape[1]
    tn = 768
    return pl.pallas_call(
        _ada_kernel,
        grid=(n // tn,),
        in_specs=[pl.BlockSpec((rows, d), lambda j: (0, 0)),
                  pl.BlockSpec((d, tn), lambda j: (0, j)),
                  pl.BlockSpec((1, tn), lambda j: (0, j))],
        out_specs=pl.BlockSpec((rows, tn), lambda j: (0, j)),
        out_shape=jax.ShapeDtypeStruct((rows, n), F32),
        compiler_params=pltpu.CompilerParams(dimension_semantics=("arbitrary",),
                                             vmem_limit_bytes=VMEM_LIMIT_BYTES),
        name="ada",
    )(cvec, w_ada, b_ada)


def _rope(x, cos, sin):
    half = HEAD_DIM // 4
    first_half = lax.broadcasted_iota(jnp.int32, (1, LANES), 1) % (2 * half) < half
    outs = []
    for c in range(x.shape[1] // LANES):
        xc = x[:, c * LANES:(c + 1) * LANES]
        partner = jnp.where(first_half, pltpu.roll(xc, LANES - half, axis=1), pltpu.roll(xc, half, axis=1))
        outs.append(xc * cos + partner * sin)
    return jnp.concatenate(outs, axis=1)


def _inproj_kernel(kinds, has_rope, *refs):
    n_blk = len(kinds)
    x_ref, mod_ref = refs[:2]
    w_refs = refs[2:2 + n_blk]
    b_refs = refs[2 + n_blk:2 + 2 * n_blk]
    n_in = 2 + 2 * n_blk + (2 if has_rope else 0)
    cos_ref, sin_ref = refs[n_in - 2:n_in] if has_rope else (None, None)
    out_refs = refs[n_in:]
    d = x_ref.shape[1]

    mod = mod_ref[0]
    tm = x_ref.shape[0]

    def modulated(rows):
        return (_layer_norm(x_ref[rows, :]) * (1.0 + mod[:, d:2 * d]) + mod[:, :d]).astype(BF16)

    h_halves = [modulated(slice(0, tm // 2)), modulated(slice(tm // 2, tm))]
    h = jnp.concatenate(h_halves, axis=0)

    o = 0
    for n, (kind, w_ref, b_ref) in enumerate(zip(kinds, w_refs, b_refs)):
        w = w_ref[...].astype(BF16)
        if n == 0:
            acc = jnp.concatenate([jnp.dot(hh, w, preferred_element_type=F32) for hh in h_halves], axis=0)
        else:
            acc = jnp.dot(h, w, preferred_element_type=F32)
        acc = acc + b_ref[...]
        if kind == "q":
            qr_ref, q_ref = out_refs[o:o + 2]
            o += 2
            qs = acc * SCORE_SCALE
            q_ref[...] = qs.astype(q_ref.dtype)
            qr_ref[...] = _rope(qs, cos_ref[...], sin_ref[...]).astype(qr_ref.dtype)
        else:
            val = _rope(acc, cos_ref[...], sin_ref[...]) if kind == "k" else acc
            out_refs[o][...] = val.astype(out_refs[o].dtype)
            o += 1


def _inproj(x2d, mod3, mod_rows_per_tile, w_bf16, b_in, col_blocks, kinds, out_dtypes, rope, tm):
    t, d = x2d.shape
    has_rope = rope is not None
    resident = pl.Buffered(1)
    in_specs = [
        pl.BlockSpec((tm, d), lambda i: (i, 0)),
        pl.BlockSpec((1, 1, mod3.shape[2]), lambda i: (mod_rows_per_tile(i), 0, 0)),
    ]
    in_specs += [pl.BlockSpec((d, d), lambda i, c=c: (0, c), pipeline_mode=resident) for c in col_blocks]
    in_specs += [pl.BlockSpec((1, d), lambda i, c=c: (0, c), pipeline_mode=resident) for c in col_blocks]
    args = [x2d, mod3] + [w_bf16] * len(col_blocks) + [b_in] * len(col_blocks)
    if has_rope:
        tiles_per_seq = rope[0].shape[0] // tm
        for tab in rope:
            in_specs.append(pl.BlockSpec((tm, LANES), lambda i: (i % tiles_per_seq, 0)))
            args.append(tab)
    out_specs = [pl.BlockSpec((tm, d), lambda i: (i, 0)) for _ in out_dtypes]
    out_shape = [jax.ShapeDtypeStruct((t, d), dt) for dt in out_dtypes]
    return pl.pallas_call(
        functools.partial(_inproj_kernel, tuple(kinds), has_rope),
        grid=(t // tm,),
        in_specs=in_specs,
        out_specs=out_specs,
        out_shape=out_shape,
        compiler_params=pltpu.CompilerParams(dimension_semantics=("arbitrary",),
                                             vmem_limit_bytes=VMEM_LIMIT_BYTES),
        name="inproj_rope" if has_rope else "inproj_ctx",
    )(*args)


def _rope_tables(seq):
    half = HEAD_DIM // 4
    inv_freq = ROPE_BASE ** (-np.arange(half, dtype=np.float64) / half)
    t = np.arange(seq)
    row, col = t // GRID_W, t % GRID_W
    lane = np.arange(LANES)
    pos = np.where((lane % HEAD_DIM < HEAD_DIM // 2)[None, :], row[:, None], col[:, None]).astype(np.float64)
    ang = pos * inv_freq[lane % half][None, :]
    sign = np.where(lane % (2 * half) < half, -1.0, 1.0)[None, :]
    return jnp.asarray(np.cos(ang), F32), jnp.asarray(np.sin(ang) * sign, F32)


def _window_row_start(r, n_rows):
    return jnp.clip(r - NA_ROWS // 2, 0, n_rows - NA_ROWS)


def _bias_kernel(rpb_ref, o_ref):
    lane = lax.broadcasted_iota(jnp.int32, (GRID_W, LANES), 1)
    qc = lax.broadcasted_iota(jnp.int32, (GRID_W, LANES), 0)
    kc = lane % GRID_W
    cs = jnp.clip(qc - NA_COLS // 2, 0, GRID_W - NA_COLS)
    col_ok = (kc >= cs) & (kc < cs + NA_COLS)
    low_half = lane < GRID_W

    for h in range(o_ref.shape[0]):
        table = rpb_ref[h] * LOG2_E

        @functools.cache
        def shifted(dr, half, table=table):
            row = jnp.broadcast_to(table[dr:dr + 1, :], (GRID_W, LANES))
            shift = (LANES - (NA_COLS - 1) + GRID_W * half) % LANES
            return pltpu.roll(row, shift, axis=1, stride=1, stride_axis=0)

        for i in range(o_ref.shape[1]):
            o_ref[h, i] = jnp.where(col_ok, jnp.where(low_half, shifted(i, 0), shifted(i + 1, 1)), NEG)


def _bias_tables(rpb):
    n_heads, n_dr, n_dc = rpb.shape
    assert n_dr == 2 * NA_ROWS - 1 and n_dc == 2 * NA_COLS - 1
    rpb_pad = jnp.pad(rpb, ((0, 0), (0, 16 - n_dr), (0, LANES - n_dc)))
    n_tiles = n_dr - 1
    heads_per_step = 8
    assert n_heads % heads_per_step == 0
    return pl.pallas_call(
        _bias_kernel,
        grid=(n_heads // heads_per_step,),
        in_specs=[pl.BlockSpec((heads_per_step, 16, LANES), lambda h: (h, 0, 0))],
        out_specs=pl.BlockSpec((heads_per_step, n_tiles, GRID_W, LANES), lambda h: (h, 0, 0, 0)),
        out_shape=jax.ShapeDtypeStruct((n_heads, n_tiles, GRID_W, LANES), F32),
        compiler_params=pltpu.CompilerParams(dimension_semantics=("arbitrary",),
                                             vmem_limit_bytes=VMEM_LIMIT_BYTES),
        name="bias_tables",
    )(rpb_pad)


def _union_row_start(i, n_rows):
    return jnp.minimum(_window_row_start(i * STEP_ROWS, n_rows), n_rows - UNION_ROWS)


def _mix_kernel(seq, x_ref, mod_ref, u_ref, up_ref, un_ref, zp_ref, za_ref, qr_ref, q_ref, k_hbm, v_hbm, *rest):
    (bias_ref, kc_ref, vc_ref, wp_ref, ps_ref, wo_ref, bo_ref, g_ref, be_ref,
     o_ref, cat_ref, k_buf, v_buf, win_sem) = rest
    d = x_ref.shape[1]
    d_pool = ps_ref.shape[1]
    gdim = d_pool // len(POOL_WINDOWS)
    n_heads = (cat_ref.shape[1] - d_pool) // HEAD_DIM
    n_rows = seq // GRID_W
    i = pl.program_id(1)
    n_blk = pl.num_programs(1)

    step = pl.program_id(0) * n_blk + i
    n_steps = pl.num_programs(0) * n_blk
    slot = step % 2

    def window_copies(s, into):
        tok0 = pl.multiple_of((s // n_blk) * seq + _union_row_start(s % n_blk, n_rows) * GRID_W, GRID_W)
        return [pltpu.make_async_copy(hbm.at[pl.ds(tok0, UNION_ROWS * GRID_W)], buf.at[into], win_sem.at[n, into])
                for n, (hbm, buf) in enumerate(((k_hbm, k_buf), (v_hbm, v_buf)))]

    @pl.when(step == 0)
    def _():
        for cp in window_copies(step, slot):
            cp.start()

    for cp in window_copies(step, slot):
        cp.wait()

    @pl.when(step + 1 < n_steps)
    def _():
        for cp in window_copies(step + 1, 1 - slot):
            cp.start()

    win_row = [_window_row_start(i * STEP_ROWS + r, n_rows) for r in range(STEP_ROWS)]
    win_off = [pl.multiple_of((win_row[r] - _union_row_start(i, n_rows)) * GRID_W, GRID_W) for r in range(STEP_ROWS)]
    win_dr = [win_row[r] - (i * STEP_ROWS + r) + NA_ROWS - 1 for r in range(STEP_ROWS)]

    u = u_ref[...]
    prev = jnp.where(i > 0, up_ref[...], 0.0)
    nxt = jnp.where(i < n_blk - 1, un_ref[...], 0.0)
    ext = jnp.concatenate([prev, u, nxt], axis=0)
    n_ext = ext.shape[0]
    t = i * ST + lax.broadcasted_iota(jnp.int32, (ST, 1), 0)

    def shift_sum(s, k):
        return pltpu.roll(s, k, axis=0) + pltpu.roll(s, n_ext - k, axis=0)

    def pool_group(g):
        w = POOL_WINDOWS[g]
        cols = slice(g * gdim, (g + 1) * gdim)
        s = ext[:, cols]
        s = s + pltpu.roll(s, 1, axis=0)
        k = 1
        while 2 * k < w:
            s = shift_sum(s, k)
            k *= 2
        s = s[POOL_HALO:POOL_HALO + ST]
        lo = jnp.clip(t - w // 2, 0, seq)
        hi = jnp.clip(t - w // 2 + w, 0, seq)
        pooled = s / (hi - lo).astype(F32) - u[:, cols]
        y = jnp.dot(pooled.astype(BF16), wp_ref[g], preferred_element_type=F32)
        y = y * ps_ref[:, cols] * _silu(zp_ref[:, cols].astype(F32))
        cat_ref[:, cols] = y.astype(BF16)

    nt_dims = (((1,), (1,)), ((), ()))
    lane = lax.broadcasted_iota(jnp.int32, (1, LANES), 1)
    first_head = lane < HEAD_DIM
    ones_win = jnp.ones((WIN, LANES), BF16)
    ones_ctx = jnp.ones((kc_ref.shape[0], LANES), BF16)
    stacked = 2 * GRID_W

    def chunk(p):
        return slice(p * LANES, (p + 1) * LANES)

    def block_rows(blk):
        return slice(blk * QB, (blk + 1) * QB)

    def stack_heads(ref, blk, p):
        qp = ref[block_rows(blk), chunk(p)]
        zero = jnp.zeros_like(qp)
        first, second = jnp.where(first_head, qp, zero), jnp.where(first_head, zero, qp)
        pieces = []
        for a in range(QUERY_ROWS):
            pieces += [first[a * GRID_W:(a + 1) * GRID_W], second[a * GRID_W:(a + 1) * GRID_W]]
        return jnp.concatenate(pieces, axis=0)

    def lane_tiles(a):
        return [a[:, c * LANES:(c + 1) * LANES] for c in range(a.shape[1] // LANES)]

    def scores(blk, p):
        q_rot = stack_heads(qr_ref, blk, p)
        s_loc = []
        for a in range(QUERY_ROWS):
            r = blk * QUERY_ROWS + a
            k_win = k_buf[slot, pl.ds(win_off[r], WIN), chunk(p)]
            bias = jnp.concatenate(
                [jnp.concatenate([bias_ref[2 * p + h, win_dr[r] + jr] for jr in range(0, NA_ROWS, 2)], axis=1)
                 for h in range(2)], axis=0)
            s_loc.append(lax.dot_general(q_rot[a * stacked:(a + 1) * stacked], k_win, nt_dims,
                                         preferred_element_type=F32) + bias)
        s_ctx = lax.dot_general(stack_heads(q_ref, blk, p), kc_ref[:, chunk(p)], nt_dims,
                                preferred_element_type=F32)
        return jnp.concatenate(s_loc, axis=0), s_ctx

    def softmax_numerators(s_loc, s_ctx):
        m = jnp.max(functools.reduce(jnp.maximum, lane_tiles(s_loc) + lane_tiles(s_ctx)), axis=1, keepdims=True)
        return jnp.concatenate([jnp.exp2(s_loc - m), jnp.exp2(s_ctx - m)], axis=1).astype(BF16)

    def weighted_values(blk, p, probs):
        vc_aug = jnp.concatenate([vc_ref[:, chunk(p)], ones_ctx], axis=1)
        outs = []
        for a in range(QUERY_ROWS):
            v_win = v_buf[slot, pl.ds(win_off[blk * QUERY_ROWS + a], WIN), chunk(p)]
            v_aug = jnp.concatenate([jnp.concatenate([v_win, ones_win], axis=1), vc_aug], axis=0)
            outs.append(jnp.dot(probs[a * stacked:(a + 1) * stacked], v_aug, preferred_element_type=F32))
        return outs

    def emit(blk, p, outs):
        num = jnp.concatenate([jnp.where(first_head, o[:GRID_W, :LANES], o[GRID_W:, :LANES]) for o in outs], axis=0)
        den = jnp.concatenate([jnp.where(first_head, o[:GRID_W, LANES:], o[GRID_W:, LANES:]) for o in outs], axis=0)
        out = num / den * _silu(za_ref[block_rows(blk), chunk(p)].astype(F32))
        cat_ref[block_rows(blk), d_pool + p * LANES:d_pool + (p + 1) * LANES] = out.astype(BF16)

    oproj_n = d
    n_oproj = d // oproj_n

    def project_out(blk, c):
        cols = slice(c * oproj_n, (c + 1) * oproj_n)
        return jnp.dot(cat_ref[block_rows(blk), :], wo_ref[:, cols], preferred_element_type=F32) + bo_ref[:, cols]

    def finish(blk, y_chunks):
        rows = block_rows(blk)
        y = jnp.concatenate(y_chunks, axis=1)
        z = DEEPNORM_ALPHA * x_ref[rows, :] + mod_ref[0][:, 2 * d:3 * d] * y
        o_ref[rows, :] = _layer_norm(z) * g_ref[...] + be_ref[...]

    n_pairs = n_heads // 2
    assert n_oproj < n_pairs
    items = [(blk, p) for blk in range(STEP_BLOCKS) for p in range(n_pairs)]
    s_next = scores(*items[0])
    y_chunks = []
    for n, (blk, p) in enumerate(items):
        s_cur = s_next
        if n + 1 < len(items):
            s_next = scores(*items[n + 1])
        if POOL_FIRST_ITEM <= n < POOL_FIRST_ITEM + len(POOL_WINDOWS):
            pool_group(n - POOL_FIRST_ITEM)
        emit(blk, p, weighted_values(blk, p, softmax_numerators(*s_cur)))
        if blk > 0 and p < n_oproj:
            y_chunks.append(project_out(blk - 1, p))
        if blk > 0 and p == n_oproj:
            finish(blk - 1, y_chunks)
            y_chunks = []
    finish(STEP_BLOCKS - 1, [project_out(STEP_BLOCKS - 1, c) for c in range(n_oproj)])


def _mix(x2d, mod3, u, zp, za, qr, q, kr, v, kc, vc, bias, w_pool, pool_scale, w_out, b_out, ln_g, ln_b,
         batch, seq, ctx_len):
    t, d = x2d.shape
    n_blk = seq // ST
    halo_per_blk = ST // POOL_HALO
    n_halo = t // POOL_HALO
    d_mix = w_out.shape[0]
    n_rows = seq // GRID_W
    single = pl.Buffered(1)

    def tok(b, i):
        return (b * n_blk + i, 0)

    full =lambda shape: pl.BlockSpec(shape, lambda b, i: (0,) * len(shape), pipeline_mode=single)
    in_specs = [
        pl.BlockSpec((ST, d), tok),
        pl.BlockSpec((1, 1, mod3.shape[2]), lambda b, i: (b, 0, 0)),
        pl.BlockSpec((ST, d), tok),
        pl.BlockSpec((POOL_HALO, d),
                     lambda b, i: (jnp.maximum((b * n_blk + i) * halo_per_blk - 1, 0), 0)),
        pl.BlockSpec((POOL_HALO, d),
                     lambda b, i: (jnp.minimum((b * n_blk + i + 1) * halo_per_blk, n_halo - 1), 0)),
        pl.BlockSpec((ST, d), tok),
        pl.BlockSpec((ST, d), tok),
        pl.BlockSpec((ST, d), tok),
        pl.BlockSpec((ST, d), tok),
    ]
    in_specs += [pl.BlockSpec(memory_space=pl.ANY)] * 2
    in_specs += [
        full(bias.shape),
        pl.BlockSpec((ctx_len, d), lambda b, i: (b, 0)),
        pl.BlockSpec((ctx_len, d), lambda b, i: (b, 0)),
        full(w_pool.shape), full(pool_scale.shape), full(w_out.shape),
        full(b_out.shape), full(ln_g.shape), full(ln_b.shape),
    ]
    args = ([x2d, mod3, u, u, u, zp, za, qr, q, kr, v]
            + [bias, kc, vc, w_pool, pool_scale, w_out, b_out, ln_g, ln_b])
    return pl.pallas_call(
        functools.partial(_mix_kernel, seq),
        grid=(batch, n_blk),
        in_specs=in_specs,
        out_specs=pl.BlockSpec((ST, d), tok),
        out_shape=jax.ShapeDtypeStruct((t, d), F32),
        scratch_shapes=[pltpu.VMEM((ST, d_mix), BF16),
                        pltpu.VMEM((2, UNION_ROWS * GRID_W, d), BF16),
                        pltpu.VMEM((2, UNION_ROWS * GRID_W, d), BF16),
                        pltpu.SemaphoreType.DMA((2, 2))],
        compiler_params=pltpu.CompilerParams(dimension_semantics=("arbitrary", "arbitrary"),
                                             vmem_limit_bytes=VMEM_LIMIT_BYTES),
        name="mix",
    )(*args)


def kernel(x, c, ctx, c_ctx, w_ada, b_ada, w_in, b_in, w_pool, pool_scale, rpb, w_out, b_out, ln_g, ln_b):
    batch, seq, d = x.shape
    ctx_len = ctx.shape[1]
    assert w_ada.shape[0] == DEPTH == 1
    assert seq % GRID_W == 0 and seq % INPROJ_TM == 0 and (batch * ctx_len) % INPROJ_TM == 0
    assert seq // GRID_W >= UNION_ROWS and seq % ST == 0
    d_pool = pool_scale.shape[1]
    assert d_pool == d and w_in.shape[2] == 6 * d and rpb.shape[1] * HEAD_DIM == d

    cvec = jnp.zeros((8, d), F32).at[:batch].set(c).at[batch].set(c_ctx)
    mod3 = _ada(cvec, w_ada[0], b_ada[0][None, :]).reshape(8, 1, 3 * d)

    w_in_b = w_in[0]
    b_in2 = b_in[0][None, :]
    x2d = x.reshape(batch * seq, d)
    tiles_per_seq = seq // INPROJ_TM
    u, zp, qr, q, kr, v, za = _inproj(
        x2d, mod3, lambda i: i // tiles_per_seq, w_in_b, b_in2, range(6),
        ("plain", "plain", "q", "k", "plain", "plain"), (F32, BF16, BF16, BF16, BF16, BF16, BF16),
        _rope_tables(seq), INPROJ_TM)
    kc, vc = _inproj(
        ctx.reshape(batch * ctx_len, d), mod3, lambda i: batch, w_in_b, b_in2, (3, 4),
        ("plain", "plain"), (BF16, BF16), None, INPROJ_TM)

    bias = _bias_tables(rpb[0])
    out = _mix(x2d, mod3, u, zp, za, qr, q, kr, v, kc, vc, bias,
               w_pool[0].astype(BF16), pool_scale, w_out[0].astype(BF16), b_out, ln_g, ln_b,
               batch, seq, ctx_len)
    return out.reshape(batch, seq, d)
```

```python
import functools

import numpy as np
import jax
import jax.numpy as jnp
from jax import lax
from jax.experimental import pallas as pl
from jax.experimental.pallas import tpu as pltpu

GRID_W = 64
POOL_WINDOWS = (2, 4, 8, 16)
HEAD_DIM = 64
NA_ROWS = 8
NA_COLS = 16
ROPE_BASE = 10000.0
LN_EPS = 1e-6
LOG2_E = float(np.log2(np.e))
SCORE_SCALE = HEAD_DIM ** -0.5 * LOG2_E
DEPTH = 1
DEEPNORM_ALPHA = (2.0 * DEPTH) ** 0.25

LANES = 128
VMEM_LIMIT_BYTES = 52 * 1024 * 1024

QUERY_ROWS = 2
QB = QUERY_ROWS * GRID_W
WIN = NA_ROWS * GRID_W
STEP_BLOCKS = 4
STEP_ROWS = STEP_BLOCKS * QUERY_ROWS
ST = STEP_BLOCKS * QB
POOL_FIRST_PAIR = 2
UNION_ROWS = NA_ROWS + STEP_ROWS - 1
POOL_HALO = 8
NEG = -1e30
INPROJ_TM = 512
F32 = jnp.float32
BF16 = jnp.bfloat16


def _silu(x):
    return x * jax.nn.sigmoid(x)


def _layer_norm(x):
    mu = jnp.mean(x, axis=-1, keepdims=True)
    xc = x - mu
    var = jnp.mean(xc * xc, axis=-1, keepdims=True)
    return xc * lax.rsqrt(var + LN_EPS)


def _ada_kernel(c_ref, w_ref, b_ref, o_ref):
    s = _silu(c_ref[...]).astype(BF16)
    o_ref[...] = jnp.dot(s, w_ref[...].astype(BF16), preferred_element_type=F32) + b_ref[...]


def _ada(cvec, w_ada, b_ada):
    rows, d = cvec.shape
    n = w_ada.shape[1]
    tn = 768
    return pl.pallas_call(
        _ada_kernel,
        grid=(n // tn,),
        in_specs=[pl.BlockSpec((rows, d), lambda j: (0, 0)),
                  pl.BlockSpec((d, tn), lambda j: (0, j)),
                  pl.BlockSpec((1, tn), lambda j: (0, j))],
        out_specs=pl.BlockSpec((rows, tn), lambda j: (0, j)),
        out_shape=jax.ShapeDtypeStruct((rows, n), F32),
        compiler_params=pltpu.CompilerParams(dimension_semantics=("arbitrary",),
                                             vmem_limit_bytes=VMEM_LIMIT_BYTES),
        name="ada",
    )(cvec, w_ada, b_ada)


def _rope(x, cos, sin):
    half = HEAD_DIM // 4
    first_half = lax.broadcasted_iota(jnp.int32, (1, LANES), 1) % (2 * half) < half
    outs = []
    for c in range(x.shape[1] // LANES):
        xc = x[:, c * LANES:(c + 1) * LANES]
        partner = jnp.where(first_half, pltpu.roll(xc, LANES - half, axis=1), pltpu.roll(xc, half, axis=1))
        outs.append(xc * cos + partner * sin)
    return jnp.concatenate(outs, axis=1)


def _inproj_kernel(kinds, has_rope, *refs):
    n_blk = len(kinds)
    x_ref, mod_ref = refs[:2]
    w_refs = refs[2:2 + n_blk]
    b_refs = refs[2 + n_blk:2 + 2 * n_blk]
    n_in = 2 + 2 * n_blk + (2 if has_rope else 0)
    cos_ref, sin_ref = refs[n_in - 2:n_in] if has_rope else (None, None)
    out_refs = refs[n_in:]
    d = x_ref.shape[1]

    mod = mod_ref[0]
    tm = x_ref.shape[0]

    def modulated(rows):
        return (_layer_norm(x_ref[rows, :]) * (1.0 + mod[:, d:2 * d]) + mod[:, :d]).astype(BF16)

    h_halves = [modulated(slice(0, tm // 2)), modulated(slice(tm // 2, tm))]
    h = jnp.concatenate(h_halves, axis=0)

    o = 0
    for n, (kind, w_ref, b_ref) in enumerate(zip(kinds, w_refs, b_refs)):
        w = w_ref[...].astype(BF16)
        if n == 0:
            acc = jnp.concatenate([jnp.dot(hh, w, preferred_element_type=F32) for hh in h_halves], axis=0)
        else:
            acc = jnp.dot(h, w, preferred_element_type=F32)
        acc = acc + b_ref[...]
        if kind == "q":
            qr_ref, q_ref = out_refs[o:o + 2]
            o += 2
            qs = acc * SCORE_SCALE
            q_ref[...] = qs.astype(q_ref.dtype)
            qr_ref[...] = _rope(qs, cos_ref[...], sin_ref[...]).astype(qr_ref.dtype)
        else:
            val = _rope(acc, cos_ref[...], sin_ref[...]) if kind == "k" else acc
            out_refs[o][...] = val.astype(out_refs[o].dtype)
            o += 1


def _inproj(x2d, mod3, mod_rows_per_tile, w_bf16, b_in, col_blocks, kinds, out_dtypes, rope, tm):
    t, d = x2d.shape
    has_rope = rope is not None
    resident = pl.Buffered(1)
    in_specs = [
        pl.BlockSpec((tm, d), lambda i: (i, 0)),
        pl.BlockSpec((1, 1, mod3.shape[2]), lambda i: (mod_rows_per_tile(i), 0, 0)),
    ]
    in_specs += [pl.BlockSpec((d, d), lambda i, c=c: (0, c), pipeline_mode=resident) for c in col_blocks]
    in_specs += [pl.BlockSpec((1, d), lambda i, c=c: (0, c), pipeline_mode=resident) for c in col_blocks]
    args = [x2d, mod3] + [w_bf16] * len(col_blocks) + [b_in] * len(col_blocks)
    if has_rope:
        tiles_per_seq = rope[0].shape[0] // tm
        for tab in rope:
            in_specs.append(pl.BlockSpec((tm, LANES), lambda i: (i % tiles_per_seq, 0)))
            args.append(tab)
    out_specs = [pl.BlockSpec((tm, d), lambda i: (i, 0)) for _ in out_dtypes]
    out_shape = [jax.ShapeDtypeStruct((t, d), dt) for dt in out_dtypes]
    return pl.pallas_call(
        functools.partial(_inproj_kernel, tuple(kinds), has_rope),
        grid=(t // tm,),
        in_specs=in_specs,
        out_specs=out_specs,
        out_shape=out_shape,
        compiler_params=pltpu.CompilerParams(dimension_semantics=("arbitrary",),
                                             vmem_limit_bytes=VMEM_LIMIT_BYTES),
        name="inproj_rope" if has_rope else "inproj_ctx",
    )(*args)


def _rope_tables(seq):
    half = HEAD_DIM // 4
    inv_freq = ROPE_BASE ** (-np.arange(half, dtype=np.float64) / half)
    t = np.arange(seq)
    row, col = t // GRID_W, t % GRID_W
    lane = np.arange(LANES)
    pos = np.where((lane % HEAD_DIM < HEAD_DIM // 2)[None, :], row[:, None], col[:, None]).astype(np.float64)
    ang = pos * inv_freq[lane % half][None, :]
    sign = np.where(lane % (2 * half) < half, -1.0, 1.0)[None, :]
    return jnp.asarray(np.cos(ang), F32), jnp.asarray(np.sin(ang) * sign, F32)


def _window_row_start(r, n_rows):
    return jnp.clip(r - NA_ROWS // 2, 0, n_rows - NA_ROWS)


def _bias_kernel(rpb_ref, o_ref):
    lane = lax.broadcasted_iota(jnp.int32, (GRID_W, LANES), 1)
    qc = lax.broadcasted_iota(jnp.int32, (GRID_W, LANES), 0)
    kc = lane % GRID_W
    cs = jnp.clip(qc - NA_COLS // 2, 0, GRID_W - NA_COLS)
    col_ok = (kc >= cs) & (kc < cs + NA_COLS)
    low_half = lane < GRID_W

    for h in range(o_ref.shape[0]):
        table = rpb_ref[h] * LOG2_E

        @functools.cache
        def shifted(dr, half, table=table):
            row = jnp.broadcast_to(table[dr:dr + 1, :], (GRID_W, LANES))
            shift = (LANES - (NA_COLS - 1) + GRID_W * half) % LANES
            return pltpu.roll(row, shift, axis=1, stride=1, stride_axis=0)

        for i in range(o_ref.shape[1]):
            o_ref[h, i] = jnp.where(col_ok, jnp.where(low_half, shifted(i, 0), shifted(i + 1, 1)), NEG)


def _bias_tables(rpb):
    n_heads, n_dr, n_dc = rpb.shape
    assert n_dr == 2 * NA_ROWS - 1 and n_dc == 2 * NA_COLS - 1
    rpb_pad = jnp.pad(rpb, ((0, 0), (0, 16 - n_dr), (0, LANES - n_dc)))
    n_tiles = n_dr - 1
    heads_per_step = 8
    assert n_heads % heads_per_step == 0
    return pl.pallas_call(
        _bias_kernel,
        grid=(n_heads // heads_per_step,),
        in_specs=[pl.BlockSpec((heads_per_step, 16, LANES), lambda h: (h, 0, 0))],
        out_specs=pl.BlockSpec((heads_per_step, n_tiles, GRID_W, LANES), lambda h: (h, 0, 0, 0)),
        out_shape=jax.ShapeDtypeStruct((n_heads, n_tiles, GRID_W, LANES), F32),
        compiler_params=pltpu.CompilerParams(dimension_semantics=("arbitrary",),
                                             vmem_limit_bytes=VMEM_LIMIT_BYTES),
        name="bias_tables",
    )(rpb_pad)


def _union_row_start(i, n_rows):
    return jnp.minimum(_window_row_start(i * STEP_ROWS, n_rows), n_rows - UNION_ROWS)


def _mix_kernel(seq, x_ref, mod_ref, u_ref, up_ref, un_ref, zp_ref, za_ref, qr_ref, q_ref, k_hbm, v_hbm, *rest):
    (bias_ref, kc_ref, vc_ref, wp_ref, ps_ref, wo_ref, bo_ref, g_ref, be_ref,
     o_ref, cat_ref, k_buf, v_buf, win_sem) = rest
    d = x_ref.shape[1]
    d_pool = ps_ref.shape[1]
    gdim = d_pool // len(POOL_WINDOWS)
    n_heads = (cat_ref.shape[1] - d_pool) // HEAD_DIM
    n_rows = seq // GRID_W
    i = pl.program_id(1)
    n_blk = pl.num_programs(1)

    step = pl.program_id(0) * n_blk + i
    n_steps = pl.num_programs(0) * n_blk
    slot = step % 2

    def window_copies(s, into):
        tok0 = pl.multiple_of((s // n_blk) * seq + _union_row_start(s % n_blk, n_rows) * GRID_W, GRID_W)
        return [pltpu.make_async_copy(hbm.at[pl.ds(tok0, UNION_ROWS * GRID_W)], buf.at[into], win_sem.at[n, into])
                for n, (hbm, buf) in enumerate(((k_hbm, k_buf), (v_hbm, v_buf)))]

    @pl.when(step == 0)
    def _():
        for cp in window_copies(step, slot):
            cp.start()

    for cp in window_copies(step, slot):
        cp.wait()

    @pl.when(step + 1 < n_steps)
    def _():
        for cp in window_copies(step + 1, 1 - slot):
            cp.start()

    win_row = [_window_row_start(i * STEP_ROWS + r, n_rows) for r in range(STEP_ROWS)]
    win_off = [pl.multiple_of((win_row[r] - _union_row_start(i, n_rows)) * GRID_W, GRID_W) for r in range(STEP_ROWS)]
    win_dr = [win_row[r] - (i * STEP_ROWS + r) + NA_ROWS - 1 for r in range(STEP_ROWS)]

    n_ext = QB + 2 * POOL_HALO

    def block_rows(blk):
        return slice(blk * QB, (blk + 1) * QB)

    def shift_sum(s, k):
        return pltpu.roll(s, k, axis=0) + pltpu.roll(s, n_ext - k, axis=0)

    def pool_group(blk, g):
        w = POOL_WINDOWS[g]
        cols = slice(g * gdim, (g + 1) * gdim)
        rows = block_rows(blk)
        before = (jnp.where(i > 0, up_ref[:, cols], 0.0) if blk == 0
                  else u_ref[blk * QB - POOL_HALO:blk * QB, cols])
        after = (jnp.where(i < n_blk - 1, un_ref[:, cols], 0.0) if blk == STEP_BLOCKS - 1
                 else u_ref[(blk + 1) * QB:(blk + 1) * QB + POOL_HALO, cols])
        u = u_ref[rows, cols]
        s = jnp.concatenate([before, u, after], axis=0)
        s = s + pltpu.roll(s, 1, axis=0)
        k = 1
        while 2 * k < w:
            s = shift_sum(s, k)
            k *= 2
        s = s[POOL_HALO:POOL_HALO + QB]
        t = i * ST + blk * QB + lax.broadcasted_iota(jnp.int32, (QB, 1), 0)
        lo = jnp.clip(t - w // 2, 0, seq)
        hi = jnp.clip(t - w // 2 + w, 0, seq)
        pooled = s / (hi - lo).astype(F32) - u
        y = jnp.dot(pooled.astype(BF16), wp_ref[g], preferred_element_type=F32)
        y = y * ps_ref[:, cols] * _silu(zp_ref[rows, cols].astype(F32))
        cat_ref[rows, cols] = y.astype(BF16)

    nt_dims = (((1,), (1,)), ((), ()))
    lane = lax.broadcasted_iota(jnp.int32, (1, LANES), 1)
    first_head = lane < HEAD_DIM
    ones_win = jnp.ones((WIN, LANES), BF16)
    ones_ctx = jnp.ones((kc_ref.shape[0], LANES), BF16)
    stacked = 2 * GRID_W

    def chunk(p):
        return slice(p * LANES, (p + 1) * LANES)

    def stack_heads(ref, blk, p):
        qp = ref[block_rows(blk), chunk(p)]
        zero = jnp.zeros_like(qp)
        first, second = jnp.where(first_head, qp, zero), jnp.where(first_head, zero, qp)
        pieces = []
        for a in range(QUERY_ROWS):
            pieces += [first[a * GRID_W:(a + 1) * GRID_W], second[a * GRID_W:(a + 1) * GRID_W]]
        return jnp.concatenate(pieces, axis=0)

    def lane_tiles(a):
        return [a[:, c * LANES:(c + 1) * LANES] for c in range(a.shape[1] // LANES)]

    def scores(blk, p):
        q_rot = stack_heads(qr_ref, blk, p)
        s_loc = []
        for a in range(QUERY_ROWS):
            r = blk * QUERY_ROWS + a
            k_win = k_buf[slot, pl.ds(win_off[r], WIN), chunk(p)]
            bias = jnp.concatenate(
                [jnp.concatenate([bias_ref[2 * p + h, win_dr[r] + jr] for jr in range(0, NA_ROWS, 2)], axis=1)
                 for h in range(2)], axis=0)
            s_loc.append(lax.dot_general(q_rot[a * stacked:(a + 1) * stacked], k_win, nt_dims,
                                         preferred_element_type=F32) + bias)
        s_ctx = lax.dot_general(stack_heads(q_ref, blk, p), kc_ref[:, chunk(p)], nt_dims,
                                preferred_element_type=F32)
        return jnp.concatenate(s_loc, axis=0), s_ctx

    def softmax_numerators(s_loc, s_ctx):
        m = jnp.max(functools.reduce(jnp.maximum, lane_tiles(s_loc) + lane_tiles(s_ctx)), axis=1, keepdims=True)
        return jnp.concatenate([jnp.exp2(s_loc - m), jnp.exp2(s_ctx - m)], axis=1).astype(BF16)

    def weighted_values(blk, p, probs):
        vc_aug = jnp.concatenate([vc_ref[:, chunk(p)], ones_ctx], axis=1)
        outs = []
        for a in range(QUERY_ROWS):
            v_win = v_buf[slot, pl.ds(win_off[blk * QUERY_ROWS + a], WIN), chunk(p)]
            v_aug = jnp.concatenate([jnp.concatenate([v_win, ones_win], axis=1), vc_aug], axis=0)
            outs.append(jnp.dot(probs[a * stacked:(a + 1) * stacked], v_aug, preferred_element_type=F32))
        return outs

    def emit(blk, p, outs):
        num = jnp.concatenate([jnp.where(first_head, o[:GRID_W, :LANES], o[GRID_W:, :LANES]) for o in outs], axis=0)
        den = jnp.concatenate([jnp.where(first_head, o[:GRID_W, LANES:], o[GRID_W:, LANES:]) for o in outs], axis=0)
        out = num / den * _silu(za_ref[block_rows(blk), chunk(p)].astype(F32))
        cat_ref[block_rows(blk), d_pool + p * LANES:d_pool + (p + 1) * LANES] = out.astype(BF16)

    oproj_n = d
    n_oproj = d // oproj_n

    def project_out(blk, c):
        cols = slice(c * oproj_n, (c + 1) * oproj_n)
        return jnp.dot(cat_ref[block_rows(blk), :], wo_ref[:, cols], preferred_element_type=F32) + bo_ref[:, cols]

    def finish(blk, y_chunks):
        rows = block_rows(blk)
        y = jnp.concatenate(y_chunks, axis=1)
        z = DEEPNORM_ALPHA * x_ref[rows, :] + mod_ref[0][:, 2 * d:3 * d] * y
        o_ref[rows, :] = _layer_norm(z) * g_ref[...] + be_ref[...]

    n_pairs = n_heads // 2
    assert n_oproj < n_pairs
    items = [(blk, p) for blk in range(STEP_BLOCKS) for p in range(n_pairs)]
    s_next = scores(*items[0])
    y_chunks = []
    for n, (blk, p) in enumerate(items):
        s_cur = s_next
        if n + 1 < len(items):
            s_next = scores(*items[n + 1])
        if POOL_FIRST_PAIR <= p < POOL_FIRST_PAIR + len(POOL_WINDOWS):
            pool_group(blk, p - POOL_FIRST_PAIR)
        emit(blk, p, weighted_values(blk, p, softmax_numerators(*s_cur)))
        if blk > 0 and p < n_oproj:
            y_chunks.append(project_out(blk - 1, p))
        if blk > 0 and p == n_oproj:
            finish(blk - 1, y_chunks)
            y_chunks = []
    finish(STEP_BLOCKS - 1, [project_out(STEP_BLOCKS - 1, c) for c in range(n_oproj)])


def _mix(x2d, mod3, u, zp, za, qr, q, kr, v, kc, vc, bias, w_pool, pool_scale, w_out, b_out, ln_g, ln_b,
         batch, seq, ctx_len):
    t, d = x2d.shape
    n_blk = seq // ST
    halo_per_blk = ST // POOL_HALO
    n_halo = t // POOL_HALO
    d_mix = w_out.shape[0]
    n_rows = seq // GRID_W
    single = pl.Buffered(1)

    def tok(b, i):
        return (b * n_blk + i, 0)

    full =lambda shape: pl.BlockSpec(shape, lambda b, i: (0,) * len(shape), pipeline_mode=single)
    in_specs = [
        pl.BlockSpec((ST, d), tok),
        pl.BlockSpec((1, 1, mod3.shape[2]), lambda b, i: (b, 0, 0)),
        pl.BlockSpec((ST, d), tok),
        pl.BlockSpec((POOL_HALO, d),
                     lambda b, i: (jnp.maximum((b * n_blk + i) * halo_per_blk - 1, 0), 0)),
        pl.BlockSpec((POOL_HALO, d),
                     lambda b, i: (jnp.minimum((b * n_blk + i + 1) * halo_per_blk, n_halo - 1), 0)),
        pl.BlockSpec((ST, d), tok),
        pl.BlockSpec((ST, d), tok),
        pl.BlockSpec((ST, d), tok),
        pl.BlockSpec((ST, d), tok),
    ]
    in_specs += [pl.BlockSpec(memory_space=pl.ANY)] * 2
    in_specs += [
        full(bias.shape),
        pl.BlockSpec((ctx_len, d), lambda b, i: (b, 0)),
        pl.BlockSpec((ctx_len, d), lambda b, i: (b, 0)),
        full(w_pool.shape), full(pool_scale.shape), full(w_out.shape),
        full(b_out.shape), full(ln_g.shape), full(ln_b.shape),
    ]
    args = ([x2d, mod3, u, u, u, zp, za, qr, q, kr, v]
            + [bias, kc, vc, w_pool, pool_scale, w_out, b_out, ln_g, ln_b])
    return pl.pallas_call(
        functools.partial(_mix_kernel, seq),
        grid=(batch, n_blk),
        in_specs=in_specs,
        out_specs=pl.BlockSpec((ST, d), tok),
        out_shape=jax.ShapeDtypeStruct((t, d), F32),
        scratch_shapes=[pltpu.VMEM((ST, d_mix), BF16),
                        pltpu.VMEM((2, UNION_ROWS * GRID_W, d), BF16),
                        pltpu.VMEM((2, UNION_ROWS * GRID_W, d), BF16),
                        pltpu.SemaphoreType.DMA((2, 2))],
        compiler_params=pltpu.CompilerParams(dimension_semantics=("arbitrary", "arbitrary"),
                                             vmem_limit_bytes=VMEM_LIMIT_BYTES),
        name="mix",
    )(*args)


def kernel(x, c, ctx, c_ctx, w_ada, b_ada, w_in, b_in, w_pool, pool_scale, rpb, w_out, b_out, ln_g, ln_b):
    batch, seq, d = x.shape
    ctx_len = ctx.shape[1]
    assert w_ada.shape[0] == DEPTH == 1
    assert seq % GRID_W == 0 and seq % INPROJ_TM == 0 and (batch * ctx_len) % INPROJ_TM == 0
    assert seq // GRID_W >= UNION_ROWS and seq % ST == 0
    d_pool = pool_scale.shape[1]
    assert d_pool == d and w_in.shape[2] == 6 * d and rpb.shape[1] * HEAD_DIM == d

    cvec = jnp.zeros((8, d), F32).at[:batch].set(c).at[batch].set(c_ctx)
    mod3 = _ada(cvec, w_ada[0], b_ada[0][None, :]).reshape(8, 1, 3 * d)

    w_in_b = w_in[0]
    b_in2 = b_in[0][None, :]
    x2d = x.reshape(batch * seq, d)
    tiles_per_seq = seq // INPROJ_TM
    u, zp, qr, q, kr, v, za = _inproj(
        x2d, mod3, lambda i: i // tiles_per_seq, w_in_b, b_in2, range(6),
        ("plain", "plain", "q", "k", "plain", "plain"), (F32, BF16, BF16, BF16, BF16, BF16, BF16),
        _rope_tables(seq), INPROJ_TM)
    kc, vc = _inproj(
        ctx.reshape(batch * ctx_len, d), mod3, lambda i: batch, w_in_b, b_in2, (3, 4),
        ("plain", "plain"), (BF16, BF16), None, INPROJ_TM)

    bias = _bias_tables(rpb[0])
    out = _mix(x2d, mod3, u, zp, za, qr, q, kr, v, kc, vc, bias,
               w_pool[0].astype(BF16), pool_scale, w_out[0].astype(BF16), b_out, ln_g, ln_b,
               batch, seq, ctx_len)
    return out.reshape(batch, seq, d)
```

```python
import functools

import numpy as np
import jax
import jax.numpy as jnp
from jax import lax
from jax.experimental import pallas as pl
from jax.experimental.pallas import tpu as pltpu

GRID_W = 64
POOL_WINDOWS = (2, 4, 8, 16)
HEAD_DIM = 64
NA_ROWS = 8
NA_COLS = 16
ROPE_BASE = 10000.0
LN_EPS = 1e-6
LOG2_E = float(np.log2(np.e))
SCORE_SCALE = HEAD_DIM ** -0.5 * LOG2_E
DEPTH = 1
DEEPNORM_ALPHA = (2.0 * DEPTH) ** 0.25

LANES = 128
VMEM_LIMIT_BYTES = 52 * 1024 * 1024

QUERY_ROWS = 2
QB = QUERY_ROWS * GRID_W
WIN = NA_ROWS * GRID_W
STEP_BLOCKS = 4
STEP_ROWS = STEP_BLOCKS * QUERY_ROWS
ST = STEP_BLOCKS * QB
OPROJ_BLOCKS = 4
POOL_FIRST_PAIR = 2
UNION_ROWS = NA_ROWS + STEP_ROWS - 1
POOL_HALO = 8
NEG = -1e30
INPROJ_TM = 512
F32 = jnp.float32
BF16 = jnp.bfloat16


def _silu(x):
    return x * jax.nn.sigmoid(x)


def _layer_norm(x):
    mu = jnp.mean(x, axis=-1, keepdims=True)
    xc = x - mu
    var = jnp.mean(xc * xc, axis=-1, keepdims=True)
    return xc * lax.rsqrt(var + LN_EPS)


def _ada_kernel(c_ref, w_ref, b_ref, o_ref):
    s = _silu(c_ref[...]).astype(BF16)
    o_ref[...] = jnp.dot(s, w_ref[...].astype(BF16), preferred_element_type=F32) + b_ref[...]


def _ada(cvec, w_ada, b_ada):
    rows, d = cvec.shape
    n = w_ada.shape[1]
    tn = 768
    return pl.pallas_call(
        _ada_kernel,
        grid=(n // tn,),
        in_specs=[pl.BlockSpec((rows, d), lambda j: (0, 0)),
                  pl.BlockSpec((d, tn), lambda j: (0, j)),
                  pl.BlockSpec((1, tn), lambda j: (0, j))],
        out_specs=pl.BlockSpec((rows, tn), lambda j: (0, j)),
        out_shape=jax.ShapeDtypeStruct((rows, n), F32),
        compiler_params=pltpu.CompilerParams(dimension_semantics=("arbitrary",),
                                             vmem_limit_bytes=VMEM_LIMIT_BYTES),
        name="ada",
    )(cvec, w_ada, b_ada)


def _rope(x, cos, sin):
    half = HEAD_DIM // 4
    first_half = lax.broadcasted_iota(jnp.int32, (1, LANES), 1) % (2 * half) < half
    outs = []
    for c in range(x.shape[1] // LANES):
        xc = x[:, c * LANES:(c + 1) * LANES]
        partner = jnp.where(first_half, pltpu.roll(xc, LANES - half, axis=1), pltpu.roll(xc, half, axis=1))
        outs.append(xc * cos + partner * sin)
    return jnp.concatenate(outs, axis=1)


def _inproj_kernel(kinds, has_rope, *refs):
    n_blk = len(kinds)
    x_ref, mod_ref = refs[:2]
    w_refs = refs[2:2 + n_blk]
    b_refs = refs[2 + n_blk:2 + 2 * n_blk]
    n_in = 2 + 2 * n_blk + (2 if has_rope else 0)
    cos_ref, sin_ref = refs[n_in - 2:n_in] if has_rope else (None, None)
    out_refs = refs[n_in:]
    d = x_ref.shape[1]

    mod = mod_ref[0]
    tm = x_ref.shape[0]

    def modulated(rows):
        return (_layer_norm(x_ref[rows, :]) * (1.0 + mod[:, d:2 * d]) + mod[:, :d]).astype(BF16)

    h_halves = [modulated(slice(0, tm // 2)), modulated(slice(tm // 2, tm))]
    h = jnp.concatenate(h_halves, axis=0)

    o = 0
    for n, (kind, w_ref, b_ref) in enumerate(zip(kinds, w_refs, b_refs)):
        w = w_ref[...].astype(BF16)
        if n == 0:
            acc = jnp.concatenate([jnp.dot(hh, w, preferred_element_type=F32) for hh in h_halves], axis=0)
        else:
            acc = jnp.dot(h, w, preferred_element_type=F32)
        acc = acc + b_ref[...]
        if kind == "q":
            qr_ref, q_ref = out_refs[o:o + 2]
            o += 2
            qs = acc * SCORE_SCALE
            q_ref[...] = qs.astype(q_ref.dtype)
            qr_ref[...] = _rope(qs, cos_ref[...], sin_ref[...]).astype(qr_ref.dtype)
        else:
            val = _rope(acc, cos_ref[...], sin_ref[...]) if kind == "k" else acc
            out_refs[o][...] = val.astype(out_refs[o].dtype)
            o += 1


def _inproj(x2d, mod3, mod_rows_per_tile, w_bf16, b_in, col_blocks, kinds, out_dtypes, rope, tm):
    t, d = x2d.shape
    has_rope = rope is not None
    resident = pl.Buffered(1)
    in_specs = [
        pl.BlockSpec((tm, d), lambda i: (i, 0)),
        pl.BlockSpec((1, 1, mod3.shape[2]), lambda i: (mod_rows_per_tile(i), 0, 0)),
    ]
    in_specs += [pl.BlockSpec((d, d), lambda i, c=c: (0, c), pipeline_mode=resident) for c in col_blocks]
    in_specs += [pl.BlockSpec((1, d), lambda i, c=c: (0, c), pipeline_mode=resident) for c in col_blocks]
    args = [x2d, mod3] + [w_bf16] * len(col_blocks) + [b_in] * len(col_blocks)
    if has_rope:
        tiles_per_seq = rope[0].shape[0] // tm
        for tab in rope:
            in_specs.append(pl.BlockSpec((tm, LANES), lambda i: (i % tiles_per_seq, 0)))
            args.append(tab)
    out_specs = [pl.BlockSpec((tm, d), lambda i: (i, 0)) for _ in out_dtypes]
    out_shape = [jax.ShapeDtypeStruct((t, d), dt) for dt in out_dtypes]
    return pl.pallas_call(
        functools.partial(_inproj_kernel, tuple(kinds), has_rope),
        grid=(t // tm,),
        in_specs=in_specs,
        out_specs=out_specs,
        out_shape=out_shape,
        compiler_params=pltpu.CompilerParams(dimension_semantics=("arbitrary",),
                                             vmem_limit_bytes=VMEM_LIMIT_BYTES),
        name="inproj_rope" if has_rope else "inproj_ctx",
    )(*args)


def _rope_tables(seq):
    half = HEAD_DIM // 4
    inv_freq = ROPE_BASE ** (-np.arange(half, dtype=np.float64) / half)
    t = np.arange(seq)
    row, col = t // GRID_W, t % GRID_W
    lane = np.arange(LANES)
    pos = np.where((lane % HEAD_DIM < HEAD_DIM // 2)[None, :], row[:, None], col[:, None]).astype(np.float64)
    ang = pos * inv_freq[lane % half][None, :]
    sign = np.where(lane % (2 * half) < half, -1.0, 1.0)[None, :]
    return jnp.asarray(np.cos(ang), F32), jnp.asarray(np.sin(ang) * sign, F32)


def _window_row_start(r, n_rows):
    return jnp.clip(r - NA_ROWS // 2, 0, n_rows - NA_ROWS)


def _bias_kernel(rpb_ref, o_ref):
    lane = lax.broadcasted_iota(jnp.int32, (GRID_W, LANES), 1)
    qc = lax.broadcasted_iota(jnp.int32, (GRID_W, LANES), 0)
    kc = lane % GRID_W
    cs = jnp.clip(qc - NA_COLS // 2, 0, GRID_W - NA_COLS)
    col_ok = (kc >= cs) & (kc < cs + NA_COLS)
    low_half = lane < GRID_W

    for h in range(o_ref.shape[0]):
        table = rpb_ref[h] * LOG2_E

        @functools.cache
        def shifted(dr, half, table=table):
            row = jnp.broadcast_to(table[dr:dr + 1, :], (GRID_W, LANES))
            shift = (LANES - (NA_COLS - 1) + GRID_W * half) % LANES
            return pltpu.roll(row, shift, axis=1, stride=1, stride_axis=0)

        for i in range(o_ref.shape[1]):
            o_ref[h, i] = jnp.where(col_ok, jnp.where(low_half, shifted(i, 0), shifted(i + 1, 1)), NEG)


def _bias_tables(rpb):
    n_heads, n_dr, n_dc = rpb.shape
    assert n_dr == 2 * NA_ROWS - 1 and n_dc == 2 * NA_COLS - 1
    rpb_pad = jnp.pad(rpb, ((0, 0), (0, 16 - n_dr), (0, LANES - n_dc)))
    n_tiles = n_dr - 1
    heads_per_step = 8
    assert n_heads % heads_per_step == 0
    return pl.pallas_call(
        _bias_kernel,
        grid=(n_heads // heads_per_step,),
        in_specs=[pl.BlockSpec((heads_per_step, 16, LANES), lambda h: (h, 0, 0))],
        out_specs=pl.BlockSpec((heads_per_step, n_tiles, GRID_W, LANES), lambda h: (h, 0, 0, 0)),
        out_shape=jax.ShapeDtypeStruct((n_heads, n_tiles, GRID_W, LANES), F32),
        compiler_params=pltpu.CompilerParams(dimension_semantics=("arbitrary",),
                                             vmem_limit_bytes=VMEM_LIMIT_BYTES),
        name="bias_tables",
    )(rpb_pad)


def _union_row_start(i, n_rows):
    return jnp.minimum(_window_row_start(i * STEP_ROWS, n_rows), n_rows - UNION_ROWS)


def _mix_kernel(seq, x_ref, mod_ref, u_ref, up_ref, un_ref, zp_ref, za_ref, qr_ref, q_ref, k_hbm, v_hbm, *rest):
    (bias_ref, kc_ref, vc_ref, wp_ref, ps_ref, wo_ref, bo_ref, g_ref, be_ref,
     o_ref, cat_ref, k_buf, v_buf, win_sem) = rest
    d = x_ref.shape[1]
    d_pool = ps_ref.shape[1]
    gdim = d_pool // len(POOL_WINDOWS)
    n_heads = (cat_ref.shape[1] - d_pool) // HEAD_DIM
    n_rows = seq // GRID_W
    i = pl.program_id(1)
    n_blk = pl.num_programs(1)

    step = pl.program_id(0) * n_blk + i
    n_steps = pl.num_programs(0) * n_blk
    slot = step % 2

    def window_copies(s, into):
        tok0 = pl.multiple_of((s // n_blk) * seq + _union_row_start(s % n_blk, n_rows) * GRID_W, GRID_W)
        return [pltpu.make_async_copy(hbm.at[pl.ds(tok0, UNION_ROWS * GRID_W)], buf.at[into], win_sem.at[n, into])
                for n, (hbm, buf) in enumerate(((k_hbm, k_buf), (v_hbm, v_buf)))]

    @pl.when(step == 0)
    def _():
        for cp in window_copies(step, slot):
            cp.start()

    for cp in window_copies(step, slot):
        cp.wait()

    @pl.when(step + 1 < n_steps)
    def _():
        for cp in window_copies(step + 1, 1 - slot):
            cp.start()

    win_row = [_window_row_start(i * STEP_ROWS + r, n_rows) for r in range(STEP_ROWS)]
    win_off = [pl.multiple_of((win_row[r] - _union_row_start(i, n_rows)) * GRID_W, GRID_W) for r in range(STEP_ROWS)]
    win_dr = [win_row[r] - (i * STEP_ROWS + r) + NA_ROWS - 1 for r in range(STEP_ROWS)]

    n_ext = QB + 2 * POOL_HALO

    def block_rows(blk):
        return slice(blk * QB, (blk + 1) * QB)

    def shift_sum(s, k):
        return pltpu.roll(s, k, axis=0) + pltpu.roll(s, n_ext - k, axis=0)

    def pool_group(blk, g):
        w = POOL_WINDOWS[g]
        cols = slice(g * gdim, (g + 1) * gdim)
        rows = block_rows(blk)
        before = (jnp.where(i > 0, up_ref[:, cols], 0.0) if blk == 0
                  else u_ref[blk * QB - POOL_HALO:blk * QB, cols])
        after = (jnp.where(i < n_blk - 1, un_ref[:, cols], 0.0) if blk == STEP_BLOCKS - 1
                 else u_ref[(blk + 1) * QB:(blk + 1) * QB + POOL_HALO, cols])
        u = u_ref[rows, cols]
        s = jnp.concatenate([before, u, after], axis=0)
        s = s + pltpu.roll(s, 1, axis=0)
        k = 1
        while 2 * k < w:
            s = shift_sum(s, k)
            k *= 2
        s = s[POOL_HALO:POOL_HALO + QB]
        t = i * ST + blk * QB + lax.broadcasted_iota(jnp.int32, (QB, 1), 0)
        lo = jnp.clip(t - w // 2, 0, seq)
        hi = jnp.clip(t - w // 2 + w, 0, seq)
        pooled = s / (hi - lo).astype(F32) - u
        y = jnp.dot(pooled.astype(BF16), wp_ref[g], preferred_element_type=F32)
        y = y * ps_ref[:, cols] * _silu(zp_ref[rows, cols].astype(F32))
        cat_ref[rows, cols] = y.astype(BF16)

    nt_dims = (((1,), (1,)), ((), ()))
    lane = lax.broadcasted_iota(jnp.int32, (1, LANES), 1)
    first_head = lane < HEAD_DIM
    ones_win = jnp.ones((WIN, LANES), BF16)
    ones_ctx = jnp.ones((kc_ref.shape[0], LANES), BF16)
    stacked = 2 * GRID_W

    def chunk(p):
        return slice(p * LANES, (p + 1) * LANES)

    def stack_heads(ref, blk, p):
        qp = ref[block_rows(blk), chunk(p)]
        zero = jnp.zeros_like(qp)
        first, second = jnp.where(first_head, qp, zero), jnp.where(first_head, zero, qp)
        pieces = []
        for a in range(QUERY_ROWS):
            pieces += [first[a * GRID_W:(a + 1) * GRID_W], second[a * GRID_W:(a + 1) * GRID_W]]
        return jnp.concatenate(pieces, axis=0)

    def lane_tiles(a):
        return [a[:, c * LANES:(c + 1) * LANES] for c in range(a.shape[1] // LANES)]

    def scores(blk, p):
        q_rot = stack_heads(qr_ref, blk, p)
        s_loc = []
        for a in range(QUERY_ROWS):
            r = blk * QUERY_ROWS + a
            k_win = k_buf[slot, pl.ds(win_off[r], WIN), chunk(p)]
            bias = jnp.concatenate(
                [jnp.concatenate([bias_ref[2 * p + h, win_dr[r] + jr] for jr in range(0, NA_ROWS, 2)], axis=1)
                 for h in range(2)], axis=0)
            s_loc.append(lax.dot_general(q_rot[a * stacked:(a + 1) * stacked], k_win, nt_dims,
                                         preferred_element_type=F32) + bias)
        s_ctx = lax.dot_general(stack_heads(q_ref, blk, p), kc_ref[:, chunk(p)], nt_dims,
                                preferred_element_type=F32)
        return jnp.concatenate(s_loc, axis=0), s_ctx

    def softmax_numerators(s_loc, s_ctx):
        m = jnp.max(functools.reduce(jnp.maximum, lane_tiles(s_loc) + lane_tiles(s_ctx)), axis=1, keepdims=True)
        return jnp.concatenate([jnp.exp2(s_loc - m), jnp.exp2(s_ctx - m)], axis=1).astype(BF16)

    def weighted_values(blk, p, probs):
        vc_aug = jnp.concatenate([vc_ref[:, chunk(p)], ones_ctx], axis=1)
        outs = []
        for a in range(QUERY_ROWS):
            v_win = v_buf[slot, pl.ds(win_off[blk * QUERY_ROWS + a], WIN), chunk(p)]
            v_aug = jnp.concatenate([jnp.concatenate([v_win, ones_win], axis=1), vc_aug], axis=0)
            outs.append(jnp.dot(probs[a * stacked:(a + 1) * stacked], v_aug, preferred_element_type=F32))
        return outs

    def emit(blk, p, outs):
        num = jnp.concatenate([jnp.where(first_head, o[:GRID_W, :LANES], o[GRID_W:, :LANES]) for o in outs], axis=0)
        den = jnp.concatenate([jnp.where(first_head, o[:GRID_W, LANES:], o[GRID_W:, LANES:]) for o in outs], axis=0)
        out = num / den * _silu(za_ref[block_rows(blk), chunk(p)].astype(F32))
        cat_ref[block_rows(blk), d_pool + p * LANES:d_pool + (p + 1) * LANES] = out.astype(BF16)

    def finish(first_blk):
        rows = slice(first_blk * QB, (first_blk + OPROJ_BLOCKS) * QB)
        y = jnp.dot(cat_ref[rows, :], wo_ref[...], preferred_element_type=F32) + bo_ref[...]
        z = DEEPNORM_ALPHA * x_ref[rows, :] + mod_ref[0][:, 2 * d:3 * d] * y
        o_ref[rows, :] = _layer_norm(z) * g_ref[...] + be_ref[...]

    n_pairs = n_heads // 2
    items = [(blk, p) for blk in range(STEP_BLOCKS) for p in range(n_pairs)]
    s_next = scores(*items[0])
    for n, (blk, p) in enumerate(items):
        s_cur = s_next
        if n + 1 < len(items):
            s_next = scores(*items[n + 1])
        if POOL_FIRST_PAIR <= p < POOL_FIRST_PAIR + len(POOL_WINDOWS):
            pool_group(blk, p - POOL_FIRST_PAIR)
        emit(blk, p, weighted_values(blk, p, softmax_numerators(*s_cur)))
        if blk > 0 and blk % OPROJ_BLOCKS == 0 and p == 0:
            finish(blk - OPROJ_BLOCKS)
    finish(STEP_BLOCKS - OPROJ_BLOCKS)


def _mix(x2d, mod3, u, zp, za, qr, q, kr, v, kc, vc, bias, w_pool, pool_scale, w_out, b_out, ln_g, ln_b,
         batch, seq, ctx_len):
    t, d = x2d.shape
    n_blk = seq // ST
    halo_per_blk = ST // POOL_HALO
    n_halo = t // POOL_HALO
    d_mix = w_out.shape[0]
    n_rows = seq // GRID_W
    single = pl.Buffered(1)

    def tok(b, i):
        return (b * n_blk + i, 0)

    full =lambda shape: pl.BlockSpec(shape, lambda b, i: (0,) * len(shape), pipeline_mode=single)
    in_specs = [
        pl.BlockSpec((ST, d), tok),
        pl.BlockSpec((1, 1, mod3.shape[2]), lambda b, i: (b, 0, 0)),
        pl.BlockSpec((ST, d), tok),
        pl.BlockSpec((POOL_HALO, d),
                     lambda b, i: (jnp.maximum((b * n_blk + i) * halo_per_blk - 1, 0), 0)),
        pl.BlockSpec((POOL_HALO, d),
                     lambda b, i: (jnp.minimum((b * n_blk + i + 1) * halo_per_blk, n_halo - 1), 0)),
        pl.BlockSpec((ST, d), tok),
        pl.BlockSpec((ST, d), tok),
        pl.BlockSpec((ST, d), tok),
        pl.BlockSpec((ST, d), tok),
    ]
    in_specs += [pl.BlockSpec(memory_space=pl.ANY)] * 2
    in_specs += [
        full(bias.shape),
        pl.BlockSpec((ctx_len, d), lambda b, i: (b, 0)),
        pl.BlockSpec((ctx_len, d), lambda b, i: (b, 0)),
        full(w_pool.shape), full(pool_scale.shape), full(w_out.shape),
        full(b_out.shape), full(ln_g.shape), full(ln_b.shape),
    ]
    args = ([x2d, mod3, u, u, u, zp, za, qr, q, kr, v]
            + [bias, kc, vc, w_pool, pool_scale, w_out, b_out, ln_g, ln_b])
    return pl.pallas_call(
        functools.partial(_mix_kernel, seq),
        grid=(batch, n_blk),
        in_specs=in_specs,
        out_specs=pl.BlockSpec((ST, d), tok),
        out_shape=jax.ShapeDtypeStruct((t, d), F32),
        scratch_shapes=[pltpu.VMEM((ST, d_mix), BF16),
                        pltpu.VMEM((2, UNION_ROWS * GRID_W, d), BF16),
                        pltpu.VMEM((2, UNION_ROWS * GRID_W, d), BF16),
                        pltpu.SemaphoreType.DMA((2, 2))],
        compiler_params=pltpu.CompilerParams(dimension_semantics=("arbitrary", "arbitrary"),
                                             vmem_limit_bytes=VMEM_LIMIT_BYTES),
        name="mix",
    )(*args)


def kernel(x, c, ctx, c_ctx, w_ada, b_ada, w_in, b_in, w_pool, pool_scale, rpb, w_out, b_out, ln_g, ln_b):
    batch, seq, d = x.shape
    ctx_len = ctx.shape[1]
    assert w_ada.shape[0] == DEPTH == 1
    assert seq % GRID_W == 0 and seq % INPROJ_TM == 0 and (batch * ctx_len) % INPROJ_TM == 0
    assert seq // GRID_W >= UNION_ROWS and seq % ST == 0
    d_pool = pool_scale.shape[1]
    assert d_pool == d and w_in.shape[2] == 6 * d and rpb.shape[1] * HEAD_DIM == d

    cvec = jnp.zeros((8, d), F32).at[:batch].set(c).at[batch].set(c_ctx)
    mod3 = _ada(cvec, w_ada[0], b_ada[0][None, :]).reshape(8, 1, 3 * d)

    w_in_b = w_in[0]
    b_in2 = b_in[0][None, :]
    x2d = x.reshape(batch * seq, d)
    tiles_per_seq = seq // INPROJ_TM
    u, zp, qr, q, kr, v, za = _inproj(
        x2d, mod3, lambda i: i // tiles_per_seq, w_in_b, b_in2, range(6),
        ("plain", "plain", "q", "k", "plain", "plain"), (F32, BF16, BF16, BF16, BF16, BF16, BF16),
        _rope_tables(seq), INPROJ_TM)
    kc, vc = _inproj(
        ctx.reshape(batch * ctx_len, d), mod3, lambda i: batch, w_in_b, b_in2, (3, 4),
        ("plain", "plain"), (BF16, BF16), None, INPROJ_TM)

    bias = _bias_tables(rpb[0])
    out = _mix(x2d, mod3, u, zp, za, qr, q, kr, v, kc, vc, bias,
               w_pool[0].astype(BF16), pool_scale, w_out[0].astype(BF16), b_out, ln_g, ln_b,
               batch, seq, ctx_len)
    return out.reshape(batch, seq, d)
```

```python
import functools

import numpy as np
import jax
import jax.numpy as jnp
from jax import lax
from jax.experimental import pallas as pl
from jax.experimental.pallas import tpu as pltpu

GRID_W = 64
POOL_WINDOWS = (2, 4, 8, 16)
HEAD_DIM = 64
NA_ROWS = 8
NA_COLS = 16
ROPE_BASE = 10000.0
LN_EPS = 1e-6
LOG2_E = float(np.log2(np.e))
SCORE_SCALE = HEAD_DIM ** -0.5 * LOG2_E
DEPTH = 1
DEEPNORM_ALPHA = (2.0 * DEPTH) ** 0.25

LANES = 128
VMEM_LIMIT_BYTES = 52 * 1024 * 1024

QUERY_ROWS = 2
QB = QUERY_ROWS * GRID_W
WIN = NA_ROWS * GRID_W
STEP_BLOCKS = 4
STEP_ROWS = STEP_BLOCKS * QUERY_ROWS
ST = STEP_BLOCKS * QB
OPROJ_BLOCKS = 4
POOL_FIRST_PAIR = 2
UNION_ROWS = NA_ROWS + STEP_ROWS - 1
POOL_HALO = 8
NEG = -1e30
INPROJ_TM = 512
F32 = jnp.float32
BF16 = jnp.bfloat16


def _silu(x):
    return x * jax.nn.sigmoid(x)


def _layer_norm(x):
    mu = jnp.mean(x, axis=-1, keepdims=True)
    xc = x - mu
    var = jnp.mean(xc * xc, axis=-1, keepdims=True)
    return xc * lax.rsqrt(var + LN_EPS)


def _ada_kernel(c_ref, w_ref, b_ref, o_ref):
    s = _silu(c_ref[...]).astype(BF16)
    o_ref[...] = jnp.dot(s, w_ref[...].astype(BF16), preferred_element_type=F32) + b_ref[...]


def _ada(cvec, w_ada, b_ada):
    rows, d = cvec.shape
    n = w_ada.shape[1]
    tn = 768
    return pl.pallas_call(
        _ada_kernel,
        grid=(n // tn,),
        in_specs=[pl.BlockSpec((rows, d), lambda j: (0, 0)),
                  pl.BlockSpec((d, tn), lambda j: (0, j)),
                  pl.BlockSpec((1, tn), lambda j: (0, j))],
        out_specs=pl.BlockSpec((rows, tn), lambda j: (0, j)),
        out_shape=jax.ShapeDtypeStruct((rows, n), F32),
        compiler_params=pltpu.CompilerParams(dimension_semantics=("arbitrary",),
                                             vmem_limit_bytes=VMEM_LIMIT_BYTES),
        name="ada",
    )(cvec, w_ada, b_ada)


def _rope(x, cos, sin):
    half = HEAD_DIM // 4
    first_half = lax.broadcasted_iota(jnp.int32, (1, LANES), 1) % (2 * half) < half
    outs = []
    for c in range(x.shape[1] // LANES):
        xc = x[:, c * LANES:(c + 1) * LANES]
        partner = jnp.where(first_half, pltpu.roll(xc, LANES - half, axis=1), pltpu.roll(xc, half, axis=1))
        outs.append(xc * cos + partner * sin)
    return jnp.concatenate(outs, axis=1)


def _inproj_kernel(kinds, has_rope, *refs):
    n_blk = len(kinds)
    x_ref, mod_ref = refs[:2]
    w_refs = refs[2:2 + n_blk]
    b_refs = refs[2 + n_blk:2 + 2 * n_blk]
    n_in = 2 + 2 * n_blk + (2 if has_rope else 0)
    cos_ref, sin_ref = refs[n_in - 2:n_in] if has_rope else (None, None)
    out_refs = refs[n_in:]
    d = x_ref.shape[1]

    mod = mod_ref[0]
    tm = x_ref.shape[0]

    def modulated(rows):
        return (_layer_norm(x_ref[rows, :]) * (1.0 + mod[:, d:2 * d]) + mod[:, :d]).astype(BF16)

    h_halves = [modulated(slice(0, tm // 2)), modulated(slice(tm // 2, tm))]
    h = jnp.concatenate(h_halves, axis=0)

    o = 0
    for n, (kind, w_ref, b_ref) in enumerate(zip(kinds, w_refs, b_refs)):
        w = w_ref[...].astype(BF16)
        if n == 0:
            acc = jnp.concatenate([jnp.dot(hh, w, preferred_element_type=F32) for hh in h_halves], axis=0)
        else:
            acc = jnp.dot(h, w, preferred_element_type=F32)
        acc = acc + b_ref[...]
        if kind == "q":
            qr_ref, q_ref = out_refs[o:o + 2]
            o += 2
            qs = acc * SCORE_SCALE
            q_ref[...] = qs.astype(q_ref.dtype)
            qr_ref[...] = _rope(qs, cos_ref[...], sin_ref[...]).astype(qr_ref.dtype)
        else:
            val = _rope(acc, cos_ref[...], sin_ref[...]) if kind == "k" else acc
            out_refs[o][...] = val.astype(out_refs[o].dtype)
            o += 1


def _inproj(x2d, mod3, mod_rows_per_tile, w_bf16, b_in, col_blocks, kinds, out_dtypes, rope, tm):
    t, d = x2d.shape
    has_rope = rope is not None
    resident = pl.Buffered(1)
    in_specs = [
        pl.BlockSpec((tm, d), lambda i: (i, 0)),
        pl.BlockSpec((1, 1, mod3.shape[2]), lambda i: (mod_rows_per_tile(i), 0, 0)),
    ]
    in_specs += [pl.BlockSpec((d, d), lambda i, c=c: (0, c), pipeline_mode=resident) for c in col_blocks]
    in_specs += [pl.BlockSpec((1, d), lambda i, c=c: (0, c), pipeline_mode=resident) for c in col_blocks]
    args = [x2d, mod3] + [w_bf16] * len(col_blocks) + [b_in] * len(col_blocks)
    if has_rope:
        tiles_per_seq = rope[0].shape[0] // tm
        for tab in rope:
            in_specs.append(pl.BlockSpec((tm, LANES), lambda i: (i % tiles_per_seq, 0)))
            args.append(tab)
    out_specs = [pl.BlockSpec((tm, d), lambda i: (i, 0)) for _ in out_dtypes]
    out_shape = [jax.ShapeDtypeStruct((t, d), dt) for dt in out_dtypes]
    return pl.pallas_call(
        functools.partial(_inproj_kernel, tuple(kinds), has_rope),
        grid=(t // tm,),
        in_specs=in_specs,
        out_specs=out_specs,
        out_shape=out_shape,
        compiler_params=pltpu.CompilerParams(dimension_semantics=("arbitrary",),
                                             vmem_limit_bytes=VMEM_LIMIT_BYTES),
        name="inproj_rope" if has_rope else "inproj_ctx",
    )(*args)


def _rope_tables(seq):
    half = HEAD_DIM // 4
    inv_freq = ROPE_BASE ** (-np.arange(half, dtype=np.float64) / half)
    t = np.arange(seq)
    row, col = t // GRID_W, t % GRID_W
    lane = np.arange(LANES)
    pos = np.where((lane % HEAD_DIM < HEAD_DIM // 2)[None, :], row[:, None], col[:, None]).astype(np.float64)
    ang = pos * inv_freq[lane % half][None, :]
    sign = np.where(lane % (2 * half) < half, -1.0, 1.0)[None, :]
    return jnp.asarray(np.cos(ang), F32), jnp.asarray(np.sin(ang) * sign, F32)


def _window_row_start(r, n_rows):
    return jnp.clip(r - NA_ROWS // 2, 0, n_rows - NA_ROWS)


def _bias_kernel(rpb_ref, o_ref):
    lane = lax.broadcasted_iota(jnp.int32, (GRID_W, LANES), 1)
    qc = lax.broadcasted_iota(jnp.int32, (GRID_W, LANES), 0)
    kc = lane % GRID_W
    cs = jnp.clip(qc - NA_COLS // 2, 0, GRID_W - NA_COLS)
    col_ok = (kc >= cs) & (kc < cs + NA_COLS)
    low_half = lane < GRID_W

    for h in range(o_ref.shape[0]):
        table = rpb_ref[h] * LOG2_E

        @functools.cache
        def shifted(dr, half, table=table):
            row = jnp.broadcast_to(table[dr:dr + 1, :], (GRID_W, LANES))
            shift = (LANES - (NA_COLS - 1) + GRID_W * half) % LANES
            return pltpu.roll(row, shift, axis=1, stride=1, stride_axis=0)

        for i in range(o_ref.shape[1]):
            o_ref[h, i] = jnp.where(col_ok, jnp.where(low_half, shifted(i, 0), shifted(i + 1, 1)), NEG)


def _bias_tables(rpb):
    n_heads, n_dr, n_dc = rpb.shape
    assert n_dr == 2 * NA_ROWS - 1 and n_dc == 2 * NA_COLS - 1
    rpb_pad = jnp.pad(rpb, ((0, 0), (0, 16 - n_dr), (0, LANES - n_dc)))
    n_tiles = n_dr - 1
    heads_per_step = 8
    assert n_heads % heads_per_step == 0
    return pl.pallas_call(
        _bias_kernel,
        grid=(n_heads // heads_per_step,),
        in_specs=[pl.BlockSpec((heads_per_step, 16, LANES), lambda h: (h, 0, 0))],
        out_specs=pl.BlockSpec((heads_per_step, n_tiles, GRID_W, LANES), lambda h: (h, 0, 0, 0)),
        out_shape=jax.ShapeDtypeStruct((n_heads, n_tiles, GRID_W, LANES), F32),
        compiler_params=pltpu.CompilerParams(dimension_semantics=("arbitrary",),
                                             vmem_limit_bytes=VMEM_LIMIT_BYTES),
        name="bias_tables",
    )(rpb_pad)


def _union_row_start(i, n_rows):
    return jnp.minimum(_window_row_start(i * STEP_ROWS, n_rows), n_rows - UNION_ROWS)


def _mix_kernel(seq, x_ref, mod_ref, u_ref, up_ref, un_ref, zp_ref, za_ref, qr_ref, q_ref, k_hbm, v_hbm, *rest):
    (bias_ref, kc_ref, vc_ref, wp_ref, ps_ref, wo_ref, bo_ref, g_ref, be_ref,
     o_ref, cat_ref, k_buf, v_buf, win_sem) = rest
    d = x_ref.shape[1]
    d_pool = ps_ref.shape[1]
    gdim = d_pool // len(POOL_WINDOWS)
    n_heads = (cat_ref.shape[1] - d_pool) // HEAD_DIM
    n_rows = seq // GRID_W
    i = pl.program_id(1)
    n_blk = pl.num_programs(1)

    step = pl.program_id(0) * n_blk + i
    n_steps = pl.num_programs(0) * n_blk
    slot = step % 2

    def window_copies(s, into):
        tok0 = pl.multiple_of((s // n_blk) * seq + _union_row_start(s % n_blk, n_rows) * GRID_W, GRID_W)
        return [pltpu.make_async_copy(hbm.at[pl.ds(tok0, UNION_ROWS * GRID_W)], buf.at[into], win_sem.at[n, into])
                for n, (hbm, buf) in enumerate(((k_hbm, k_buf), (v_hbm, v_buf)))]

    @pl.when(step == 0)
    def _():
        for cp in window_copies(step, slot):
            cp.start()

    for cp in window_copies(step, slot):
        cp.wait()

    @pl.when(step + 1 < n_steps)
    def _():
        for cp in window_copies(step + 1, 1 - slot):
            cp.start()

    win_row = [_window_row_start(i * STEP_ROWS + r, n_rows) for r in range(STEP_ROWS)]
    win_off = [pl.multiple_of((win_row[r] - _union_row_start(i, n_rows)) * GRID_W, GRID_W) for r in range(STEP_ROWS)]
    win_dr = [win_row[r] - (i * STEP_ROWS + r) + NA_ROWS - 1 for r in range(STEP_ROWS)]

    n_ext = QB + 2 * POOL_HALO

    def block_rows(blk):
        return slice(blk * QB, (blk + 1) * QB)

    def shift_sum(s, k):
        return pltpu.roll(s, k, axis=0) + pltpu.roll(s, n_ext - k, axis=0)

    def pool_group(blk, g):
        w = POOL_WINDOWS[g]
        cols = slice(g * gdim, (g + 1) * gdim)
        rows = block_rows(blk)
        before = (jnp.where(i > 0, up_ref[:, cols], 0.0) if blk == 0
                  else u_ref[blk * QB - POOL_HALO:blk * QB, cols])
        after = (jnp.where(i < n_blk - 1, un_ref[:, cols], 0.0) if blk == STEP_BLOCKS - 1
                 else u_ref[(blk + 1) * QB:(blk + 1) * QB + POOL_HALO, cols])
        u = u_ref[rows, cols]
        s = jnp.concatenate([before, u, after], axis=0)
        s = s + pltpu.roll(s, 1, axis=0)
        k = 1
        while 2 * k < w:
            s = shift_sum(s, k)
            k *= 2
        s = s[POOL_HALO:POOL_HALO + QB]
        t = i * ST + blk * QB + lax.broadcasted_iota(jnp.int32, (QB, 1), 0)
        lo = jnp.clip(t - w // 2, 0, seq)
        hi = jnp.clip(t - w // 2 + w, 0, seq)
        pooled = s / (hi - lo).astype(F32) - u
        y = jnp.dot(pooled.astype(BF16), wp_ref[g], preferred_element_type=F32)
        y = y * ps_ref[:, cols] * _silu(zp_ref[rows, cols].astype(F32))
        cat_ref[rows, cols] = y.astype(BF16)

    nt_dims = (((1,), (1,)), ((), ()))
    lane = lax.broadcasted_iota(jnp.int32, (1, LANES), 1)
    first_head = lane < HEAD_DIM
    ones_win = jnp.ones((WIN, LANES), BF16)
    ones_ctx = jnp.ones((kc_ref.shape[0], LANES), BF16)
    stacked = 2 * GRID_W

    def chunk(p):
        return slice(p * LANES, (p + 1) * LANES)

    def stack_heads(ref, blk, p):
        qp = ref[block_rows(blk), chunk(p)]
        zero = jnp.zeros_like(qp)
        first, second = jnp.where(first_head, qp, zero), jnp.where(first_head, zero, qp)
        pieces = []
        for a in range(QUERY_ROWS):
            pieces += [first[a * GRID_W:(a + 1) * GRID_W], second[a * GRID_W:(a + 1) * GRID_W]]
        return jnp.concatenate(pieces, axis=0)

    def lane_tiles(a):
        return [a[:, c * LANES:(c + 1) * LANES] for c in range(a.shape[1] // LANES)]

    def scores(blk, p):
        q_rot = stack_heads(qr_ref, blk, p)
        s_loc = []
        for a in range(QUERY_ROWS):
            r = blk * QUERY_ROWS + a
            k_win = k_buf[slot, pl.ds(win_off[r], WIN), chunk(p)]
            bias = jnp.concatenate(
                [jnp.concatenate([bias_ref[2 * p + h, win_dr[r] + jr] for jr in range(0, NA_ROWS, 2)], axis=1)
                 for h in range(2)], axis=0)
            s_loc.append(lax.dot_general(q_rot[a * stacked:(a + 1) * stacked], k_win, nt_dims,
                                         preferred_element_type=F32) + bias)
        s_ctx = lax.dot_general(stack_heads(q_ref, blk, p), kc_ref[:, chunk(p)], nt_dims,
                                preferred_element_type=F32)
        return jnp.concatenate(s_loc, axis=0), s_ctx

    def softmax_numerators(s_loc, s_ctx):
        m = jnp.max(functools.reduce(jnp.maximum, lane_tiles(s_loc) + lane_tiles(s_ctx)), axis=1, keepdims=True)
        return jnp.concatenate([jnp.exp2(s_loc - m), jnp.exp2(s_ctx - m)], axis=1).astype(BF16)

    def weighted_values(blk, p, probs):
        vc_aug = jnp.concatenate([vc_ref[:, chunk(p)], ones_ctx], axis=1)
        outs = []
        for a in range(QUERY_ROWS):
            v_win = v_buf[slot, pl.ds(win_off[blk * QUERY_ROWS + a], WIN), chunk(p)]
            v_aug = jnp.concatenate([jnp.concatenate([v_win, ones_win], axis=1), vc_aug], axis=0)
            outs.append(jnp.dot(probs[a * stacked:(a + 1) * stacked], v_aug, preferred_element_type=F32))
        return outs

    def emit(blk, p, outs):
        num = jnp.concatenate([jnp.where(first_head, o[:GRID_W, :LANES], o[GRID_W:, :LANES]) for o in outs], axis=0)
        den = jnp.concatenate([jnp.where(first_head, o[:GRID_W, LANES:], o[GRID_W:, LANES:]) for o in outs], axis=0)
        out = num / den * _silu(za_ref[block_rows(blk), chunk(p)].astype(F32))
        cat_ref[block_rows(blk), d_pool + p * LANES:d_pool + (p + 1) * LANES] = out.astype(BF16)

    def finish(first_blk):
        rows = slice(first_blk * QB, (first_blk + OPROJ_BLOCKS) * QB)
        y = jnp.dot(cat_ref[rows, :], wo_ref[...], preferred_element_type=F32) + bo_ref[...]
        z = DEEPNORM_ALPHA * x_ref[rows, :] + mod_ref[0][:, 2 * d:3 * d] * y
        o_ref[rows, :] = _layer_norm(z) * g_ref[...] + be_ref[...]

    n_pairs = n_heads // 2
    items = [(blk, p) for blk in range(STEP_BLOCKS) for p in range(n_pairs)]
    s = {n: scores(*items[n]) for n in range(min(2, len(items)))}
    probs = {0: softmax_numerators(*s.pop(0))}
    for n, (blk, p) in enumerate(items):
        if n + 2 < len(items):
            s[n + 2] = scores(*items[n + 2])
        if n + 1 < len(items):
            probs[n + 1] = softmax_numerators(*s.pop(n + 1))
        if POOL_FIRST_PAIR <= p < POOL_FIRST_PAIR + len(POOL_WINDOWS):
            pool_group(blk, p - POOL_FIRST_PAIR)
        emit(blk, p, weighted_values(blk, p, probs.pop(n)))
        if blk > 0 and blk % OPROJ_BLOCKS == 0 and p == 0:
            finish(blk - OPROJ_BLOCKS)
    finish(STEP_BLOCKS - OPROJ_BLOCKS)


def _mix(x2d, mod3, u, zp, za, qr, q, kr, v, kc, vc, bias, w_pool, pool_scale, w_out, b_out, ln_g, ln_b,
         batch, seq, ctx_len):
    t, d = x2d.shape
    n_blk = seq // ST
    halo_per_blk = ST // POOL_HALO
    n_halo = t // POOL_HALO
    d_mix = w_out.shape[0]
    n_rows = seq // GRID_W
    single = pl.Buffered(1)

    def tok(b, i):
        return (b * n_blk + i, 0)

    full =lambda shape: pl.BlockSpec(shape, lambda b, i: (0,) * len(shape), pipeline_mode=single)
    in_specs = [
        pl.BlockSpec((ST, d), tok),
        pl.BlockSpec((1, 1, mod3.shape[2]), lambda b, i: (b, 0, 0)),
        pl.BlockSpec((ST, d), tok),
        pl.BlockSpec((POOL_HALO, d),
                     lambda b, i: (jnp.maximum((b * n_blk + i) * halo_per_blk - 1, 0), 0)),
        pl.BlockSpec((POOL_HALO, d),
                     lambda b, i: (jnp.minimum((b * n_blk + i + 1) * halo_per_blk, n_halo - 1), 0)),
        pl.BlockSpec((ST, d), tok),
        pl.BlockSpec((ST, d), tok),
        pl.BlockSpec((ST, d), tok),
        pl.BlockSpec((ST, d), tok),
    ]
    in_specs += [pl.BlockSpec(memory_space=pl.ANY)] * 2
    in_specs += [
        full(bias.shape),
        pl.BlockSpec((ctx_len, d), lambda b, i: (b, 0)),
        pl.BlockSpec((ctx_len, d), lambda b, i: (b, 0)),
        full(w_pool.shape), full(pool_scale.shape), full(w_out.shape),
        full(b_out.shape), full(ln_g.shape), full(ln_b.shape),
    ]
    args = ([x2d, mod3, u, u, u, zp, za, qr, q, kr, v]
            + [bias, kc, vc, w_pool, pool_scale, w_out, b_out, ln_g, ln_b])
    return pl.pallas_call(
        functools.partial(_mix_kernel, seq),
        grid=(batch, n_blk),
        in_specs=in_specs,
        out_specs=pl.BlockSpec((ST, d), tok),
        out_shape=jax.ShapeDtypeStruct((t, d), F32),
        scratch_shapes=[pltpu.VMEM((ST, d_mix), BF16),
                        pltpu.VMEM((2, UNION_ROWS * GRID_W, d), BF16),
                        pltpu.VMEM((2, UNION_ROWS * GRID_W, d), BF16),
                        pltpu.SemaphoreType.DMA((2, 2))],
        compiler_params=pltpu.CompilerParams(dimension_semantics=("arbitrary", "arbitrary"),
                                             vmem_limit_bytes=VMEM_LIMIT_BYTES),
        name="mix",
    )(*args)


def kernel(x, c, ctx, c_ctx, w_ada, b_ada, w_in, b_in, w_pool, pool_scale, rpb, w_out, b_out, ln_g, ln_b):
    batch, seq, d = x.shape
    ctx_len = ctx.shape[1]
    assert w_ada.shape[0] == DEPTH == 1
    assert seq % GRID_W == 0 and seq % INPROJ_TM == 0 and (batch * ctx_len) % INPROJ_TM == 0
    assert seq // GRID_W >= UNION_ROWS and seq % ST == 0
    d_pool = pool_scale.shape[1]
    assert d_pool == d and w_in.shape[2] == 6 * d and rpb.shape[1] * HEAD_DIM == d

    cvec = jnp.zeros((8, d), F32).at[:batch].set(c).at[batch].set(c_ctx)
    mod3 = _ada(cvec, w_ada[0], b_ada[0][None, :]).reshape(8, 1, 3 * d)

    w_in_b = w_in[0]
    b_in2 = b_in[0][None, :]
    x2d = x.reshape(batch * seq, d)
    tiles_per_seq = seq // INPROJ_TM
    u, zp, qr, q, kr, v, za = _inproj(
        x2d, mod3, lambda i: i // tiles_per_seq, w_in_b, b_in2, range(6),
        ("plain", "plain", "q", "k", "plain", "plain"), (F32, BF16, BF16, BF16, BF16, BF16, BF16),
        _rope_tables(seq), INPROJ_TM)
    kc, vc = _inproj(
        ctx.reshape(batch * ctx_len, d), mod3, lambda i: batch, w_in_b, b_in2, (3, 4),
        ("plain", "plain"), (BF16, BF16), None, INPROJ_TM)

    bias = _bias_tables(rpb[0])
    out = _mix(x2d, mod3, u, zp, za, qr, q, kr, v, kc, vc, bias,
               w_pool[0].astype(BF16), pool_scale, w_out[0].astype(BF16), b_out, ln_g, ln_b,
               batch, seq, ctx_len)
    return out.reshape(batch, seq, d)
```

```python
import functools

import numpy as np
import jax
import jax.numpy as jnp
from jax import lax
from jax.experimental import pallas as pl
from jax.experimental.pallas import tpu as pltpu

GRID_W = 64
POOL_WINDOWS = (2, 4, 8, 16)
HEAD_DIM = 64
NA_ROWS = 8
NA_COLS = 16
ROPE_BASE = 10000.0
LN_EPS = 1e-6
LOG2_E = float(np.log2(np.e))
SCORE_SCALE = HEAD_DIM ** -0.5 * LOG2_E
DEPTH = 1
DEEPNORM_ALPHA = (2.0 * DEPTH) ** 0.25

LANES = 128
VMEM_LIMIT_BYTES = 58 * 1024 * 1024

QUERY_ROWS = 2
QB = QUERY_ROWS * GRID_W
WIN = NA_ROWS * GRID_W
STEP_BLOCKS = 4
STEP_ROWS = STEP_BLOCKS * QUERY_ROWS
ST = STEP_BLOCKS * QB
OPROJ_BLOCKS = 4
POOL_FIRST_PAIR = 2
UNION_ROWS = NA_ROWS + STEP_ROWS - 1
POOL_HALO = 8
NEG = -1e30
INPROJ_TM = 512
F32 = jnp.float32
BF16 = jnp.bfloat16


def _silu(x):
    return x * jax.nn.sigmoid(x)


def _layer_norm(x):
    mu = jnp.mean(x, axis=-1, keepdims=True)
    xc = x - mu
    var = jnp.mean(xc * xc, axis=-1, keepdims=True)
    return xc * lax.rsqrt(var + LN_EPS)


def _ada_kernel(c_ref, w_ref, b_ref, o_ref):
    s = _silu(c_ref[...]).astype(BF16)
    o_ref[...] = jnp.dot(s, w_ref[...].astype(BF16), preferred_element_type=F32) + b_ref[...]


def _ada(cvec, w_ada, b_ada):
    rows, d = cvec.shape
    n = w_ada.shape[1]
    tn = 768
    return pl.pallas_call(
        _ada_kernel,
        grid=(n // tn,),
        in_specs=[pl.BlockSpec((rows, d), lambda j: (0, 0)),
                  pl.BlockSpec((d, tn), lambda j: (0, j)),
                  pl.BlockSpec((1, tn), lambda j: (0, j))],
        out_specs=pl.BlockSpec((rows, tn), lambda j: (0, j)),
        out_shape=jax.ShapeDtypeStruct((rows, n), F32),
        compiler_params=pltpu.CompilerParams(dimension_semantics=("arbitrary",),
                                             vmem_limit_bytes=VMEM_LIMIT_BYTES),
        name="ada",
    )(cvec, w_ada, b_ada)


def _rope(x, cos, sin):
    half = HEAD_DIM // 4
    first_half = lax.broadcasted_iota(jnp.int32, (1, LANES), 1) % (2 * half) < half
    outs = []
    for c in range(x.shape[1] // LANES):
        xc = x[:, c * LANES:(c + 1) * LANES]
        partner = jnp.where(first_half, pltpu.roll(xc, LANES - half, axis=1), pltpu.roll(xc, half, axis=1))
        outs.append(xc * cos + partner * sin)
    return jnp.concatenate(outs, axis=1)


def _inproj_kernel(kinds, has_rope, *refs):
    n_blk = len(kinds)
    x0_ref, mod0_ref, xn_ref, modn_ref = refs[:4]
    w_refs = refs[4:4 + n_blk]
    b_refs = refs[4 + n_blk:4 + 2 * n_blk]
    n_in = 4 + 2 * n_blk + (2 if has_rope else 0)
    cos_ref, sin_ref = refs[n_in - 2:n_in] if has_rope else (None, None)
    out_refs = refs[n_in:-1]
    h_ref = refs[-1]
    d = xn_ref.shape[1]
    i = pl.program_id(0)
    slot = i % 2

    def modulated(x_ref, mod_ref):
        mod = mod_ref[0]
        return (_layer_norm(x_ref[...]) * (1.0 + mod[:, d:2 * d]) + mod[:, :d]).astype(BF16)

    @pl.when(i == 0)
    def _():
        h_ref[0] = modulated(x0_ref, mod0_ref)

    o = 0
    for n, (kind, w_ref, b_ref) in enumerate(zip(kinds, w_refs, b_refs)):
        w = w_ref[...].astype(BF16)
        acc = jnp.dot(h_ref[slot], w, preferred_element_type=F32) + b_ref[...]
        if n == 0:
            h_ref[1 - slot] = modulated(xn_ref, modn_ref)
        if kind == "q":
            qr_ref, q_ref = out_refs[o:o + 2]
            o += 2
            qs = acc * SCORE_SCALE
            q_ref[...] = qs.astype(q_ref.dtype)
            qr_ref[...] = _rope(qs, cos_ref[...], sin_ref[...]).astype(qr_ref.dtype)
        else:
            val = _rope(acc, cos_ref[...], sin_ref[...]) if kind == "k" else acc
            out_refs[o][...] = val.astype(out_refs[o].dtype)
            o += 1


def _inproj(x2d, mod3, mod_rows_per_tile, w_bf16, b_in, col_blocks, kinds, out_dtypes, rope, tm):
    t, d = x2d.shape
    nt = t // tm
    has_rope = rope is not None
    resident = pl.Buffered(1)

    def next_tile(i):
        return jnp.minimum(i + 1, nt - 1)

    mod_block = (1, 1, mod3.shape[2])
    in_specs = [
        pl.BlockSpec((tm, d), lambda i: (0, 0), pipeline_mode=resident),
        pl.BlockSpec(mod_block, lambda i: (mod_rows_per_tile(0), 0, 0), pipeline_mode=resident),
        pl.BlockSpec((tm, d), lambda i: (next_tile(i), 0)),
        pl.BlockSpec(mod_block, lambda i: (mod_rows_per_tile(next_tile(i)), 0, 0)),
    ]
    in_specs += [pl.BlockSpec((d, d), lambda i, c=c: (0, c), pipeline_mode=resident) for c in col_blocks]
    in_specs += [pl.BlockSpec((1, d), lambda i, c=c: (0, c), pipeline_mode=resident) for c in col_blocks]
    args = [x2d, mod3, x2d, mod3] + [w_bf16] * len(col_blocks) + [b_in] * len(col_blocks)
    if has_rope:
        tiles_per_seq = rope[0].shape[0] // tm
        for tab in rope:
            in_specs.append(pl.BlockSpec((tm, LANES), lambda i: (i % tiles_per_seq, 0)))
            args.append(tab)
    out_specs = [pl.BlockSpec((tm, d), lambda i: (i, 0)) for _ in out_dtypes]
    out_shape = [jax.ShapeDtypeStruct((t, d), dt) for dt in out_dtypes]
    return pl.pallas_call(
        functools.partial(_inproj_kernel, tuple(kinds), has_rope),
        grid=(nt,),
        in_specs=in_specs,
        out_specs=out_specs,
        out_shape=out_shape,
        scratch_shapes=[pltpu.VMEM((2, tm, d), BF16)],
        compiler_params=pltpu.CompilerParams(dimension_semantics=("arbitrary",),
                                             vmem_limit_bytes=VMEM_LIMIT_BYTES),
        name="inproj_rope" if has_rope else "inproj_ctx",
    )(*args)


def _rope_tables(seq):
    half = HEAD_DIM // 4
    inv_freq = ROPE_BASE ** (-np.arange(half, dtype=np.float64) / half)
    t = np.arange(seq)
    row, col = t // GRID_W, t % GRID_W
    lane = np.arange(LANES)
    pos = np.where((lane % HEAD_DIM < HEAD_DIM // 2)[None, :], row[:, None], col[:, None]).astype(np.float64)
    ang = pos * inv_freq[lane % half][None, :]
    sign = np.where(lane % (2 * half) < half, -1.0, 1.0)[None, :]
    return jnp.asarray(np.cos(ang), F32), jnp.asarray(np.sin(ang) * sign, F32)


def _window_row_start(r, n_rows):
    return jnp.clip(r - NA_ROWS // 2, 0, n_rows - NA_ROWS)


def _bias_kernel(rpb_ref, o_ref):
    lane = lax.broadcasted_iota(jnp.int32, (GRID_W, LANES), 1)
    qc = lax.broadcasted_iota(jnp.int32, (GRID_W, LANES), 0)
    kc = lane % GRID_W
    cs = jnp.clip(qc - NA_COLS // 2, 0, GRID_W - NA_COLS)
    col_ok = (kc >= cs) & (kc < cs + NA_COLS)
    low_half = lane < GRID_W

    for h in range(o_ref.shape[0]):
        table = rpb_ref[h] * LOG2_E

        @functools.cache
        def shifted(dr, half, table=table):
            row = jnp.broadcast_to(table[dr:dr + 1, :], (GRID_W, LANES))
            shift = (LANES - (NA_COLS - 1) + GRID_W * half) % LANES
            return pltpu.roll(row, shift, axis=1, stride=1, stride_axis=0)

        for i in range(o_ref.shape[1]):
            o_ref[h, i] = jnp.where(col_ok, jnp.where(low_half, shifted(i, 0), shifted(i + 1, 1)), NEG)


def _bias_tables(rpb):
    n_heads, n_dr, n_dc = rpb.shape
    assert n_dr == 2 * NA_ROWS - 1 and n_dc == 2 * NA_COLS - 1
    rpb_pad = jnp.pad(rpb, ((0, 0), (0, 16 - n_dr), (0, LANES - n_dc)))
    n_tiles = n_dr - 1
    heads_per_step = 8
    assert n_heads % heads_per_step == 0
    return pl.pallas_call(
        _bias_kernel,
        grid=(n_heads // heads_per_step,),
        in_specs=[pl.BlockSpec((heads_per_step, 16, LANES), lambda h: (h, 0, 0))],
        out_specs=pl.BlockSpec((heads_per_step, n_tiles, GRID_W, LANES), lambda h: (h, 0, 0, 0)),
        out_shape=jax.ShapeDtypeStruct((n_heads, n_tiles, GRID_W, LANES), F32),
        compiler_params=pltpu.CompilerParams(dimension_semantics=("arbitrary",),
                                             vmem_limit_bytes=VMEM_LIMIT_BYTES),
        name="bias_tables",
    )(rpb_pad)


def _union_row_start(i, n_rows):
    return jnp.minimum(_window_row_start(i * STEP_ROWS, n_rows), n_rows - UNION_ROWS)


def _mix_kernel(seq, x_ref, mod_ref, u_ref, up_ref, un_ref, zp_ref, za_ref, qr_ref, q_ref, k_hbm, v_hbm, *rest):
    (bias_ref, kc_ref, vc_ref, wp_ref, ps_ref, wo_ref, bo_ref, g_ref, be_ref,
     o_ref, cat_ref, k_buf, v_buf, win_sem) = rest
    d = x_ref.shape[1]
    d_pool = ps_ref.shape[1]
    gdim = d_pool // len(POOL_WINDOWS)
    n_heads = (cat_ref.shape[1] - d_pool) // HEAD_DIM
    n_rows = seq // GRID_W
    i = pl.program_id(1)
    n_blk = pl.num_programs(1)

    step = pl.program_id(0) * n_blk + i
    n_steps = pl.num_programs(0) * n_blk
    slot = step % 2

    def window_copies(s, into):
        tok0 = pl.multiple_of((s // n_blk) * seq + _union_row_start(s % n_blk, n_rows) * GRID_W, GRID_W)
        return [pltpu.make_async_copy(hbm.at[pl.ds(tok0, UNION_ROWS * GRID_W)], buf.at[into], win_sem.at[n, into])
                for n, (hbm, buf) in enumerate(((k_hbm, k_buf), (v_hbm, v_buf)))]

    @pl.when(step == 0)
    def _():
        for cp in window_copies(step, slot):
            cp.start()

    for cp in window_copies(step, slot):
        cp.wait()

    @pl.when(step + 1 < n_steps)
    def _():
        for cp in window_copies(step + 1, 1 - slot):
            cp.start()

    win_row = [_window_row_start(i * STEP_ROWS + r, n_rows) for r in range(STEP_ROWS)]
    win_off = [pl.multiple_of((win_row[r] - _union_row_start(i, n_rows)) * GRID_W, GRID_W) for r in range(STEP_ROWS)]
    win_dr = [win_row[r] - (i * STEP_ROWS + r) + NA_ROWS - 1 for r in range(STEP_ROWS)]

    n_ext = QB + 2 * POOL_HALO

    def block_rows(blk):
        return slice(blk * QB, (blk + 1) * QB)

    def shift_sum(s, k):
        return pltpu.roll(s, k, axis=0) + pltpu.roll(s, n_ext - k, axis=0)

    def pool_group(blk, g):
        w = POOL_WINDOWS[g]
        cols = slice(g * gdim, (g + 1) * gdim)
        rows = block_rows(blk)
        before = (jnp.where(i > 0, up_ref[:, cols], 0.0) if blk == 0
                  else u_ref[blk * QB - POOL_HALO:blk * QB, cols])
        after = (jnp.where(i < n_blk - 1, un_ref[:, cols], 0.0) if blk == STEP_BLOCKS - 1
                 else u_ref[(blk + 1) * QB:(blk + 1) * QB + POOL_HALO, cols])
        u = u_ref[rows, cols]
        s = jnp.concatenate([before, u, after], axis=0)
        s = s + pltpu.roll(s, 1, axis=0)
        k = 1
        while 2 * k < w:
            s = shift_sum(s, k)
            k *= 2
        s = s[POOL_HALO:POOL_HALO + QB]
        t = i * ST + blk * QB + lax.broadcasted_iota(jnp.int32, (QB, 1), 0)
        lo = jnp.clip(t - w // 2, 0, seq)
        hi = jnp.clip(t - w // 2 + w, 0, seq)
        pooled = s / (hi - lo).astype(F32) - u
        y = jnp.dot(pooled.astype(BF16), wp_ref[g], preferred_element_type=F32)
        y = y * ps_ref[:, cols] * _silu(zp_ref[rows, cols].astype(F32))
        cat_ref[rows, cols] = y.astype(BF16)

    nt_dims = (((1,), (1,)), ((), ()))
    lane = lax.broadcasted_iota(jnp.int32, (1, LANES), 1)
    first_head = lane < HEAD_DIM
    ones_win = jnp.ones((WIN, LANES), BF16)
    ones_ctx = jnp.ones((kc_ref.shape[0], LANES), BF16)
    stacked = 2 * GRID_W

    def chunk(p):
        return slice(p * LANES, (p + 1) * LANES)

    def stack_heads(ref, blk, p):
        qp = ref[block_rows(blk), chunk(p)]
        zero = jnp.zeros_like(qp)
        first, second = jnp.where(first_head, qp, zero), jnp.where(first_head, zero, qp)
        pieces = []
        for a in range(QUERY_ROWS):
            pieces += [first[a * GRID_W:(a + 1) * GRID_W], second[a * GRID_W:(a + 1) * GRID_W]]
        return jnp.concatenate(pieces, axis=0)

    def lane_tiles(a):
        return [a[:, c * LANES:(c + 1) * LANES] for c in range(a.shape[1] // LANES)]

    def scores(blk, p):
        q_rot = stack_heads(qr_ref, blk, p)
        s_loc = []
        for a in range(QUERY_ROWS):
            r = blk * QUERY_ROWS + a
            k_win = k_buf[slot, pl.ds(win_off[r], WIN), chunk(p)]
            bias = jnp.concatenate(
                [jnp.concatenate([bias_ref[2 * p + h, win_dr[r] + jr] for jr in range(0, NA_ROWS, 2)], axis=1)
                 for h in range(2)], axis=0)
            s_loc.append(lax.dot_general(q_rot[a * stacked:(a + 1) * stacked], k_win, nt_dims,
                                         preferred_element_type=F32) + bias)
        s_ctx = lax.dot_general(stack_heads(q_ref, blk, p), kc_ref[:, chunk(p)], nt_dims,
                                preferred_element_type=F32)
        return jnp.concatenate(s_loc, axis=0), s_ctx

    def softmax_numerators(s_loc, s_ctx):
        m = jnp.max(functools.reduce(jnp.maximum, lane_tiles(s_loc) + lane_tiles(s_ctx)), axis=1, keepdims=True)
        return jnp.concatenate([jnp.exp2(s_loc - m), jnp.exp2(s_ctx - m)], axis=1).astype(BF16)

    def weighted_values(blk, p, probs):
        vc_aug = jnp.concatenate([vc_ref[:, chunk(p)], ones_ctx], axis=1)
        outs = []
        for a in range(QUERY_ROWS):
            v_win = v_buf[slot, pl.ds(win_off[blk * QUERY_ROWS + a], WIN), chunk(p)]
            v_aug = jnp.concatenate([jnp.concatenate([v_win, ones_win], axis=1), vc_aug], axis=0)
            outs.append(jnp.dot(probs[a * stacked:(a + 1) * stacked], v_aug, preferred_element_type=F32))
        return outs

    def emit(blk, p, outs):
        num = jnp.concatenate([jnp.where(first_head, o[:GRID_W, :LANES], o[GRID_W:, :LANES]) for o in outs], axis=0)
        den = jnp.concatenate([jnp.where(first_head, o[:GRID_W, LANES:], o[GRID_W:, LANES:]) for o in outs], axis=0)
        out = num / den * _silu(za_ref[block_rows(blk), chunk(p)].astype(F32))
        cat_ref[block_rows(blk), d_pool + p * LANES:d_pool + (p + 1) * LANES] = out.astype(BF16)

    def finish(first_blk):
        rows = slice(first_blk * QB, (first_blk + OPROJ_BLOCKS) * QB)
        y = jnp.dot(cat_ref[rows, :], wo_ref[...], preferred_element_type=F32) + bo_ref[...]
        z = DEEPNORM_ALPHA * x_ref[rows, :] + mod_ref[0][:, 2 * d:3 * d] * y
        o_ref[rows, :] = _layer_norm(z) * g_ref[...] + be_ref[...]

    n_pairs = n_heads // 2
    items = [(blk, p) for blk in range(STEP_BLOCKS) for p in range(n_pairs)]
    s = {n: scores(*items[n]) for n in range(min(2, len(items)))}
    probs = {0: softmax_numerators(*s.pop(0))}
    for n, (blk, p) in enumerate(items):
        if n + 2 < len(items):
            s[n + 2] = scores(*items[n + 2])
        if n + 1 < len(items):
            probs[n + 1] = softmax_numerators(*s.pop(n + 1))
        if POOL_FIRST_PAIR <= p < POOL_FIRST_PAIR + len(POOL_WINDOWS):
            pool_group(blk, p - POOL_FIRST_PAIR)
        emit(blk, p, weighted_values(blk, p, probs.pop(n)))
        if blk > 0 and blk % OPROJ_BLOCKS == 0 and p == 0:
            finish(blk - OPROJ_BLOCKS)
    finish(STEP_BLOCKS - OPROJ_BLOCKS)


def _mix(x2d, mod3, u, zp, za, qr, q, kr, v, kc, vc, bias, w_pool, pool_scale, w_out, b_out, ln_g, ln_b,
         batch, seq, ctx_len):
    t, d = x2d.shape
    n_blk = seq // ST
    halo_per_blk = ST // POOL_HALO
    n_halo = t // POOL_HALO
    d_mix = w_out.shape[0]
    n_rows = seq // GRID_W
    single = pl.Buffered(1)

    def tok(b, i):
        return (b * n_blk + i, 0)

    full =lambda shape: pl.BlockSpec(shape, lambda b, i: (0,) * len(shape), pipeline_mode=single)
    in_specs = [
        pl.BlockSpec((ST, d), tok),
        pl.BlockSpec((1, 1, mod3.shape[2]), lambda b, i: (b, 0, 0)),
        pl.BlockSpec((ST, d), tok),
        pl.BlockSpec((POOL_HALO, d),
                     lambda b, i: (jnp.maximum((b * n_blk + i) * halo_per_blk - 1, 0), 0)),
        pl.BlockSpec((POOL_HALO, d),
                     lambda b, i: (jnp.minimum((b * n_blk + i + 1) * halo_per_blk, n_halo - 1), 0)),
        pl.BlockSpec((ST, d), tok),
        pl.BlockSpec((ST, d), tok),
        pl.BlockSpec((ST, d), tok),
        pl.BlockSpec((ST, d), tok),
    ]
    in_specs += [pl.BlockSpec(memory_space=pl.ANY)] * 2
    in_specs += [
        full(bias.shape),
        pl.BlockSpec((ctx_len, d), lambda b, i: (b, 0)),
        pl.BlockSpec((ctx_len, d), lambda b, i: (b, 0)),
        full(w_pool.shape), full(pool_scale.shape), full(w_out.shape),
        full(b_out.shape), full(ln_g.shape), full(ln_b.shape),
    ]
    args = ([x2d, mod3, u, u, u, zp, za, qr, q, kr, v]
            + [bias, kc, vc, w_pool, pool_scale, w_out, b_out, ln_g, ln_b])
    return pl.pallas_call(
        functools.partial(_mix_kernel, seq),
        grid=(batch, n_blk),
        in_specs=in_specs,
        out_specs=pl.BlockSpec((ST, d), tok),
        out_shape=jax.ShapeDtypeStruct((t, d), F32),
        scratch_shapes=[pltpu.VMEM((ST, d_mix), BF16),
                        pltpu.VMEM((2, UNION_ROWS * GRID_W, d), BF16),
                        pltpu.VMEM((2, UNION_ROWS * GRID_W, d), BF16),
                        pltpu.SemaphoreType.DMA((2, 2))],
        compiler_params=pltpu.CompilerParams(dimension_semantics=("arbitrary", "arbitrary"),
                                             vmem_limit_bytes=VMEM_LIMIT_BYTES),
        name="mix",
    )(*args)


def kernel(x, c, ctx, c_ctx, w_ada, b_ada, w_in, b_in, w_pool, pool_scale, rpb, w_out, b_out, ln_g, ln_b):
    batch, seq, d = x.shape
    ctx_len = ctx.shape[1]
    assert w_ada.shape[0] == DEPTH == 1
    assert seq % GRID_W == 0 and seq % INPROJ_TM == 0 and (batch * ctx_len) % INPROJ_TM == 0
    assert seq // GRID_W >= UNION_ROWS and seq % ST == 0
    d_pool = pool_scale.shape[1]
    assert d_pool == d and w_in.shape[2] == 6 * d and rpb.shape[1] * HEAD_DIM == d

    cvec = jnp.zeros((8, d), F32).at[:batch].set(c).at[batch].set(c_ctx)
    mod3 = _ada(cvec, w_ada[0], b_ada[0][None, :]).reshape(8, 1, 3 * d)

    w_in_b = w_in[0]
    b_in2 = b_in[0][None, :]
    x2d = x.reshape(batch * seq, d)
    tiles_per_seq = seq // INPROJ_TM
    u, zp, qr, q, kr, v, za = _inproj(
        x2d, mod3, lambda i: i // tiles_per_seq, w_in_b, b_in2, range(6),
        ("plain", "plain", "q", "k", "plain", "plain"), (F32, BF16, BF16, BF16, BF16, BF16, BF16),
        _rope_tables(seq), INPROJ_TM)
    kc, vc = _inproj(
        ctx.reshape(batch * ctx_len, d), mod3, lambda i: batch, w_in_b, b_in2, (3, 4),
        ("plain", "plain"), (BF16, BF16), None, INPROJ_TM)

    bias = _bias_tables(rpb[0])
    out = _mix(x2d, mod3, u, zp, za, qr, q, kr, v, kc, vc, bias,
               w_pool[0].astype(BF16), pool_scale, w_out[0].astype(BF16), b_out, ln_g, ln_b,
               batch, seq, ctx_len)
    return out.reshape(batch, seq, d)
```

```python
import functools

import numpy as np
import jax
import jax.numpy as jnp
from jax import lax
from jax.experimental import pallas as pl
from jax.experimental.pallas import tpu as pltpu

GRID_W = 64
POOL_WINDOWS = (2, 4, 8, 16)
HEAD_DIM = 64
NA_ROWS = 8
NA_COLS = 16
ROPE_BASE = 10000.0
LN_EPS = 1e-6
LOG2_E = float(np.log2(np.e))
SCORE_SCALE = HEAD_DIM ** -0.5 * LOG2_E
DEPTH = 1
DEEPNORM_ALPHA = (2.0 * DEPTH) ** 0.25

LANES = 128
MIB = 1024 * 1024
VMEM_LIMIT_SMALL = 24 * MIB
VMEM_LIMIT_INPROJ = 58 * MIB
VMEM_LIMIT_MIX = 48 * MIB

QUERY_ROWS = 2
QB = QUERY_ROWS * GRID_W
WIN = NA_ROWS * GRID_W
STEP_BLOCKS = 4
STEP_ROWS = STEP_BLOCKS * QUERY_ROWS
ST = STEP_BLOCKS * QB
OPROJ_BLOCKS = 4
POOL_FIRST_PAIR = 2
UNION_ROWS = NA_ROWS + STEP_ROWS - 1
POOL_HALO = 8
NEG = -1e30
INPROJ_TM = 512
F32 = jnp.float32
BF16 = jnp.bfloat16


def _silu(x):
    return x * jax.nn.sigmoid(x)


def _layer_norm(x):
    mu = jnp.mean(x, axis=-1, keepdims=True)
    xc = x - mu
    var = jnp.mean(xc * xc, axis=-1, keepdims=True)
    return xc * lax.rsqrt(var + LN_EPS)


def _ada_kernel(c_ref, w_ref, b_ref, o_ref):
    s = _silu(c_ref[...]).astype(BF16)
    o_ref[...] = jnp.dot(s, w_ref[...].astype(BF16), preferred_element_type=F32) + b_ref[...]


def _ada(cvec, w_ada, b_ada):
    rows, d = cvec.shape
    n = w_ada.shape[1]
    tn = 768
    return pl.pallas_call(
        _ada_kernel,
        grid=(n // tn,),
        in_specs=[pl.BlockSpec((rows, d), lambda j: (0, 0)),
                  pl.BlockSpec((d, tn), lambda j: (0, j)),
                  pl.BlockSpec((1, tn), lambda j: (0, j))],
        out_specs=pl.BlockSpec((rows, tn), lambda j: (0, j)),
        out_shape=jax.ShapeDtypeStruct((rows, n), F32),
        compiler_params=pltpu.CompilerParams(dimension_semantics=("arbitrary",),
                                             vmem_limit_bytes=VMEM_LIMIT_SMALL),
        name="ada",
    )(cvec, w_ada, b_ada)


def _rope(x, cos, sin):
    half = HEAD_DIM // 4
    first_half = lax.broadcasted_iota(jnp.int32, (1, LANES), 1) % (2 * half) < half
    outs = []
    for c in range(x.shape[1] // LANES):
        xc = x[:, c * LANES:(c + 1) * LANES]
        partner = jnp.where(first_half, pltpu.roll(xc, LANES - half, axis=1), pltpu.roll(xc, half, axis=1))
        outs.append(xc * cos + partner * sin)
    return jnp.concatenate(outs, axis=1)


def _inproj_kernel(kinds, has_rope, *refs):
    n_blk = len(kinds)
    x0_ref, mod0_ref, xn_ref, modn_ref = refs[:4]
    w_refs = refs[4:4 + n_blk]
    b_refs = refs[4 + n_blk:4 + 2 * n_blk]
    n_in = 4 + 2 * n_blk + (2 if has_rope else 0)
    cos_ref, sin_ref = refs[n_in - 2:n_in] if has_rope else (None, None)
    out_refs = refs[n_in:-1]
    h_ref = refs[-1]
    d = xn_ref.shape[1]
    i = pl.program_id(0)
    slot = i % 2

    def modulated(x_ref, mod_ref):
        mod = mod_ref[0]
        return (_layer_norm(x_ref[...]) * (1.0 + mod[:, d:2 * d]) + mod[:, :d]).astype(BF16)

    @pl.when(i == 0)
    def _():
        h_ref[0] = modulated(x0_ref, mod0_ref)

    o = 0
    for n, (kind, w_ref, b_ref) in enumerate(zip(kinds, w_refs, b_refs)):
        w = w_ref[...].astype(BF16)
        acc = jnp.dot(h_ref[slot], w, preferred_element_type=F32) + b_ref[...]
        if n == 0:
            h_ref[1 - slot] = modulated(xn_ref, modn_ref)
        if kind == "q":
            qr_ref, q_ref = out_refs[o:o + 2]
            o += 2
            qs = acc * SCORE_SCALE
            q_ref[...] = qs.astype(q_ref.dtype)
            qr_ref[...] = _rope(qs, cos_ref[...], sin_ref[...]).astype(qr_ref.dtype)
        else:
            val = _rope(acc, cos_ref[...], sin_ref[...]) if kind == "k" else acc
            out_refs[o][...] = val.astype(out_refs[o].dtype)
            o += 1


def _inproj(x2d, mod3, mod_rows_per_tile, w_bf16, b_in, col_blocks, kinds, out_dtypes, rope, tm):
    t, d = x2d.shape
    nt = t // tm
    has_rope = rope is not None
    resident = pl.Buffered(1)

    def next_tile(i):
        return jnp.minimum(i + 1, nt - 1)

    mod_block = (1, 1, mod3.shape[2])
    in_specs = [
        pl.BlockSpec((tm, d), lambda i: (0, 0), pipeline_mode=resident),
        pl.BlockSpec(mod_block, lambda i: (mod_rows_per_tile(0), 0, 0), pipeline_mode=resident),
        pl.BlockSpec((tm, d), lambda i: (next_tile(i), 0)),
        pl.BlockSpec(mod_block, lambda i: (mod_rows_per_tile(next_tile(i)), 0, 0)),
    ]
    in_specs += [pl.BlockSpec((d, d), lambda i, c=c: (0, c), pipeline_mode=resident) for c in col_blocks]
    in_specs += [pl.BlockSpec((1, d), lambda i, c=c: (0, c), pipeline_mode=resident) for c in col_blocks]
    args = [x2d, mod3, x2d, mod3] + [w_bf16] * len(col_blocks) + [b_in] * len(col_blocks)
    if has_rope:
        tiles_per_seq = rope[0].shape[0] // tm
        for tab in rope:
            in_specs.append(pl.BlockSpec((tm, LANES), lambda i: (i % tiles_per_seq, 0)))
            args.append(tab)
    out_specs = [pl.BlockSpec((tm, d), lambda i: (i, 0)) for _ in out_dtypes]
    out_shape = [jax.ShapeDtypeStruct((t, d), dt) for dt in out_dtypes]
    return pl.pallas_call(
        functools.partial(_inproj_kernel, tuple(kinds), has_rope),
        grid=(nt,),
        in_specs=in_specs,
        out_specs=out_specs,
        out_shape=out_shape,
        scratch_shapes=[pltpu.VMEM((2, tm, d), BF16)],
        compiler_params=pltpu.CompilerParams(
            dimension_semantics=("arbitrary",),
            vmem_limit_bytes=VMEM_LIMIT_INPROJ if len(col_blocks) > 2 else VMEM_LIMIT_SMALL),
        name="inproj_rope" if has_rope else "inproj_ctx",
    )(*args)


def _rope_tables(seq):
    half = HEAD_DIM // 4
    inv_freq = ROPE_BASE ** (-np.arange(half, dtype=np.float64) / half)
    t = np.arange(seq)
    row, col = t // GRID_W, t % GRID_W
    lane = np.arange(LANES)
    pos = np.where((lane % HEAD_DIM < HEAD_DIM // 2)[None, :], row[:, None], col[:, None]).astype(np.float64)
    ang = pos * inv_freq[lane % half][None, :]
    sign = np.where(lane % (2 * half) < half, -1.0, 1.0)[None, :]
    return jnp.asarray(np.cos(ang), F32), jnp.asarray(np.sin(ang) * sign, F32)


def _window_row_start(r, n_rows):
    return jnp.clip(r - NA_ROWS // 2, 0, n_rows - NA_ROWS)


def _bias_kernel(rpb_ref, o_ref):
    lane = lax.broadcasted_iota(jnp.int32, (GRID_W, LANES), 1)
    qc = lax.broadcasted_iota(jnp.int32, (GRID_W, LANES), 0)
    kc = lane % GRID_W
    cs = jnp.clip(qc - NA_COLS // 2, 0, GRID_W - NA_COLS)
    col_ok = (kc >= cs) & (kc < cs + NA_COLS)
    low_half = lane < GRID_W

    for h in range(o_ref.shape[0]):
        table = rpb_ref[h] * LOG2_E

        @functools.cache
        def shifted(dr, half, table=table):
            row = jnp.broadcast_to(table[dr:dr + 1, :], (GRID_W, LANES))
            shift = (LANES - (NA_COLS - 1) + GRID_W * half) % LANES
            return pltpu.roll(row, shift, axis=1, stride=1, stride_axis=0)

        for i in range(o_ref.shape[1]):
            o_ref[h, i] = jnp.where(col_ok, jnp.where(low_half, shifted(i, 0), shifted(i + 1, 1)), NEG)


def _bias_tables(rpb):
    n_heads, n_dr, n_dc = rpb.shape
    assert n_dr == 2 * NA_ROWS - 1 and n_dc == 2 * NA_COLS - 1
    rpb_pad = jnp.pad(rpb, ((0, 0), (0, 16 - n_dr), (0, LANES - n_dc)))
    n_tiles = n_dr - 1
    heads_per_step = 8
    assert n_heads % heads_per_step == 0
    return pl.pallas_call(
        _bias_kernel,
        grid=(n_heads // heads_per_step,),
        in_specs=[pl.BlockSpec((heads_per_step, 16, LANES), lambda h: (h, 0, 0))],
        out_specs=pl.BlockSpec((heads_per_step, n_tiles, GRID_W, LANES), lambda h: (h, 0, 0, 0)),
        out_shape=jax.ShapeDtypeStruct((n_heads, n_tiles, GRID_W, LANES), F32),
        compiler_params=pltpu.CompilerParams(dimension_semantics=("arbitrary",),
                                             vmem_limit_bytes=VMEM_LIMIT_SMALL),
        name="bias_tables",
    )(rpb_pad)


def _union_row_start(i, n_rows):
    return jnp.minimum(_window_row_start(i * STEP_ROWS, n_rows), n_rows - UNION_ROWS)


def _mix_kernel(seq, x_ref, mod_ref, u_ref, up_ref, un_ref, zp_ref, za_ref, qr_ref, q_ref, k_hbm, v_hbm, *rest):
    (bias_ref, kc_ref, vc_ref, wp_ref, ps_ref, wo_ref, bo_ref, g_ref, be_ref,
     o_ref, cat_ref, k_buf, v_buf, win_sem) = rest
    d = x_ref.shape[1]
    d_pool = ps_ref.shape[1]
    gdim = d_pool // len(POOL_WINDOWS)
    n_heads = (cat_ref.shape[1] - d_pool) // HEAD_DIM
    n_rows = seq // GRID_W
    i = pl.program_id(1)
    n_blk = pl.num_programs(1)

    step = pl.program_id(0) * n_blk + i
    n_steps = pl.num_programs(0) * n_blk
    slot = step % 2

    def window_copies(s, into):
        tok0 = pl.multiple_of((s // n_blk) * seq + _union_row_start(s % n_blk, n_rows) * GRID_W, GRID_W)
        return [pltpu.make_async_copy(hbm.at[pl.ds(tok0, UNION_ROWS * GRID_W)], buf.at[into], win_sem.at[n, into])
                for n, (hbm, buf) in enumerate(((k_hbm, k_buf), (v_hbm, v_buf)))]

    @pl.when(step == 0)
    def _():
        for cp in window_copies(step, slot):
            cp.start()

    for cp in window_copies(step, slot):
        cp.wait()

    @pl.when(step + 1 < n_steps)
    def _():
        for cp in window_copies(step + 1, 1 - slot):
            cp.start()

    win_row = [_window_row_start(i * STEP_ROWS + r, n_rows) for r in range(STEP_ROWS)]
    win_off = [pl.multiple_of((win_row[r] - _union_row_start(i, n_rows)) * GRID_W, GRID_W) for r in range(STEP_ROWS)]
    win_dr = [win_row[r] - (i * STEP_ROWS + r) + NA_ROWS - 1 for r in range(STEP_ROWS)]

    n_ext = QB + 2 * POOL_HALO

    def block_rows(blk):
        return slice(blk * QB, (blk + 1) * QB)

    def shift_sum(s, k):
        return pltpu.roll(s, k, axis=0) + pltpu.roll(s, n_ext - k, axis=0)

    def pool_group(blk, g):
        w = POOL_WINDOWS[g]
        cols = slice(g * gdim, (g + 1) * gdim)
        rows = block_rows(blk)
        before = (jnp.where(i > 0, up_ref[:, cols], 0.0) if blk == 0
                  else u_ref[blk * QB - POOL_HALO:blk * QB, cols])
        after = (jnp.where(i < n_blk - 1, un_ref[:, cols], 0.0) if blk == STEP_BLOCKS - 1
                 else u_ref[(blk + 1) * QB:(blk + 1) * QB + POOL_HALO, cols])
        u = u_ref[rows, cols]
        s = jnp.concatenate([before, u, after], axis=0)
        s = s + pltpu.roll(s, 1, axis=0)
        k = 1
        while 2 * k < w:
            s = shift_sum(s, k)
            k *= 2
        s = s[POOL_HALO:POOL_HALO + QB]
        t = i * ST + blk * QB + lax.broadcasted_iota(jnp.int32, (QB, 1), 0)
        lo = jnp.clip(t - w // 2, 0, seq)
        hi = jnp.clip(t - w // 2 + w, 0, seq)
        pooled = s / (hi - lo).astype(F32) - u
        y = jnp.dot(pooled.astype(BF16), wp_ref[g], preferred_element_type=F32)
        y = y * ps_ref[:, cols] * _silu(zp_ref[rows, cols].astype(F32))
        cat_ref[rows, cols] = y.astype(BF16)

    nt_dims = (((1,), (1,)), ((), ()))
    lane = lax.broadcasted_iota(jnp.int32, (1, LANES), 1)
    first_head = lane < HEAD_DIM
    ones_win = jnp.ones((WIN, LANES), BF16)
    ones_ctx = jnp.ones((kc_ref.shape[0], LANES), BF16)
    stacked = 2 * GRID_W

    def chunk(p):
        return slice(p * LANES, (p + 1) * LANES)

    def stack_heads(ref, blk, p):
        qp = ref[block_rows(blk), chunk(p)]
        zero = jnp.zeros_like(qp)
        first, second = jnp.where(first_head, qp, zero), jnp.where(first_head, zero, qp)
        pieces = []
        for a in range(QUERY_ROWS):
            pieces += [first[a * GRID_W:(a + 1) * GRID_W], second[a * GRID_W:(a + 1) * GRID_W]]
        return jnp.concatenate(pieces, axis=0)

    def lane_tiles(a):
        return [a[:, c * LANES:(c + 1) * LANES] for c in range(a.shape[1] // LANES)]

    def scores(blk, p):
        q_rot = stack_heads(qr_ref, blk, p)
        s_loc = []
        for a in range(QUERY_ROWS):
            r = blk * QUERY_ROWS + a
            k_win = k_buf[slot, pl.ds(win_off[r], WIN), chunk(p)]
            bias = jnp.concatenate(
                [jnp.concatenate([bias_ref[2 * p + h, win_dr[r] + jr] for jr in range(0, NA_ROWS, 2)], axis=1)
                 for h in range(2)], axis=0)
            s_loc.append(lax.dot_general(q_rot[a * stacked:(a + 1) * stacked], k_win, nt_dims,
                                         preferred_element_type=F32) + bias)
        s_ctx = lax.dot_general(stack_heads(q_ref, blk, p), kc_ref[:, chunk(p)], nt_dims,
                                preferred_element_type=F32)
        return jnp.concatenate(s_loc, axis=0), s_ctx

    def softmax_numerators(s_loc, s_ctx):
        m = jnp.max(functools.reduce(jnp.maximum, lane_tiles(s_loc) + lane_tiles(s_ctx)), axis=1, keepdims=True)
        return jnp.concatenate([jnp.exp2(s_loc - m), jnp.exp2(s_ctx - m)], axis=1).astype(BF16)

    def weighted_values(blk, p, probs):
        vc_aug = jnp.concatenate([vc_ref[:, chunk(p)], ones_ctx], axis=1)
        outs = []
        for a in range(QUERY_ROWS):
            v_win = v_buf[slot, pl.ds(win_off[blk * QUERY_ROWS + a], WIN), chunk(p)]
            v_aug = jnp.concatenate([jnp.concatenate([v_win, ones_win], axis=1), vc_aug], axis=0)
            outs.append(jnp.dot(probs[a * stacked:(a + 1) * stacked], v_aug, preferred_element_type=F32))
        return outs

    def emit(blk, p, outs):
        num = jnp.concatenate([jnp.where(first_head, o[:GRID_W, :LANES], o[GRID_W:, :LANES]) for o in outs], axis=0)
        den = jnp.concatenate([jnp.where(first_head, o[:GRID_W, LANES:], o[GRID_W:, LANES:]) for o in outs], axis=0)
        out = num / den * _silu(za_ref[block_rows(blk), chunk(p)].astype(F32))
        cat_ref[block_rows(blk), d_pool + p * LANES:d_pool + (p + 1) * LANES] = out.astype(BF16)

    def finish(first_blk):
        rows = slice(first_blk * QB, (first_blk + OPROJ_BLOCKS) * QB)
        y = jnp.dot(cat_ref[rows, :], wo_ref[...], preferred_element_type=F32) + bo_ref[...]
        z = DEEPNORM_ALPHA * x_ref[rows, :] + mod_ref[0][:, 2 * d:3 * d] * y
        o_ref[rows, :] = _layer_norm(z) * g_ref[...] + be_ref[...]

    n_pairs = n_heads // 2
    items = [(blk, p) for blk in range(STEP_BLOCKS) for p in range(n_pairs)]
    s = {n: scores(*items[n]) for n in range(min(2, len(items)))}
    probs = {0: softmax_numerators(*s.pop(0))}
    for n, (blk, p) in enumerate(items):
        if n + 2 < len(items):
            s[n + 2] = scores(*items[n + 2])
        if n + 1 < len(items):
            probs[n + 1] = softmax_numerators(*s.pop(n + 1))
        if POOL_FIRST_PAIR <= p < POOL_FIRST_PAIR + len(POOL_WINDOWS):
            pool_group(blk, p - POOL_FIRST_PAIR)
        emit(blk, p, weighted_values(blk, p, probs.pop(n)))
        if blk > 0 and blk % OPROJ_BLOCKS == 0 and p == 0:
            finish(blk - OPROJ_BLOCKS)
    finish(STEP_BLOCKS - OPROJ_BLOCKS)


def _mix(x2d, mod3, u, zp, za, qr, q, kr, v, kc, vc, bias, w_pool, pool_scale, w_out, b_out, ln_g, ln_b,
         batch, seq, ctx_len):
    t, d = x2d.shape
    n_blk = seq // ST
    halo_per_blk = ST // POOL_HALO
    n_halo = t // POOL_HALO
    d_mix = w_out.shape[0]
    n_rows = seq // GRID_W
    single = pl.Buffered(1)

    def tok(b, i):
        return (b * n_blk + i, 0)

    full =lambda shape: pl.BlockSpec(shape, lambda b, i: (0,) * len(shape), pipeline_mode=single)
    in_specs = [
        pl.BlockSpec((ST, d), tok),
        pl.BlockSpec((1, 1, mod3.shape[2]), lambda b, i: (b, 0, 0)),
        pl.BlockSpec((ST, d), tok),
        pl.BlockSpec((POOL_HALO, d),
                     lambda b, i: (jnp.maximum((b * n_blk + i) * halo_per_blk - 1, 0), 0)),
        pl.BlockSpec((POOL_HALO, d),
                     lambda b, i: (jnp.minimum((b * n_blk + i + 1) * halo_per_blk, n_halo - 1), 0)),
        pl.BlockSpec((ST, d), tok),
        pl.BlockSpec((ST, d), tok),
        pl.BlockSpec((ST, d), tok),
        pl.BlockSpec((ST, d), tok),
    ]
    in_specs += [pl.BlockSpec(memory_space=pl.ANY)] * 2
    in_specs += [
        full(bias.shape),
        pl.BlockSpec((ctx_len, d), lambda b, i: (b, 0)),
        pl.BlockSpec((ctx_len, d), lambda b, i: (b, 0)),
        full(w_pool.shape), full(pool_scale.shape), full(w_out.shape),
        full(b_out.shape), full(ln_g.shape), full(ln_b.shape),
    ]
    args = ([x2d, mod3, u, u, u, zp, za, qr, q, kr, v]
            + [bias, kc, vc, w_pool, pool_scale, w_out, b_out, ln_g, ln_b])
    return pl.pallas_call(
        functools.partial(_mix_kernel, seq),
        grid=(batch, n_blk),
        in_specs=in_specs,
        out_specs=pl.BlockSpec((ST, d), tok),
        out_shape=jax.ShapeDtypeStruct((t, d), F32),
        scratch_shapes=[pltpu.VMEM((ST, d_mix), BF16),
                        pltpu.VMEM((2, UNION_ROWS * GRID_W, d), BF16),
                        pltpu.VMEM((2, UNION_ROWS * GRID_W, d), BF16),
                        pltpu.SemaphoreType.DMA((2, 2))],
        compiler_params=pltpu.CompilerParams(dimension_semantics=("arbitrary", "arbitrary"),
                                             vmem_limit_bytes=VMEM_LIMIT_MIX),
        name="mix",
    )(*args)


def kernel(x, c, ctx, c_ctx, w_ada, b_ada, w_in, b_in, w_pool, pool_scale, rpb, w_out, b_out, ln_g, ln_b):
    batch, seq, d = x.shape
    ctx_len = ctx.shape[1]
    assert w_ada.shape[0] == DEPTH == 1
    assert seq % GRID_W == 0 and seq % INPROJ_TM == 0 and (batch * ctx_len) % INPROJ_TM == 0
    assert seq // GRID_W >= UNION_ROWS and seq % ST == 0
    d_pool = pool_scale.shape[1]
    assert d_pool == d and w_in.shape[2] == 6 * d and rpb.shape[1] * HEAD_DIM == d

    cvec = jnp.zeros((8, d), F32).at[:batch].set(c).at[batch].set(c_ctx)
    mod3 = _ada(cvec, w_ada[0], b_ada[0][None, :]).reshape(8, 1, 3 * d)

    w_in_b = w_in[0]
    b_in2 = b_in[0][None, :]
    x2d = x.reshape(batch * seq, d)
    tiles_per_seq = seq // INPROJ_TM
    u, zp, qr, q, kr, v, za = _inproj(
        x2d, mod3, lambda i: i // tiles_per_seq, w_in_b, b_in2, range(6),
        ("plain", "plain", "q", "k", "plain", "plain"), (F32, BF16, BF16, BF16, BF16, BF16, BF16),
        _rope_tables(seq), INPROJ_TM)
    kc, vc = _inproj(
        ctx.reshape(batch * ctx_len, d), mod3, lambda i: batch, w_in_b, b_in2, (3, 4),
        ("plain", "plain"), (BF16, BF16), None, INPROJ_TM)

    bias = _bias_tables(rpb[0])
    out = _mix(x2d, mod3, u, zp, za, qr, q, kr, v, kc, vc, bias,
               w_pool[0].astype(BF16), pool_scale, w_out[0].astype(BF16), b_out, ln_g, ln_b,
               batch, seq, ctx_len)
    return out.reshape(batch, seq, d)
```

```python
import functools

import numpy as np
import jax
import jax.numpy as jnp
from jax import lax
from jax.experimental import pallas as pl
from jax.experimental.pallas import tpu as pltpu

GRID_W = 64
POOL_WINDOWS = (2, 4, 8, 16)
HEAD_DIM = 64
NA_ROWS = 8
NA_COLS = 16
ROPE_BASE = 10000.0
LN_EPS = 1e-6
LOG2_E = float(np.log2(np.e))
SCORE_SCALE = HEAD_DIM ** -0.5 * LOG2_E
DEPTH = 1
DEEPNORM_ALPHA = (2.0 * DEPTH) ** 0.25

LANES = 128
MIB = 1024 * 1024
VMEM_LIMIT_SMALL = 52 * MIB
VMEM_LIMIT_INPROJ = 58 * MIB
VMEM_LIMIT_MIX = 52 * MIB

QUERY_ROWS = 2
QB = QUERY_ROWS * GRID_W
WIN = NA_ROWS * GRID_W
STEP_BLOCKS = 4
STEP_ROWS = STEP_BLOCKS * QUERY_ROWS
ST = STEP_BLOCKS * QB
OPROJ_BLOCKS = 4
POOL_FIRST_PAIR = 2
UNION_ROWS = NA_ROWS + STEP_ROWS - 1
POOL_HALO = 8
NEG = -1e30
INPROJ_TM = 512
F32 = jnp.float32
BF16 = jnp.bfloat16


def _silu(x):
    return x * jax.nn.sigmoid(x)


def _layer_norm(x):
    mu = jnp.mean(x, axis=-1, keepdims=True)
    xc = x - mu
    var = jnp.mean(xc * xc, axis=-1, keepdims=True)
    return xc * lax.rsqrt(var + LN_EPS)


def _ada_kernel(c_ref, w_ref, b_ref, o_ref):
    s = _silu(c_ref[...]).astype(BF16)
    o_ref[...] = jnp.dot(s, w_ref[...].astype(BF16), preferred_element_type=F32) + b_ref[...]


def _ada(cvec, w_ada, b_ada):
    rows, d = cvec.shape
    n = w_ada.shape[1]
    tn = 768
    return pl.pallas_call(
        _ada_kernel,
        grid=(n // tn,),
        in_specs=[pl.BlockSpec((rows, d), lambda j: (0, 0)),
                  pl.BlockSpec((d, tn), lambda j: (0, j)),
                  pl.BlockSpec((1, tn), lambda j: (0, j))],
        out_specs=pl.BlockSpec((rows, tn), lambda j: (0, j)),
        out_shape=jax.ShapeDtypeStruct((rows, n), F32),
        compiler_params=pltpu.CompilerParams(dimension_semantics=("arbitrary",),
                                             vmem_limit_bytes=VMEM_LIMIT_SMALL),
        name="ada",
    )(cvec, w_ada, b_ada)


def _rope(x, cos, sin):
    half = HEAD_DIM // 4
    first_half = lax.broadcasted_iota(jnp.int32, (1, LANES), 1) % (2 * half) < half
    outs = []
    for c in range(x.shape[1] // LANES):
        xc = x[:, c * LANES:(c + 1) * LANES]
        partner = jnp.where(first_half, pltpu.roll(xc, LANES - half, axis=1), pltpu.roll(xc, half, axis=1))
        outs.append(xc * cos + partner * sin)
    return jnp.concatenate(outs, axis=1)


def _inproj_kernel(kinds, has_rope, *refs):
    n_blk = len(kinds)
    x0_ref, mod0_ref, xn_ref, modn_ref = refs[:4]
    w_refs = refs[4:4 + n_blk]
    b_refs = refs[4 + n_blk:4 + 2 * n_blk]
    n_in = 4 + 2 * n_blk + (2 if has_rope else 0)
    cos_ref, sin_ref = refs[n_in - 2:n_in] if has_rope else (None, None)
    out_refs = refs[n_in:-1]
    h_ref = refs[-1]
    d = xn_ref.shape[1]
    i = pl.program_id(0)
    slot = i % 2

    def modulated(x_ref, mod_ref):
        mod = mod_ref[0]
        return (_layer_norm(x_ref[...]) * (1.0 + mod[:, d:2 * d]) + mod[:, :d]).astype(BF16)

    @pl.when(i == 0)
    def _():
        h_ref[0] = modulated(x0_ref, mod0_ref)

    o = 0
    for n, (kind, w_ref, b_ref) in enumerate(zip(kinds, w_refs, b_refs)):
        w = w_ref[...].astype(BF16)
        acc = jnp.dot(h_ref[slot], w, preferred_element_type=F32) + b_ref[...]
        if n == 0:
            h_ref[1 - slot] = modulated(xn_ref, modn_ref)
        if kind == "q":
            qr_ref, q_ref = out_refs[o:o + 2]
            o += 2
            qs = acc * SCORE_SCALE
            q_ref[...] = qs.astype(q_ref.dtype)
            qr_ref[...] = _rope(qs, cos_ref[...], sin_ref[...]).astype(qr_ref.dtype)
        else:
            val = _rope(acc, cos_ref[...], sin_ref[...]) if kind == "k" else acc
            out_refs[o][...] = val.astype(out_refs[o].dtype)
            o += 1


def _inproj(x2d, mod3, mod_rows_per_tile, w_bf16, b_in, col_blocks, kinds, out_dtypes, rope, tm):
    t, d = x2d.shape
    nt = t // tm
    has_rope = rope is not None
    resident = pl.Buffered(1)

    def next_tile(i):
        return jnp.minimum(i + 1, nt - 1)

    mod_block = (1, 1, mod3.shape[2])
    in_specs = [
        pl.BlockSpec((tm, d), lambda i: (0, 0), pipeline_mode=resident),
        pl.BlockSpec(mod_block, lambda i: (mod_rows_per_tile(0), 0, 0), pipeline_mode=resident),
        pl.BlockSpec((tm, d), lambda i: (next_tile(i), 0)),
        pl.BlockSpec(mod_block, lambda i: (mod_rows_per_tile(next_tile(i)), 0, 0)),
    ]
    in_specs += [pl.BlockSpec((d, d), lambda i, c=c: (0, c), pipeline_mode=resident) for c in col_blocks]
    in_specs += [pl.BlockSpec((1, d), lambda i, c=c: (0, c), pipeline_mode=resident) for c in col_blocks]
    args = [x2d, mod3, x2d, mod3] + [w_bf16] * len(col_blocks) + [b_in] * len(col_blocks)
    if has_rope:
        tiles_per_seq = rope[0].shape[0] // tm
        for tab in rope:
            in_specs.append(pl.BlockSpec((tm, LANES), lambda i: (i % tiles_per_seq, 0)))
            args.append(tab)
    out_specs = [pl.BlockSpec((tm, d), lambda i: (i, 0)) for _ in out_dtypes]
    out_shape = [jax.ShapeDtypeStruct((t, d), dt) for dt in out_dtypes]
    return pl.pallas_call(
        functools.partial(_inproj_kernel, tuple(kinds), has_rope),
        grid=(nt,),
        in_specs=in_specs,
        out_specs=out_specs,
        out_shape=out_shape,
        scratch_shapes=[pltpu.VMEM((2, tm, d), BF16)],
        compiler_params=pltpu.CompilerParams(
            dimension_semantics=("arbitrary",),
            vmem_limit_bytes=VMEM_LIMIT_INPROJ if len(col_blocks) > 2 else VMEM_LIMIT_SMALL),
        name="inproj_rope" if has_rope else "inproj_ctx",
    )(*args)


def _rope_tables(seq):
    half = HEAD_DIM // 4
    inv_freq = ROPE_BASE ** (-np.arange(half, dtype=np.float64) / half)
    t = np.arange(seq)
    row, col = t // GRID_W, t % GRID_W
    lane = np.arange(LANES)
    pos = np.where((lane % HEAD_DIM < HEAD_DIM // 2)[None, :], row[:, None], col[:, None]).astype(np.float64)
    ang = pos * inv_freq[lane % half][None, :]
    sign = np.where(lane % (2 * half) < half, -1.0, 1.0)[None, :]
    return jnp.asarray(np.cos(ang), F32), jnp.asarray(np.sin(ang) * sign, F32)


def _window_row_start(r, n_rows):
    return jnp.clip(r - NA_ROWS // 2, 0, n_rows - NA_ROWS)


def _bias_kernel(rpb_ref, o_ref):
    lane = lax.broadcasted_iota(jnp.int32, (GRID_W, LANES), 1)
    qc = lax.broadcasted_iota(jnp.int32, (GRID_W, LANES), 0)
    kc = lane % GRID_W
    cs = jnp.clip(qc - NA_COLS // 2, 0, GRID_W - NA_COLS)
    col_ok = (kc >= cs) & (kc < cs + NA_COLS)
    low_half = lane < GRID_W

    for h in range(o_ref.shape[0]):
        table = rpb_ref[h] * LOG2_E

        @functools.cache
        def shifted(dr, half, table=table):
            row = jnp.broadcast_to(table[dr:dr + 1, :], (GRID_W, LANES))
            shift = (LANES - (NA_COLS - 1) + GRID_W * half) % LANES
            return pltpu.roll(row, shift, axis=1, stride=1, stride_axis=0)

        for i in range(o_ref.shape[1]):
            o_ref[h, i] = jnp.where(col_ok, jnp.where(low_half, shifted(i, 0), shifted(i + 1, 1)), NEG)


def _bias_tables(rpb):
    n_heads, n_dr, n_dc = rpb.shape
    assert n_dr == 2 * NA_ROWS - 1 and n_dc == 2 * NA_COLS - 1
    rpb_pad = jnp.pad(rpb, ((0, 0), (0, 16 - n_dr), (0, LANES - n_dc)))
    n_tiles = n_dr - 1
    heads_per_step = 8
    assert n_heads % heads_per_step == 0
    return pl.pallas_call(
        _bias_kernel,
        grid=(n_heads // heads_per_step,),
        in_specs=[pl.BlockSpec((heads_per_step, 16, LANES), lambda h: (h, 0, 0))],
        out_specs=pl.BlockSpec((heads_per_step, n_tiles, GRID_W, LANES), lambda h: (h, 0, 0, 0)),
        out_shape=jax.ShapeDtypeStruct((n_heads, n_tiles, GRID_W, LANES), F32),
        compiler_params=pltpu.CompilerParams(dimension_semantics=("arbitrary",),
                                             vmem_limit_bytes=VMEM_LIMIT_SMALL),
        name="bias_tables",
    )(rpb_pad)


def _union_row_start(i, n_rows):
    return jnp.minimum(_window_row_start(i * STEP_ROWS, n_rows), n_rows - UNION_ROWS)


def _mix_kernel(seq, x_ref, mod_ref, u_ref, up_ref, un_ref, zp_ref, za_ref, qr_ref, q_ref, k_hbm, v_hbm, *rest):
    (bias_ref, kc_ref, vc_ref, wp_ref, ps_ref, wo_ref, bo_ref, g_ref, be_ref,
     o_ref, cat_ref, k_buf, v_buf, win_sem) = rest
    d = x_ref.shape[1]
    d_pool = ps_ref.shape[1]
    gdim = d_pool // len(POOL_WINDOWS)
    n_heads = (cat_ref.shape[1] - d_pool) // HEAD_DIM
    n_rows = seq // GRID_W
    i = pl.program_id(1)
    n_blk = pl.num_programs(1)

    step = pl.program_id(0) * n_blk + i
    n_steps = pl.num_programs(0) * n_blk
    slot = step % 2

    def window_copies(s, into):
        tok0 = pl.multiple_of((s // n_blk) * seq + _union_row_start(s % n_blk, n_rows) * GRID_W, GRID_W)
        return [pltpu.make_async_copy(hbm.at[pl.ds(tok0, UNION_ROWS * GRID_W)], buf.at[into], win_sem.at[n, into])
                for n, (hbm, buf) in enumerate(((k_hbm, k_buf), (v_hbm, v_buf)))]

    @pl.when(step == 0)
    def _():
        for cp in window_copies(step, slot):
            cp.start()

    for cp in window_copies(step, slot):
        cp.wait()

    @pl.when(step + 1 < n_steps)
    def _():
        for cp in window_copies(step + 1, 1 - slot):
            cp.start()

    win_row = [_window_row_start(i * STEP_ROWS + r, n_rows) for r in range(STEP_ROWS)]
    win_off = [pl.multiple_of((win_row[r] - _union_row_start(i, n_rows)) * GRID_W, GRID_W) for r in range(STEP_ROWS)]
    win_dr = [win_row[r] - (i * STEP_ROWS + r) + NA_ROWS - 1 for r in range(STEP_ROWS)]

    n_ext = QB + 2 * POOL_HALO

    def block_rows(blk):
        return slice(blk * QB, (blk + 1) * QB)

    def shift_sum(s, k):
        return pltpu.roll(s, k, axis=0) + pltpu.roll(s, n_ext - k, axis=0)

    def pool_group(blk, g):
        w = POOL_WINDOWS[g]
        cols = slice(g * gdim, (g + 1) * gdim)
        rows = block_rows(blk)
        before = (jnp.where(i > 0, up_ref[:, cols], 0.0) if blk == 0
                  else u_ref[blk * QB - POOL_HALO:blk * QB, cols])
        after = (jnp.where(i < n_blk - 1, un_ref[:, cols], 0.0) if blk == STEP_BLOCKS - 1
                 else u_ref[(blk + 1) * QB:(blk + 1) * QB + POOL_HALO, cols])
        u = u_ref[rows, cols]
        s = jnp.concatenate([before, u, after], axis=0)
        s = s + pltpu.roll(s, 1, axis=0)
        k = 1
        while 2 * k < w:
            s = shift_sum(s, k)
            k *= 2
        s = s[POOL_HALO:POOL_HALO + QB]
        t = i * ST + blk * QB + lax.broadcasted_iota(jnp.int32, (QB, 1), 0)
        lo = jnp.clip(t - w // 2, 0, seq)
        hi = jnp.clip(t - w // 2 + w, 0, seq)
        pooled = s / (hi - lo).astype(F32) - u
        y = jnp.dot(pooled.astype(BF16), wp_ref[g], preferred_element_type=F32)
        y = y * ps_ref[:, cols] * _silu(zp_ref[rows, cols].astype(F32))
        cat_ref[rows, cols] = y.astype(BF16)

    nt_dims = (((1,), (1,)), ((), ()))
    lane = lax.broadcasted_iota(jnp.int32, (1, LANES), 1)
    first_head = lane < HEAD_DIM
    ones_win = jnp.ones((WIN, LANES), BF16)
    ones_ctx = jnp.ones((kc_ref.shape[0], LANES), BF16)
    stacked = 2 * GRID_W

    def chunk(p):
        return slice(p * LANES, (p + 1) * LANES)

    def stack_heads(ref, blk, p):
        qp = ref[block_rows(blk), chunk(p)]
        zero = jnp.zeros_like(qp)
        first, second = jnp.where(first_head, qp, zero), jnp.where(first_head, zero, qp)
        pieces = []
        for a in range(QUERY_ROWS):
            pieces += [first[a * GRID_W:(a + 1) * GRID_W], second[a * GRID_W:(a + 1) * GRID_W]]
        return jnp.concatenate(pieces, axis=0)

    def lane_tiles(a):
        return [a[:, c * LANES:(c + 1) * LANES] for c in range(a.shape[1] // LANES)]

    def scores(blk, p):
        q_rot = stack_heads(qr_ref, blk, p)
        s_loc = []
        for a in range(QUERY_ROWS):
            r = blk * QUERY_ROWS + a
            k_win = k_buf[slot, pl.ds(win_off[r], WIN), chunk(p)]
            bias = jnp.concatenate(
                [jnp.concatenate([bias_ref[2 * p + h, win_dr[r] + jr] for jr in range(0, NA_ROWS, 2)], axis=1)
                 for h in range(2)], axis=0)
            s_loc.append(lax.dot_general(q_rot[a * stacked:(a + 1) * stacked], k_win, nt_dims,
                                         preferred_element_type=F32) + bias)
        s_ctx = lax.dot_general(stack_heads(q_ref, blk, p), kc_ref[:, chunk(p)], nt_dims,
                                preferred_element_type=F32)
        return jnp.concatenate(s_loc, axis=0), s_ctx

    def softmax_numerators(s_loc, s_ctx):
        m = jnp.max(functools.reduce(jnp.maximum, lane_tiles(s_loc) + lane_tiles(s_ctx)), axis=1, keepdims=True)
        return jnp.concatenate([jnp.exp2(s_loc - m), jnp.exp2(s_ctx - m)], axis=1).astype(BF16)

    def weighted_values(blk, p, probs):
        vc_aug = jnp.concatenate([vc_ref[:, chunk(p)], ones_ctx], axis=1)
        outs = []
        for a in range(QUERY_ROWS):
            v_win = v_buf[slot, pl.ds(win_off[blk * QUERY_ROWS + a], WIN), chunk(p)]
            v_aug = jnp.concatenate([jnp.concatenate([v_win, ones_win], axis=1), vc_aug], axis=0)
            outs.append(jnp.dot(probs[a * stacked:(a + 1) * stacked], v_aug, preferred_element_type=F32))
        return outs

    def emit(blk, p, outs):
        num = jnp.concatenate([jnp.where(first_head, o[:GRID_W, :LANES], o[GRID_W:, :LANES]) for o in outs], axis=0)
        den = jnp.concatenate([jnp.where(first_head, o[:GRID_W, LANES:], o[GRID_W:, LANES:]) for o in outs], axis=0)
        out = num / den * _silu(za_ref[block_rows(blk), chunk(p)].astype(F32))
        cat_ref[block_rows(blk), d_pool + p * LANES:d_pool + (p + 1) * LANES] = out.astype(BF16)

    def finish(first_blk):
        rows = slice(first_blk * QB, (first_blk + OPROJ_BLOCKS) * QB)
        y = jnp.dot(cat_ref[rows, :], wo_ref[...], preferred_element_type=F32) + bo_ref[...]
        z = DEEPNORM_ALPHA * x_ref[rows, :] + mod_ref[0][:, 2 * d:3 * d] * y
        o_ref[rows, :] = _layer_norm(z) * g_ref[...] + be_ref[...]

    n_pairs = n_heads // 2
    items = [(blk, p) for blk in range(STEP_BLOCKS) for p in range(n_pairs)]
    s = {n: scores(*items[n]) for n in range(min(2, len(items)))}
    probs = {0: softmax_numerators(*s.pop(0))}
    for n, (blk, p) in enumerate(items):
        if n + 2 < len(items):
            s[n + 2] = scores(*items[n + 2])
        if n + 1 < len(items):
            probs[n + 1] = softmax_numerators(*s.pop(n + 1))
        if POOL_FIRST_PAIR <= p < POOL_FIRST_PAIR + len(POOL_WINDOWS):
            pool_group(blk, p - POOL_FIRST_PAIR)
        emit(blk, p, weighted_values(blk, p, probs.pop(n)))
        if blk > 0 and blk % OPROJ_BLOCKS == 0 and p == 0:
            finish(blk - OPROJ_BLOCKS)
    finish(STEP_BLOCKS - OPROJ_BLOCKS)


def _mix(x2d, mod3, u, zp, za, qr, q, kr, v, kc, vc, bias, w_pool, pool_scale, w_out, b_out, ln_g, ln_b,
         batch, seq, ctx_len):
    t, d = x2d.shape
    n_blk = seq // ST
    halo_per_blk = ST // POOL_HALO
    n_halo = t // POOL_HALO
    d_mix = w_out.shape[0]
    n_rows = seq // GRID_W
    single = pl.Buffered(1)

    def tok(b, i):
        return (b * n_blk + i, 0)

    full =lambda shape: pl.BlockSpec(shape, lambda b, i: (0,) * len(shape), pipeline_mode=single)
    in_specs = [
        pl.BlockSpec((ST, d), tok),
        pl.BlockSpec((1, 1, mod3.shape[2]), lambda b, i: (b, 0, 0)),
        pl.BlockSpec((ST, d), tok),
        pl.BlockSpec((POOL_HALO, d),
                     lambda b, i: (jnp.maximum((b * n_blk + i) * halo_per_blk - 1, 0), 0)),
        pl.BlockSpec((POOL_HALO, d),
                     lambda b, i: (jnp.minimum((b * n_blk + i + 1) * halo_per_blk, n_halo - 1), 0)),
        pl.BlockSpec((ST, d), tok),
        pl.BlockSpec((ST, d), tok),
        pl.BlockSpec((ST, d), tok),
        pl.BlockSpec((ST, d), tok),
    ]
    in_specs += [pl.BlockSpec(memory_space=pl.ANY)] * 2
    in_specs += [
        full(bias.shape),
        pl.BlockSpec((ctx_len, d), lambda b, i: (b, 0)),
        pl.BlockSpec((ctx_len, d), lambda b, i: (b, 0)),
        full(w_pool.shape), full(pool_scale.shape), full(w_out.shape),
        full(b_out.shape), full(ln_g.shape), full(ln_b.shape),
    ]
    args = ([x2d, mod3, u, u, u, zp, za, qr, q, kr, v]
            + [bias, kc, vc, w_pool, pool_scale, w_out, b_out, ln_g, ln_b])
    return pl.pallas_call(
        functools.partial(_mix_kernel, seq),
        grid=(batch, n_blk),
        in_specs=in_specs,
        out_specs=pl.BlockSpec((ST, d), tok),
        out_shape=jax.ShapeDtypeStruct((t, d), F32),
        scratch_shapes=[pltpu.VMEM((ST, d_mix), BF16),
                        pltpu.VMEM((2, UNION_ROWS * GRID_W, d), BF16),
                        pltpu.VMEM((2, UNION_ROWS * GRID_W, d), BF16),
                        pltpu.SemaphoreType.DMA((2, 2))],
        compiler_params=pltpu.CompilerParams(dimension_semantics=("arbitrary", "arbitrary"),
                                             vmem_limit_bytes=VMEM_LIMIT_MIX),
        name="mix",
    )(*args)


def kernel(x, c, ctx, c_ctx, w_ada, b_ada, w_in, b_in, w_pool, pool_scale, rpb, w_out, b_out, ln_g, ln_b):
    batch, seq, d = x.shape
    ctx_len = ctx.shape[1]
    assert w_ada.shape[0] == DEPTH == 1
    assert seq % GRID_W == 0 and seq % INPROJ_TM == 0 and (batch * ctx_len) % INPROJ_TM == 0
    assert seq // GRID_W >= UNION_ROWS and seq % ST == 0
    d_pool = pool_scale.shape[1]
    assert d_pool == d and w_in.shape[2] == 6 * d and rpb.shape[1] * HEAD_DIM == d

    cvec = jnp.zeros((8, d), F32).at[:batch].set(c).at[batch].set(c_ctx)
    mod3 = _ada(cvec, w_ada[0], b_ada[0][None, :]).reshape(8, 1, 3 * d)

    w_in_b = w_in[0]
    b_in2 = b_in[0][None, :]
    x2d = x.reshape(batch * seq, d)
    tiles_per_seq = seq // INPROJ_TM
    u, zp, qr, q, kr, v, za = _inproj(
        x2d, mod3, lambda i: i // tiles_per_seq, w_in_b, b_in2, range(6),
        ("plain", "plain", "q", "k", "plain", "plain"), (F32, BF16, BF16, BF16, BF16, BF16, BF16),
        _rope_tables(seq), INPROJ_TM)
    kc, vc = _inproj(
        ctx.reshape(batch * ctx_len, d), mod3, lambda i: batch, w_in_b, b_in2, (3, 4),
        ("plain", "plain"), (BF16, BF16), None, INPROJ_TM)

    bias = _bias_tables(rpb[0])
    out = _mix(x2d, mod3, u, zp, za, qr, q, kr, v, kc, vc, bias,
               w_pool[0].astype(BF16), pool_scale, w_out[0].astype(BF16), b_out, ln_g, ln_b,
               batch, seq, ctx_len)
    return out.reshape(batch, seq, d)
```

```python
import functools

import numpy as np
import jax
import jax.numpy as jnp
from jax import lax
from jax.experimental import pallas as pl
from jax.experimental.pallas import tpu as pltpu

GRID_W = 64
POOL_WINDOWS = (2, 4, 8, 16)
HEAD_DIM = 64
NA_ROWS = 8
NA_COLS = 16
ROPE_BASE = 10000.0
LN_EPS = 1e-6
LOG2_E = float(np.log2(np.e))
SCORE_SCALE = HEAD_DIM ** -0.5 * LOG2_E
DEPTH = 1
DEEPNORM_ALPHA = (2.0 * DEPTH) ** 0.25

LANES = 128
MIB = 1024 * 1024
VMEM_LIMIT_SMALL = 52 * MIB
VMEM_LIMIT_INPROJ = 58 * MIB
VMEM_LIMIT_MIX = 52 * MIB

QUERY_ROWS = 2
QB = QUERY_ROWS * GRID_W
WIN = NA_ROWS * GRID_W
STEP_BLOCKS = 4
STEP_ROWS = STEP_BLOCKS * QUERY_ROWS
ST = STEP_BLOCKS * QB
OUTPROJ_TM = 1024
OUTPROJ_CHUNK = 256
POOL_FIRST_PAIR = 2
UNION_ROWS = NA_ROWS + STEP_ROWS - 1
POOL_HALO = 8
NEG = -1e30
INPROJ_TM = 512
F32 = jnp.float32
BF16 = jnp.bfloat16


def _silu(x):
    return x * jax.nn.sigmoid(x)


def _layer_norm(x):
    mu = jnp.mean(x, axis=-1, keepdims=True)
    xc = x - mu
    var = jnp.mean(xc * xc, axis=-1, keepdims=True)
    return xc * lax.rsqrt(var + LN_EPS)


def _ada_kernel(c_ref, w_ref, b_ref, o_ref):
    s = _silu(c_ref[...]).astype(BF16)
    o_ref[...] = jnp.dot(s, w_ref[...].astype(BF16), preferred_element_type=F32) + b_ref[...]


def _ada(cvec, w_ada, b_ada):
    rows, d = cvec.shape
    n = w_ada.shape[1]
    tn = 768
    return pl.pallas_call(
        _ada_kernel,
        grid=(n // tn,),
        in_specs=[pl.BlockSpec((rows, d), lambda j: (0, 0)),
                  pl.BlockSpec((d, tn), lambda j: (0, j)),
                  pl.BlockSpec((1, tn), lambda j: (0, j))],
        out_specs=pl.BlockSpec((rows, tn), lambda j: (0, j)),
        out_shape=jax.ShapeDtypeStruct((rows, n), F32),
        compiler_params=pltpu.CompilerParams(dimension_semantics=("arbitrary",),
                                             vmem_limit_bytes=VMEM_LIMIT_SMALL),
        name="ada",
    )(cvec, w_ada, b_ada)


def _rope(x, cos, sin):
    half = HEAD_DIM // 4
    first_half = lax.broadcasted_iota(jnp.int32, (1, LANES), 1) % (2 * half) < half
    outs = []
    for c in range(x.shape[1] // LANES):
        xc = x[:, c * LANES:(c + 1) * LANES]
        partner = jnp.where(first_half, pltpu.roll(xc, LANES - half, axis=1), pltpu.roll(xc, half, axis=1))
        outs.append(xc * cos + partner * sin)
    return jnp.concatenate(outs, axis=1)


def _inproj_kernel(kinds, has_rope, *refs):
    n_blk = len(kinds)
    x0_ref, mod0_ref, xn_ref, modn_ref = refs[:4]
    w_refs = refs[4:4 + n_blk]
    b_refs = refs[4 + n_blk:4 + 2 * n_blk]
    n_in = 4 + 2 * n_blk + (2 if has_rope else 0)
    cos_ref, sin_ref = refs[n_in - 2:n_in] if has_rope else (None, None)
    out_refs = refs[n_in:-1]
    h_ref = refs[-1]
    d = xn_ref.shape[1]
    i = pl.program_id(0)
    slot = i % 2

    def modulated(x_ref, mod_ref):
        mod = mod_ref[0]
        return (_layer_norm(x_ref[...]) * (1.0 + mod[:, d:2 * d]) + mod[:, :d]).astype(BF16)

    @pl.when(i == 0)
    def _():
        h_ref[0] = modulated(x0_ref, mod0_ref)

    o = 0
    for n, (kind, w_ref, b_ref) in enumerate(zip(kinds, w_refs, b_refs)):
        w = w_ref[...].astype(BF16)
        acc = jnp.dot(h_ref[slot], w, preferred_element_type=F32) + b_ref[...]
        if n == 0:
            h_ref[1 - slot] = modulated(xn_ref, modn_ref)
        if kind == "q":
            qr_ref, q_ref = out_refs[o:o + 2]
            o += 2
            qs = acc * SCORE_SCALE
            q_ref[...] = qs.astype(q_ref.dtype)
            qr_ref[...] = _rope(qs, cos_ref[...], sin_ref[...]).astype(qr_ref.dtype)
        else:
            val = _rope(acc, cos_ref[...], sin_ref[...]) if kind == "k" else acc
            out_refs[o][...] = val.astype(out_refs[o].dtype)
            o += 1


def _inproj(x2d, mod3, mod_rows_per_tile, w_bf16, b_in, col_blocks, kinds, out_dtypes, rope, tm):
    t, d = x2d.shape
    nt = t // tm
    has_rope = rope is not None
    resident = pl.Buffered(1)

    def next_tile(i):
        return jnp.minimum(i + 1, nt - 1)

    mod_block = (1, 1, mod3.shape[2])
    in_specs = [
        pl.BlockSpec((tm, d), lambda i: (0, 0), pipeline_mode=resident),
        pl.BlockSpec(mod_block, lambda i: (mod_rows_per_tile(0), 0, 0), pipeline_mode=resident),
        pl.BlockSpec((tm, d), lambda i: (next_tile(i), 0)),
        pl.BlockSpec(mod_block, lambda i: (mod_rows_per_tile(next_tile(i)), 0, 0)),
    ]
    in_specs += [pl.BlockSpec((d, d), lambda i, c=c: (0, c), pipeline_mode=resident) for c in col_blocks]
    in_specs += [pl.BlockSpec((1, d), lambda i, c=c: (0, c), pipeline_mode=resident) for c in col_blocks]
    args = [x2d, mod3, x2d, mod3] + [w_bf16] * len(col_blocks) + [b_in] * len(col_blocks)
    if has_rope:
        tiles_per_seq = rope[0].shape[0] // tm
        for tab in rope:
            in_specs.append(pl.BlockSpec((tm, LANES), lambda i: (i % tiles_per_seq, 0)))
            args.append(tab)
    out_specs = [pl.BlockSpec((tm, d), lambda i: (i, 0)) for _ in out_dtypes]
    out_shape = [jax.ShapeDtypeStruct((t, d), dt) for dt in out_dtypes]
    return pl.pallas_call(
        functools.partial(_inproj_kernel, tuple(kinds), has_rope),
        grid=(nt,),
        in_specs=in_specs,
        out_specs=out_specs,
        out_shape=out_shape,
        scratch_shapes=[pltpu.VMEM((2, tm, d), BF16)],
        compiler_params=pltpu.CompilerParams(
            dimension_semantics=("arbitrary",),
            vmem_limit_bytes=VMEM_LIMIT_INPROJ if len(col_blocks) > 2 else VMEM_LIMIT_SMALL),
        name="inproj_rope" if has_rope else "inproj_ctx",
    )(*args)


def _rope_tables(seq):
    half = HEAD_DIM // 4
    inv_freq = ROPE_BASE ** (-np.arange(half, dtype=np.float64) / half)
    t = np.arange(seq)
    row, col = t // GRID_W, t % GRID_W
    lane = np.arange(LANES)
    pos = np.where((lane % HEAD_DIM < HEAD_DIM // 2)[None, :], row[:, None], col[:, None]).astype(np.float64)
    ang = pos * inv_freq[lane % half][None, :]
    sign = np.where(lane % (2 * half) < half, -1.0, 1.0)[None, :]
    return jnp.asarray(np.cos(ang), F32), jnp.asarray(np.sin(ang) * sign, F32)


def _window_row_start(r, n_rows):
    return jnp.clip(r - NA_ROWS // 2, 0, n_rows - NA_ROWS)


def _bias_kernel(rpb_ref, o_ref):
    lane = lax.broadcasted_iota(jnp.int32, (GRID_W, LANES), 1)
    qc = lax.broadcasted_iota(jnp.int32, (GRID_W, LANES), 0)
    kc = lane % GRID_W
    cs = jnp.clip(qc - NA_COLS // 2, 0, GRID_W - NA_COLS)
    col_ok = (kc >= cs) & (kc < cs + NA_COLS)
    low_half = lane < GRID_W

    for h in range(o_ref.shape[0]):
        table = rpb_ref[h] * LOG2_E

        @functools.cache
        def shifted(dr, half, table=table):
            row = jnp.broadcast_to(table[dr:dr + 1, :], (GRID_W, LANES))
            shift = (LANES - (NA_COLS - 1) + GRID_W * half) % LANES
            return pltpu.roll(row, shift, axis=1, stride=1, stride_axis=0)

        for i in range(o_ref.shape[1]):
            o_ref[h, i] = jnp.where(col_ok, jnp.where(low_half, shifted(i, 0), shifted(i + 1, 1)), NEG)


def _bias_tables(rpb):
    n_heads, n_dr, n_dc = rpb.shape
    assert n_dr == 2 * NA_ROWS - 1 and n_dc == 2 * NA_COLS - 1
    rpb_pad = jnp.pad(rpb, ((0, 0), (0, 16 - n_dr), (0, LANES - n_dc)))
    n_tiles = n_dr - 1
    heads_per_step = 8
    assert n_heads % heads_per_step == 0
    return pl.pallas_call(
        _bias_kernel,
        grid=(n_heads // heads_per_step,),
        in_specs=[pl.BlockSpec((heads_per_step, 16, LANES), lambda h: (h, 0, 0))],
        out_specs=pl.BlockSpec((heads_per_step, n_tiles, GRID_W, LANES), lambda h: (h, 0, 0, 0)),
        out_shape=jax.ShapeDtypeStruct((n_heads, n_tiles, GRID_W, LANES), F32),
        compiler_params=pltpu.CompilerParams(dimension_semantics=("arbitrary",),
                                             vmem_limit_bytes=VMEM_LIMIT_SMALL),
        name="bias_tables",
    )(rpb_pad)


def _union_row_start(i, n_rows):
    return jnp.minimum(_window_row_start(i * STEP_ROWS, n_rows), n_rows - UNION_ROWS)


def _mix_kernel(seq, u_ref, up_ref, un_ref, zp_ref, za_ref, qr_ref, q_ref, k_hbm, v_hbm,
                bias_ref, kc_ref, vc_ref, wp_ref, ps_ref, cat_ref, k_buf, v_buf, win_sem):
    d_pool = ps_ref.shape[1]
    gdim = d_pool // len(POOL_WINDOWS)
    n_heads = (cat_ref.shape[1] - d_pool) // HEAD_DIM
    n_rows = seq // GRID_W
    i = pl.program_id(1)
    n_blk = pl.num_programs(1)

    step = pl.program_id(0) * n_blk + i
    n_steps = pl.num_programs(0) * n_blk
    slot = step % 2

    def window_copies(s, into):
        tok0 = pl.multiple_of((s // n_blk) * seq + _union_row_start(s % n_blk, n_rows) * GRID_W, GRID_W)
        return [pltpu.make_async_copy(hbm.at[pl.ds(tok0, UNION_ROWS * GRID_W)], buf.at[into], win_sem.at[n, into])
                for n, (hbm, buf) in enumerate(((k_hbm, k_buf), (v_hbm, v_buf)))]

    @pl.when(step == 0)
    def _():
        for cp in window_copies(step, slot):
            cp.start()

    for cp in window_copies(step, slot):
        cp.wait()

    @pl.when(step + 1 < n_steps)
    def _():
        for cp in window_copies(step + 1, 1 - slot):
            cp.start()

    win_row = [_window_row_start(i * STEP_ROWS + r, n_rows) for r in range(STEP_ROWS)]
    win_off = [pl.multiple_of((win_row[r] - _union_row_start(i, n_rows)) * GRID_W, GRID_W) for r in range(STEP_ROWS)]
    win_dr = [win_row[r] - (i * STEP_ROWS + r) + NA_ROWS - 1 for r in range(STEP_ROWS)]

    n_ext = QB + 2 * POOL_HALO

    def block_rows(blk):
        return slice(blk * QB, (blk + 1) * QB)

    def shift_sum(s, k):
        return pltpu.roll(s, k, axis=0) + pltpu.roll(s, n_ext - k, axis=0)

    def pool_group(blk, g):
        w = POOL_WINDOWS[g]
        cols = slice(g * gdim, (g + 1) * gdim)
        rows = block_rows(blk)
        before = (jnp.where(i > 0, up_ref[:, cols], 0.0) if blk == 0
                  else u_ref[blk * QB - POOL_HALO:blk * QB, cols])
        after = (jnp.where(i < n_blk - 1, un_ref[:, cols], 0.0) if blk == STEP_BLOCKS - 1
                 else u_ref[(blk + 1) * QB:(blk + 1) * QB + POOL_HALO, cols])
        u = u_ref[rows, cols]
        s = jnp.concatenate([before, u, after], axis=0)
        s = s + pltpu.roll(s, 1, axis=0)
        k = 1
        while 2 * k < w:
            s = shift_sum(s, k)
            k *= 2
        s = s[POOL_HALO:POOL_HALO + QB]
        t = i * ST + blk * QB + lax.broadcasted_iota(jnp.int32, (QB, 1), 0)
        lo = jnp.clip(t - w // 2, 0, seq)
        hi = jnp.clip(t - w // 2 + w, 0, seq)
        pooled = s / (hi - lo).astype(F32) - u
        y = jnp.dot(pooled.astype(BF16), wp_ref[g], preferred_element_type=F32)
        y = y * ps_ref[:, cols] * _silu(zp_ref[rows, cols].astype(F32))
        cat_ref[rows, cols] = y.astype(BF16)

    nt_dims = (((1,), (1,)), ((), ()))
    lane = lax.broadcasted_iota(jnp.int32, (1, LANES), 1)
    first_head = lane < HEAD_DIM
    ones_win = jnp.ones((WIN, LANES), BF16)
    ones_ctx = jnp.ones((kc_ref.shape[0], LANES), BF16)
    stacked = 2 * GRID_W

    def chunk(p):
        return slice(p * LANES, (p + 1) * LANES)

    def stack_heads(ref, blk, p):
        qp = ref[block_rows(blk), chunk(p)]
        zero = jnp.zeros_like(qp)
        first, second = jnp.where(first_head, qp, zero), jnp.where(first_head, zero, qp)
        pieces = []
        for a in range(QUERY_ROWS):
            pieces += [first[a * GRID_W:(a + 1) * GRID_W], second[a * GRID_W:(a + 1) * GRID_W]]
        return jnp.concatenate(pieces, axis=0)

    def lane_tiles(a):
        return [a[:, c * LANES:(c + 1) * LANES] for c in range(a.shape[1] // LANES)]

    def scores(blk, p):
        q_rot = stack_heads(qr_ref, blk, p)
        s_loc = []
        for a in range(QUERY_ROWS):
            r = blk * QUERY_ROWS + a
            k_win = k_buf[slot, pl.ds(win_off[r], WIN), chunk(p)]
            bias = jnp.concatenate(
                [jnp.concatenate([bias_ref[2 * p + h, win_dr[r] + jr] for jr in range(0, NA_ROWS, 2)], axis=1)
                 for h in range(2)], axis=0)
            s_loc.append(lax.dot_general(q_rot[a * stacked:(a + 1) * stacked], k_win, nt_dims,
                                         preferred_element_type=F32) + bias)
        s_ctx = lax.dot_general(stack_heads(q_ref, blk, p), kc_ref[:, chunk(p)], nt_dims,
                                preferred_element_type=F32)
        return jnp.concatenate(s_loc, axis=0), s_ctx

    def softmax_numerators(s_loc, s_ctx):
        m = jnp.max(functools.reduce(jnp.maximum, lane_tiles(s_loc) + lane_tiles(s_ctx)), axis=1, keepdims=True)
        return jnp.concatenate([jnp.exp2(s_loc - m), jnp.exp2(s_ctx - m)], axis=1).astype(BF16)

    def weighted_values(blk, p, probs):
        vc_aug = jnp.concatenate([vc_ref[:, chunk(p)], ones_ctx], axis=1)
        outs = []
        for a in range(QUERY_ROWS):
            v_win = v_buf[slot, pl.ds(win_off[blk * QUERY_ROWS + a], WIN), chunk(p)]
            v_aug = jnp.concatenate([jnp.concatenate([v_win, ones_win], axis=1), vc_aug], axis=0)
            outs.append(jnp.dot(probs[a * stacked:(a + 1) * stacked], v_aug, preferred_element_type=F32))
        return outs

    def emit(blk, p, outs):
        num = jnp.concatenate([jnp.where(first_head, o[:GRID_W, :LANES], o[GRID_W:, :LANES]) for o in outs], axis=0)
        den = jnp.concatenate([jnp.where(first_head, o[:GRID_W, LANES:], o[GRID_W:, LANES:]) for o in outs], axis=0)
        out = num / den * _silu(za_ref[block_rows(blk), chunk(p)].astype(F32))
        cat_ref[block_rows(blk), d_pool + p * LANES:d_pool + (p + 1) * LANES] = out.astype(BF16)

    n_pairs = n_heads // 2
    items = [(blk, p) for blk in range(STEP_BLOCKS) for p in range(n_pairs)]
    s = {n: scores(*items[n]) for n in range(min(2, len(items)))}
    probs = {0: softmax_numerators(*s.pop(0))}
    for n, (blk, p) in enumerate(items):
        if n + 2 < len(items):
            s[n + 2] = scores(*items[n + 2])
        if n + 1 < len(items):
            probs[n + 1] = softmax_numerators(*s.pop(n + 1))
        if POOL_FIRST_PAIR <= p < POOL_FIRST_PAIR + len(POOL_WINDOWS):
            pool_group(blk, p - POOL_FIRST_PAIR)
        emit(blk, p, weighted_values(blk, p, probs.pop(n)))


def _outproj_kernel(cat_ref, x_ref, mod_ref, wo_ref, bo_ref, g_ref, be_ref, o_ref):
    d = x_ref.shape[1]
    gate = mod_ref[0][:, 2 * d:3 * d]
    chunks = [slice(r, r + OUTPROJ_CHUNK) for r in range(0, cat_ref.shape[0], OUTPROJ_CHUNK)]

    def project(rows):
        return jnp.dot(cat_ref[rows, :], wo_ref[...], preferred_element_type=F32) + bo_ref[...]

    def normalise(rows, y):
        z = DEEPNORM_ALPHA * x_ref[rows, :] + gate * y
        o_ref[rows, :] = _layer_norm(z) * g_ref[...] + be_ref[...]

    y_prev = project(chunks[0])
    for prev, rows in zip(chunks, chunks[1:]):
        y = project(rows)
        normalise(prev, y_prev)
        y_prev = y
    normalise(chunks[-1], y_prev)


def _outproj(cat, x2d, mod3, w_out, b_out, ln_g, ln_b, seq, tm):
    t, d = x2d.shape
    tiles_per_seq = seq // tm
    full = lambda shape: pl.BlockSpec(shape, lambda i: (0,) * len(shape), pipeline_mode=pl.Buffered(1))
    return pl.pallas_call(
        _outproj_kernel,
        grid=(t // tm,),
        in_specs=[pl.BlockSpec((tm, cat.shape[1]), lambda i: (i, 0)),
                  pl.BlockSpec((tm, d), lambda i: (i, 0)),
                  pl.BlockSpec((1, 1, mod3.shape[2]), lambda i: (i // tiles_per_seq, 0, 0)),
                  full(w_out.shape), full(b_out.shape), full(ln_g.shape), full(ln_b.shape)],
        out_specs=pl.BlockSpec((tm, d), lambda i: (i, 0)),
        out_shape=jax.ShapeDtypeStruct((t, d), F32),
        compiler_params=pltpu.CompilerParams(dimension_semantics=("arbitrary",),
                                             vmem_limit_bytes=VMEM_LIMIT_SMALL),
        name="outproj_ln",
    )(cat, x2d, mod3, w_out, b_out, ln_g, ln_b)


def _mix(u, zp, za, qr, q, kr, v, kc, vc, bias, w_pool, pool_scale, d_mix, batch, seq, ctx_len):
    t, d = u.shape
    n_blk = seq // ST
    halo_per_blk = ST // POOL_HALO
    n_halo = t // POOL_HALO
    single = pl.Buffered(1)

    def tok(b, i):
        return (b * n_blk + i, 0)

    full =lambda shape: pl.BlockSpec(shape, lambda b, i: (0,) * len(shape), pipeline_mode=single)
    in_specs = [
        pl.BlockSpec((ST, d), tok),
        pl.BlockSpec((POOL_HALO, d),
                     lambda b, i: (jnp.maximum((b * n_blk + i) * halo_per_blk - 1, 0), 0)),
        pl.BlockSpec((POOL_HALO, d),
                     lambda b, i: (jnp.minimum((b * n_blk + i + 1) * halo_per_blk, n_halo - 1), 0)),
        pl.BlockSpec((ST, d), tok),
        pl.BlockSpec((ST, d), tok),
        pl.BlockSpec((ST, d), tok),
        pl.BlockSpec((ST, d), tok),
    ]
    in_specs += [pl.BlockSpec(memory_space=pl.ANY)] * 2
    in_specs += [
        full(bias.shape),
        pl.BlockSpec((ctx_len, d), lambda b, i: (b, 0)),
        pl.BlockSpec((ctx_len, d), lambda b, i: (b, 0)),
        full(w_pool.shape), full(pool_scale.shape),
    ]
    args = [u, u, u, zp, za, qr, q, kr, v, bias, kc, vc, w_pool, pool_scale]
    return pl.pallas_call(
        functools.partial(_mix_kernel, seq),
        grid=(batch, n_blk),
        in_specs=in_specs,
        out_specs=pl.BlockSpec((ST, d_mix), tok),
        out_shape=jax.ShapeDtypeStruct((t, d_mix), BF16),
        scratch_shapes=[pltpu.VMEM((2, UNION_ROWS * GRID_W, d), BF16),
                        pltpu.VMEM((2, UNION_ROWS * GRID_W, d), BF16),
                        pltpu.SemaphoreType.DMA((2, 2))],
        compiler_params=pltpu.CompilerParams(dimension_semantics=("arbitrary", "arbitrary"),
                                             vmem_limit_bytes=VMEM_LIMIT_MIX),
        name="mix",
    )(*args)


def kernel(x, c, ctx, c_ctx, w_ada, b_ada, w_in, b_in, w_pool, pool_scale, rpb, w_out, b_out, ln_g, ln_b):
    batch, seq, d = x.shape
    ctx_len = ctx.shape[1]
    assert w_ada.shape[0] == DEPTH == 1
    assert seq % GRID_W == 0 and seq % INPROJ_TM == 0 and (batch * ctx_len) % INPROJ_TM == 0
    assert seq // GRID_W >= UNION_ROWS and seq % ST == 0
    d_pool = pool_scale.shape[1]
    assert d_pool == d and w_in.shape[2] == 6 * d and rpb.shape[1] * HEAD_DIM == d

    cvec = jnp.zeros((8, d), F32).at[:batch].set(c).at[batch].set(c_ctx)
    mod3 = _ada(cvec, w_ada[0], b_ada[0][None, :]).reshape(8, 1, 3 * d)

    w_in_b = w_in[0]
    b_in2 = b_in[0][None, :]
    x2d = x.reshape(batch * seq, d)
    tiles_per_seq = seq // INPROJ_TM
    u, zp, qr, q, kr, v, za = _inproj(
        x2d, mod3, lambda i: i // tiles_per_seq, w_in_b, b_in2, range(6),
        ("plain", "plain", "q", "k", "plain", "plain"), (F32, BF16, BF16, BF16, BF16, BF16, BF16),
        _rope_tables(seq), INPROJ_TM)
    kc, vc = _inproj(
        ctx.reshape(batch * ctx_len, d), mod3, lambda i: batch, w_in_b, b_in2, (3, 4),
        ("plain", "plain"), (BF16, BF16), None, INPROJ_TM)

    bias = _bias_tables(rpb[0])
    w_out_b = w_out[0].astype(BF16)
    cat = _mix(u, zp, za, qr, q, kr, v, kc, vc, bias, w_pool[0].astype(BF16), pool_scale,
               w_out_b.shape[0], batch, seq, ctx_len)
    out = _outproj(cat, x2d, mod3, w_out_b, b_out, ln_g, ln_b, seq, OUTPROJ_TM)
    return out.reshape(batch, seq, d)
```
